```python
import math
import jax, jax.numpy as jnp
from jax import lax
import numpy as np

D_MODEL = 2048
BATCH = 8
SEQ = 2048
DEPTH = 2

GRID_W = 64
CTX_LEN = 256
EPS = 1e-6

BRANCH_WIDTH = 1024
N_BRANCHES = 3
DIFF_HEADS = 8
DIFF_HEAD_DIM = 64
DIFF_V_DIM = 2 * DIFF_HEAD_DIM
ROPE_BASE = 10000.0
ROPE_FREQ = DIFF_HEAD_DIM // 4
Q_BLOCK = 128
LRU_WIDTH = BRANCH_WIDTH
LRU_BLOCKS = 8
LRU_BLOCK_DIM = LRU_WIDTH // LRU_BLOCKS
LRU_CONV_W = 4
CONV_LEFT = LRU_CONV_W // 2
LRU_C = 8.0
NA_HEADS = 8
NA_HEAD_DIM = BRANCH_WIDTH // NA_HEADS
NA_WIN_R = 8
NA_WIN_C = 16
N_IN_PARTS = 8
IN_WIDTH = N_IN_PARTS * BRANCH_WIDTH
FFN_DENSE = 5504
N_EXPERTS = 8
TOP_K = 2
FFN_EXPERT = 7168
N_DENSE_LAYERS = (DEPTH + 1) // 2
N_MOE_LAYERS = DEPTH // 2

kernel_name = 'hybrid_diffattn_rglru_natten_moe_dit'


def rmsnorm(x, g):
    xf = x.astype(jnp.float32)
    y = xf * lax.rsqrt(jnp.mean(xf * xf, axis=-1, keepdims=True) + EPS)
    return (y * g.astype(jnp.float32)).astype(x.dtype)


def axial_rope(n_tok):
    inv = ROPE_BASE ** (-jnp.arange(ROPE_FREQ, dtype=jnp.float32) / ROPE_FREQ)
    t = jnp.arange(n_tok, dtype=jnp.int32)
    pos = jnp.stack([t // GRID_W, t % GRID_W], axis=-1).astype(jnp.float32)
    ang = pos[:, :, None] * inv
    return jnp.cos(ang), jnp.sin(ang)


def apply_rope(x, cos, sin):
    shp = x.shape
    xf = x.astype(jnp.float32).reshape(shp[:-1] + (2, 2, ROPE_FREQ))
    x1, x2 = xf[..., 0, :], xf[..., 1, :]
    cb, sb = cos[None, :, None, None], sin[None, :, None, None]
    out = jnp.stack([x1 * cb - x2 * sb, x2 * cb + x1 * sb], axis=-2)
    return out.reshape(shp).astype(x.dtype)


def diff_attend(q, k, v, lam):
    s = jnp.einsum('bqhcd,bkhcd->bhcqk', q, k).astype(jnp.float32) * (DIFF_HEAD_DIM ** -0.5)
    p = jax.nn.softmax(s, axis=-1)
    a = (p[:, :, 0] - lam * p[:, :, 1]).astype(v.dtype)
    return jnp.einsum('bhqk,bkhv->bqhv', a, v)


def dense_attend(q, k, v):
    s = jnp.einsum('bqhd,bkhd->bhqk', q, k).astype(jnp.float32) * (q.shape[-1] ** -0.5)
    p = jax.nn.softmax(s, axis=-1).astype(v.dtype)
    return jnp.einsum('bhqk,bkhd->bqhd', p, v)


def short_conv(x, w, b):
    t = x.shape[1]
    xp = jnp.pad(x, ((0, 0), (CONV_LEFT, LRU_CONV_W - 1 - CONV_LEFT), (0, 0)))
    y = b
    for i in range(LRU_CONV_W):
        y = y + xp[:, i:i + t] * w[i]
    return y


def lru_coeffs(x, w, b, lam):
    bsz, t, wd = x.shape
    xb = x.reshape(bsz, t, LRU_BLOCKS, LRU_BLOCK_DIM)
    g = jnp.einsum('btnc,gncf->gbtnf', xb, w).reshape(2, bsz, t, wd) + b[:, None, None, :]
    g = g.astype(jnp.float32)
    r = jax.nn.sigmoid(g[0])
    i = jax.nn.sigmoid(g[1])
    log_a = -LRU_C * r * jax.nn.softplus(-lam.astype(jnp.float32))
    a = jnp.exp(log_a)
    beta = jnp.sqrt(-jnp.expm1(2.0 * log_a))
    return a, beta * i * x.astype(jnp.float32)


def linear_scan(a, b, h0, reverse):
    def comb(l, r):
        return l[0] * r[0], r[0] * l[1] + r[1]
    a_cum, b_cum = lax.associative_scan(comb, (a, b), axis=1, reverse=reverse)
    return a_cum * h0[:, None, :] + b_cum


def na_latent(q, k, v, k_ctx, v_ctx, rpb):
    bsz, s_len, h, d = q.shape
    rows = s_len // GRID_W
    kr = min(NA_WIN_R, rows)
    kc = NA_WIN_C
    nw = kr * kc
    col = jnp.arange(GRID_W)
    cs = jnp.clip(col - kc // 2, 0, GRID_W - kc)
    win_col = cs[:, None] + jnp.arange(kc)
    dc = win_col - col[:, None]
    qg = q.reshape(bsz, rows, GRID_W, h, d)
    scale = d ** -0.5

    def one_row(r):
        rs = jnp.clip(r - kr // 2, 0, rows - kr)
        win_row = rs + jnp.arange(kr)
        tok = (win_row[None, :, None] * GRID_W + win_col[:, None, :]).reshape(GRID_W, nw)
        kw = k[:, tok]
        vw = v[:, tok]
        qr = lax.dynamic_index_in_dim(qg, r, axis=1, keepdims=False)
        dr = win_row - r
        bias = rpb[:, dr[None, :, None] + (NA_WIN_R - 1), dc[:, None, :] + (NA_WIN_C - 1)]
        bias = bias.reshape(h, GRID_W, nw).astype(jnp.float32)
        s_win = jnp.einsum('bqhd,bqkhd->bhqk', qr, kw).astype(jnp.float32) * scale + bias
        s_ctx = jnp.einsum('bqhd,bchd->bhqc', qr, k_ctx).astype(jnp.float32) * scale
        p = jax.nn.softmax(jnp.concatenate([s_win, s_ctx], axis=-1), axis=-1).astype(v.dtype)
        return (jnp.einsum('bhqk,bqkhd->bqhd', p[..., :nw], vw)
                + jnp.einsum('bhqc,bchd->bqhd', p[..., nw:], v_ctx))

    out = lax.map(one_row, jnp.arange(rows))
    return jnp.transpose(out, (1, 0, 2, 3, 4)).reshape(bsz, s_len, h * d)


def merge_branches(h, ys, w_branch, w_merge, b_merge, w_out):
    g = jax.nn.sigmoid((h @ w_merge + b_merge).astype(jnp.float32)).astype(h.dtype)
    gs = jnp.split(g, N_BRANCHES, axis=-1)
    m = gs[0] * (ys[0] @ w_branch[0]) + gs[1] * (ys[1] @ w_branch[1]) + gs[2] * (ys[2] @ w_branch[2])
    return m @ w_out


def token_mixer(h_lat, h_ctx, cos, sin, w_in, lam_vecs, lam_init, subln_g, conv_w, conv_b,
                gate_w, gate_b, lru_lam, rpb, w_branch, w_merge, b_merge, w_out, need_ctx):
    bsz, s_len, _ = h_lat.shape
    c_len = h_ctx.shape[1]
    dq, dk, dv, lx, lg, nq, nk, nv = jnp.split(h_lat @ w_in, N_IN_PARTS, axis=-1)
    cdq, cdk, cdv, clx, clg, cnq, cnk, cnv = jnp.split(h_ctx @ w_in, N_IN_PARTS, axis=-1)

    lv = lam_vecs.astype(jnp.float32)
    lam = jnp.exp(jnp.sum(lv[0] * lv[1])) - jnp.exp(jnp.sum(lv[2] * lv[3])) + lam_init
    q = apply_rope(dq.reshape(bsz, s_len, DIFF_HEADS, 2, DIFF_HEAD_DIM), cos, sin)
    k = apply_rope(dk.reshape(bsz, s_len, DIFF_HEADS, 2, DIFF_HEAD_DIM), cos, sin)
    v = dv.reshape(bsz, s_len, DIFF_HEADS, DIFF_V_DIM)
    ck = cdk.reshape(bsz, c_len, DIFF_HEADS, 2, DIFF_HEAD_DIM)
    cv = cdv.reshape(bsz, c_len, DIFF_HEADS, DIFF_V_DIM)
    keys = jnp.concatenate([k, ck], axis=1)
    vals = jnp.concatenate([v, cv], axis=1)
    qb = jnp.swapaxes(q.reshape(bsz, s_len // Q_BLOCK, Q_BLOCK, DIFF_HEADS, 2, DIFF_HEAD_DIM), 0, 1)
    o = lax.map(lambda qi: diff_attend(qi, keys, vals, lam), qb)
    o = jnp.swapaxes(o, 0, 1).reshape(bsz, s_len, DIFF_HEADS, DIFF_V_DIM)
    y_diff = (rmsnorm(o, subln_g) * (1.0 - lam_init)).reshape(bsz, s_len, BRANCH_WIDTH)

    xl = short_conv(lx, conv_w, conv_b)
    xc = short_conv(clx, conv_w, conv_b)
    h_lat_sum = jnp.zeros((bsz, s_len, LRU_WIDTH), jnp.float32)
    h_ctx_sum = jnp.zeros((bsz, c_len, LRU_WIDTH), jnp.float32)
    for d in range(2):
        rev = d == 1
        a_c, b_c = lru_coeffs(xc, gate_w[d], gate_b[d], lru_lam[d])
        hc = linear_scan(a_c, b_c, jnp.zeros((bsz, LRU_WIDTH), jnp.float32), rev)
        h0 = hc[:, 0] if rev else hc[:, -1]
        a_l, b_l = lru_coeffs(xl, gate_w[d], gate_b[d], lru_lam[d])
        h_lat_sum = h_lat_sum + linear_scan(a_l, b_l, h0, rev)
        h_ctx_sum = h_ctx_sum + hc
    y_lru = h_lat_sum.astype(h_lat.dtype) * jax.nn.gelu(lg)

    nk_ctx = cnk.reshape(bsz, c_len, NA_HEADS, NA_HEAD_DIM)
    nv_ctx = cnv.reshape(bsz, c_len, NA_HEADS, NA_HEAD_DIM)
    y_na = na_latent(nq.reshape(bsz, s_len, NA_HEADS, NA_HEAD_DIM),
                     nk.reshape(bsz, s_len, NA_HEADS, NA_HEAD_DIM),
                     nv.reshape(bsz, s_len, NA_HEADS, NA_HEAD_DIM), nk_ctx, nv_ctx, rpb)

    out_lat = merge_branches(h_lat, (y_diff, y_lru, y_na), w_branch, w_merge, b_merge, w_out)
    if not need_ctx:
        return out_lat, None

    cq = cdq.reshape(bsz, c_len, DIFF_HEADS, 2, DIFF_HEAD_DIM)
    oc = diff_attend(cq, ck, cv, lam)
    y_diff_c = (rmsnorm(oc, subln_g) * (1.0 - lam_init)).reshape(bsz, c_len, BRANCH_WIDTH)
    y_lru_c = h_ctx_sum.astype(h_ctx.dtype) * jax.nn.gelu(clg)
    y_na_c = dense_attend(cnq.reshape(bsz, c_len, NA_HEADS, NA_HEAD_DIM), nk_ctx, nv_ctx)
    y_na_c = y_na_c.reshape(bsz, c_len, BRANCH_WIDTH)
    out_ctx = merge_branches(h_ctx, (y_diff_c, y_lru_c, y_na_c), w_branch, w_merge, b_merge, w_out)
    return out_lat, out_ctx


def swiglu(x, w_gate, w_up, w_down):
    return (jax.nn.silu(x @ w_gate) * (x @ w_up)) @ w_down


def moe_swiglu(x, w_router, w_gate, w_up, w_down):
    logits = (x @ w_router).astype(jnp.float32)
    top_val, top_idx = lax.top_k(logits, TOP_K)
    wts = jax.nn.softmax(top_val, axis=-1)
    combine = jnp.sum(jax.nn.one_hot(top_idx, N_EXPERTS, dtype=jnp.float32) * wts[..., None], axis=-2)
    combine = combine.astype(x.dtype)
    y = jnp.zeros_like(x)
    for e in range(N_EXPERTS):
        y = y + combine[..., e:e + 1] * swiglu(x, w_gate[e], w_up[e], w_down[e])
    return y


def channel_mixer(h, l, ffn_w_gate, ffn_w_up, ffn_w_down, moe_w_router, moe_w_gate, moe_w_up, moe_w_down):
    i = l // 2
    if l % 2 == 0:
        return swiglu(h, ffn_w_gate[i], ffn_w_up[i], ffn_w_down[i])
    return moe_swiglu(h, moe_w_router[i], moe_w_gate[i], moe_w_up[i], moe_w_down[i])


def setup_inputs(seed: int = 0) -> dict:
    key = jax.random.key(seed)
    ks = jax.random.split(key, 32)
    f32 = jnp.float32
    D = D_MODEL

    def nrm(k, shape, scale):
        return jax.random.normal(k, shape, f32) * scale

    a_init = jax.random.uniform(ks[14], (DEPTH, 2, LRU_WIDTH), f32, 0.9, 0.999)
    return {
        'x': nrm(ks[0], (BATCH, SEQ, D), 1.0),
        'c': nrm(ks[1], (BATCH, D), 1.0),
        'ctx': nrm(ks[2], (BATCH, CTX_LEN, D), 1.0),
        'c_ctx': nrm(ks[3], (D,), 1.0),
        'norm_g': 1.0 + nrm(ks[4], (DEPTH, 4, D), 0.05),
        'w_ada': nrm(ks[5], (DEPTH, D, 6 * D), D ** -0.5),
        'b_ada': nrm(ks[6], (DEPTH, 6 * D), 0.02),
        'w_in': nrm(ks[7], (DEPTH, D, IN_WIDTH), D ** -0.5),
        'diff_lambda': nrm(ks[8], (DEPTH, 4, DIFF_HEAD_DIM), 0.1),
        'diff_subln_g': 1.0 + nrm(ks[9], (DEPTH, DIFF_V_DIM), 0.05),
        'lru_conv_w': nrm(ks[10], (DEPTH, LRU_CONV_W, LRU_WIDTH), LRU_CONV_W ** -0.5),
        'lru_conv_b': nrm(ks[11], (DEPTH, LRU_WIDTH), 0.02),
        'lru_gate_w': nrm(ks[12], (DEPTH, 2, 2, LRU_BLOCKS, LRU_BLOCK_DIM, LRU_BLOCK_DIM), LRU_BLOCK_DIM ** -0.5),
        'lru_gate_b': nrm(ks[13], (DEPTH, 2, 2, LRU_WIDTH), 0.02),
        'lru_lambda': jnp.log(a_init) - jnp.log1p(-a_init),
        'na_rpb': nrm(ks[15], (DEPTH, NA_HEADS, 2 * NA_WIN_R - 1, 2 * NA_WIN_C - 1), 0.1),
        'w_branch': nrm(ks[16], (DEPTH, N_BRANCHES, BRANCH_WIDTH, D), BRANCH_WIDTH ** -0.5),
        'w_merge': nrm(ks[17], (DEPTH, D, N_BRANCHES * D), D ** -0.5),
        'b_merge': nrm(ks[18], (DEPTH, N_BRANCHES * D), 0.02),
        'w_out': nrm(ks[19], (DEPTH, D, D), D ** -0.5),
        'ffn_w_gate': nrm(ks[20], (N_DENSE_LAYERS, D, FFN_DENSE), D ** -0.5),
        'ffn_w_up': nrm(ks[21], (N_DENSE_LAYERS, D, FFN_DENSE), D ** -0.5),
        'ffn_w_down': nrm(ks[22], (N_DENSE_LAYERS, FFN_DENSE, D), FFN_DENSE ** -0.5),
        'moe_w_router': nrm(ks[23], (N_MOE_LAYERS, D, N_EXPERTS), D ** -0.5),
        'moe_w_gate': nrm(ks[24], (N_MOE_LAYERS, N_EXPERTS, D, FFN_EXPERT), D ** -0.5),
        'moe_w_up': nrm(ks[25], (N_MOE_LAYERS, N_EXPERTS, D, FFN_EXPERT), D ** -0.5),
        'moe_w_down': nrm(ks[26], (N_MOE_LAYERS, N_EXPERTS, FFN_EXPERT, D), FFN_EXPERT ** -0.5),
    }


def reference(x, c, ctx, c_ctx, norm_g, w_ada, b_ada, w_in, diff_lambda, diff_subln_g,
              lru_conv_w, lru_conv_b, lru_gate_w, lru_gate_b, lru_lambda, na_rpb,
              w_branch, w_merge, b_merge, w_out, ffn_w_gate, ffn_w_up, ffn_w_down,
              moe_w_router, moe_w_gate, moe_w_up, moe_w_down):
    s_len = x.shape[1]
    cos, sin = axial_rope(s_len)
    silu_c = jax.nn.silu(c)
    silu_cc = jax.nn.silu(c_ctx)
    x_lat, x_ctx = x, ctx
    for l in range(DEPTH):
        need_ctx = l < DEPTH - 1
        lam_init = 0.8 - 0.6 * math.exp(-0.3 * l)
        mod_lat = (silu_c @ w_ada[l] + b_ada[l])[:, None, :]
        mod_ctx = silu_cc @ w_ada[l] + b_ada[l]
        sh1, sc1, g1, sh2, sc2, g2 = jnp.split(mod_lat, 6, axis=-1)
        csh1, csc1, cg1, csh2, csc2, cg2 = jnp.split(mod_ctx, 6, axis=-1)

        h_lat = rmsnorm(x_lat, norm_g[l, 0]) * (1.0 + sc1) + sh1
        h_ctx = rmsnorm(x_ctx, norm_g[l, 0]) * (1.0 + csc1) + csh1
        m_lat, m_ctx = token_mixer(h_lat, h_ctx, cos, sin, w_in[l], diff_lambda[l], lam_init,
                                   diff_subln_g[l], lru_conv_w[l], lru_conv_b[l], lru_gate_w[l],
                                   lru_gate_b[l], lru_lambda[l], na_rpb[l], w_branch[l],
                                   w_merge[l], b_merge[l], w_out[l], need_ctx)
        x_lat = x_lat + g1 * rmsnorm(m_lat, norm_g[l, 1])

        h_lat = rmsnorm(x_lat, norm_g[l, 2]) * (1.0 + sc2) + sh2
        f_lat = channel_mixer(h_lat, l, ffn_w_gate, ffn_w_up, ffn_w_down,
                              moe_w_router, moe_w_gate, moe_w_up, moe_w_down)
        x_lat = x_lat + g2 * rmsnorm(f_lat, norm_g[l, 3])

        if need_ctx:
            x_ctx = x_ctx + cg1 * rmsnorm(m_ctx, norm_g[l, 1])
            h_ctx = rmsnorm(x_ctx, norm_g[l, 2]) * (1.0 + csc2) + csh2
            f_ctx = channel_mixer(h_ctx, l, ffn_w_gate, ffn_w_up, ffn_w_down,
                                  moe_w_router, moe_w_gate, moe_w_up, moe_w_down)
            x_ctx = x_ctx + cg2 * rmsnorm(f_ctx, norm_g[l, 3])
    return x_lat
```

```python
import functools
import math

import numpy as np
import jax
import jax.numpy as jnp
from jax import lax
from jax.experimental import pallas as pl
from jax.experimental.pallas import tpu as pltpu

F32 = jnp.float32
BF16 = jnp.bfloat16
I32 = jnp.int32

EPS = 1e-6
GRID_W = 64
BRANCH_WIDTH = 1024
N_IN_PARTS = 8
IN_WIDTH = N_IN_PARTS * BRANCH_WIDTH
DIFF_HEADS = 8
DIFF_HEAD_DIM = 64
DIFF_V_DIM = 2 * DIFF_HEAD_DIM
ROPE_BASE = 10000.0
ROPE_FREQ = DIFF_HEAD_DIM // 4
LRU_BLOCKS = 8
LRU_BLOCK_DIM = BRANCH_WIDTH // LRU_BLOCKS
LRU_CONV_W = 4
LRU_C = 8.0
NA_HEADS = 8
NA_HEAD_DIM = BRANCH_WIDTH // NA_HEADS
NA_WIN_R = 8
NA_WIN_C = 16
TOP_K = 2

LANES = 128
SUBLANES = 8
VMEM_LIMIT = 56 * 1024 * 1024
NEG = -1e30

P_DQ, P_DK, P_DV, P_LX, P_LG, P_NQ, P_NK, P_NV = range(8)
M_SH1, M_SC1, M_G1, M_SH2, M_SC2, M_G2 = range(6)


def _params(sem):
    return pltpu.CompilerParams(dimension_semantics=sem, vmem_limit_bytes=VMEM_LIMIT)


def _rms(x, g):
    return x * lax.rsqrt(jnp.mean(x * x, axis=-1, keepdims=True) + EPS) * g


def _dot(a, b):
    return jnp.dot(a, b, preferred_element_type=F32)


def _dot_nt(a, b):
    return lax.dot_general(a, b, (((1,), (1,)), ((), ())), preferred_element_type=F32)


def _ada_kernel(c_ref, w_ref, b_ref, o_ref):
    c = c_ref[...]
    s = (c * jax.nn.sigmoid(c)).astype(BF16)
    o_ref[...] = _dot(s, w_ref[...].astype(BF16)) + b_ref[...]


def _ada(cvec, w_ada, b_ada):
    n_layers, d, n6 = w_ada.shape
    mr = cvec.shape[0]
    tn = min(1024, d)
    return pl.pallas_call(
        _ada_kernel,
        grid=(n_layers, n6 // tn),
        in_specs=[pl.BlockSpec((mr, d), lambda l, j: (0, 0)),
                  pl.BlockSpec((None, d, tn), lambda l, j: (l, 0, j)),
                  pl.BlockSpec((None, 1, tn), lambda l, j: (l, 0, j))],
        out_specs=pl.BlockSpec((None, mr, tn), lambda l, j: (l, 0, j)),
        out_shape=jax.ShapeDtypeStruct((n_layers, mr, n6), F32),
        compiler_params=_params(("arbitrary", "arbitrary")),
        name="ada_mod",
    )(cvec, w_ada, b_ada.reshape(n_layers, 1, n6))


def _mod_spec(d, chunk, row_fn):
    return pl.BlockSpec((None, 1, d), lambda i, *_: (row_fn(i), 0, chunk))


def _vec_spec(d):
    return pl.BlockSpec((1, d), lambda i, *_: (0, 0))


def _prenorm_kernel(x_ref, g_ref, sc_ref, sh_ref, o_ref):
    y = _rms(x_ref[...], g_ref[...])
    o_ref[...] = (y * (1.0 + sc_ref[...]) + sh_ref[...]).astype(o_ref.dtype)


def _prenorm(x, g, mod, row_fn, tm):
    n, d = x.shape
    return pl.pallas_call(
        _prenorm_kernel,
        grid=(n // tm,),
        in_specs=[pl.BlockSpec((tm, d), lambda i: (i, 0)),
                  _vec_spec(d),
                  _mod_spec(d, M_SC1, row_fn),
                  _mod_spec(d, M_SH1, row_fn)],
        out_specs=pl.BlockSpec((tm, d), lambda i: (i, 0)),
        out_shape=jax.ShapeDtypeStruct((n, d), BF16),
        compiler_params=_params(("parallel",)),
        name="prenorm",
    )(x, g.reshape(1, d), mod, mod)


def _proj_kernel(h_ref, w_ref, b_ref, cos_ref, sin_ref, o_ref, *, n_rope, n_plain):
    j = pl.program_id(1)
    acc = _dot(h_ref[...], w_ref[...])
    tn = acc.shape[1]

    if n_rope:
        @pl.when(j < n_rope)
        def _():
            lane = lax.broadcasted_iota(I32, acc.shape, 1)
            first_half = (lane % (2 * ROPE_FREQ)) < ROPE_FREQ
            partner = jnp.where(first_half, pltpu.roll(acc, tn - ROPE_FREQ, 1),
                                pltpu.roll(acc, ROPE_FREQ, 1))
            cos = jnp.tile(cos_ref[...], (1, tn // LANES))
            sin = jnp.tile(sin_ref[...], (1, tn // LANES))
            o_ref[...] = (acc * cos + partner * sin).astype(o_ref.dtype)

    @pl.when((j >= n_rope) & (j < n_plain))
    def _():
        o_ref[...] = acc.astype(o_ref.dtype)

    @pl.when(j >= n_plain)
    def _():
        o_ref[...] = jax.nn.sigmoid(acc + b_ref[...]).astype(o_ref.dtype)


def _proj(h, wcat, bcat, cos_t, sin_t, seq, rope, tm):
    n, d = h.shape
    nw = wcat.shape[1]
    tn = min(1024, d)
    tpb = seq // tm
    n_plain = IN_WIDTH // tn
    n_rope = (2 * BRANCH_WIDTH) // tn if rope else 0
    kern = functools.partial(_proj_kernel, n_rope=n_rope, n_plain=n_plain)
    return pl.pallas_call(
        kern,
        grid=(n // tm, nw // tn),
        in_specs=[pl.BlockSpec((tm, d), lambda i, j: (i, 0)),
                  pl.BlockSpec((d, tn), lambda i, j: (0, j)),
                  pl.BlockSpec((1, tn), lambda i, j: (0, j)),
                  pl.BlockSpec((tm, LANES), lambda i, j: (i % tpb, 0)),
                  pl.BlockSpec((tm, LANES), lambda i, j: (i % tpb, 0))],
        out_specs=pl.BlockSpec((tm, tn), lambda i, j: (i, j)),
        out_shape=jax.ShapeDtypeStruct((n, nw), BF16),
        compiler_params=_params(("parallel", "arbitrary")),
        name="proj_rope" if rope else "proj",
    )(h, wcat, bcat, cos_t, sin_t)


def _diff_kernel(*refs, lam_init, has_lat):
    if has_lat:
        lamv_ref, sg_ref, q_ref, k_ref, v_ref, kc_ref, vc_ref, o_ref = refs
    else:
        lamv_ref, sg_ref, q_ref, kc_ref, vc_ref, o_ref = refs
    lv = lamv_ref[...]
    lam = (jnp.exp(jnp.sum(lv[0:1] * lv[1:2], axis=-1, keepdims=True))
           - jnp.exp(jnp.sum(lv[2:3] * lv[3:4], axis=-1, keepdims=True)) + lam_init)
    q = q_ref[...]
    kc = kc_ref[...]
    scale = DIFF_HEAD_DIM ** -0.5

    def probs(c):
        sl = slice(c * DIFF_HEAD_DIM, (c + 1) * DIFF_HEAD_DIM)
        qc = q[:, sl]
        s_c = _dot_nt(qc, kc[:, sl]) * scale
        m = jnp.max(s_c, axis=-1, keepdims=True)
        if has_lat:
            s_l = _dot_nt(qc, k_ref[:, sl]) * scale
            m = jnp.maximum(m, jnp.max(s_l, axis=-1, keepdims=True))
            e_l = jnp.exp(s_l - m)
        e_c = jnp.exp(s_c - m)
        den = jnp.sum(e_c, axis=-1, keepdims=True)
        if has_lat:
            den = den + jnp.sum(e_l, axis=-1, keepdims=True)
        inv = 1.0 / den
        return (e_l * inv if has_lat else None), e_c * inv

    p1l, p1c = probs(0)
    p2l, p2c = probs(1)
    o = _dot((p1c - lam * p2c).astype(BF16), vc_ref[...])
    if has_lat:
        o = o + _dot((p1l - lam * p2l).astype(BF16), v_ref[...])
    y = _rms(o, sg_ref[...]) * (1.0 - lam_init)
    o_ref[...] = y.astype(o_ref.dtype)


def _diff_attn_lat(p_lat, p_ctx, lamv, subln_g, lam_init, bsz, seq, clen, tq):
    n = p_lat.shape[0]
    nq = seq // tq
    hw = DIFF_V_DIM
    hpb = BRANCH_WIDTH // hw
    kern = functools.partial(_diff_kernel, lam_init=lam_init, has_lat=True)
    return pl.pallas_call(
        kern,
        grid=(bsz, DIFF_HEADS, nq),
        in_specs=[pl.BlockSpec((4, DIFF_HEAD_DIM), lambda b, h, i: (0, 0)),
                  pl.BlockSpec((1, hw), lambda b, h, i: (0, 0)),
                  pl.BlockSpec((tq, hw), lambda b, h, i: (b * nq + i, P_DQ * hpb + h)),
                  pl.BlockSpec((seq, hw), lambda b, h, i: (b, P_DK * hpb + h)),
                  pl.BlockSpec((seq, hw), lambda b, h, i: (b, P_DV * hpb + h)),
                  pl.BlockSpec((clen, hw), lambda b, h, i: (b, P_DK * hpb + h)),
                  pl.BlockSpec((clen, hw), lambda b, h, i: (b, P_DV * hpb + h))],
        out_specs=pl.BlockSpec((tq, hw), lambda b, h, i: (b * nq + i, h)),
        out_shape=jax.ShapeDtypeStruct((n, BRANCH_WIDTH), BF16),
        compiler_params=_params(("parallel", "parallel", "arbitrary")),
        name="diff_attn",
    )(lamv, subln_g.reshape(1, hw), p_lat, p_lat, p_lat, p_ctx, p_ctx)


def _diff_attn_ctx(p_ctx, lamv, subln_g, lam_init, bsz, clen):
    hw = DIFF_V_DIM
    hpb = BRANCH_WIDTH // hw
    kern = functools.partial(_diff_kernel, lam_init=lam_init, has_lat=False)
    return pl.pallas_call(
        kern,
        grid=(bsz, DIFF_HEADS),
        in_specs=[pl.BlockSpec((4, DIFF_HEAD_DIM), lambda b, h: (0, 0)),
                  pl.BlockSpec((1, hw), lambda b, h: (0, 0)),
                  pl.BlockSpec((clen, hw), lambda b, h: (b, P_DQ * hpb + h)),
                  pl.BlockSpec((clen, hw), lambda b, h: (b, P_DK * hpb + h)),
                  pl.BlockSpec((clen, hw), lambda b, h: (b, P_DV * hpb + h))],
        out_specs=pl.BlockSpec((clen, hw), lambda b, h: (b, h)),
        out_shape=jax.ShapeDtypeStruct((bsz * clen, BRANCH_WIDTH), BF16),
        compiler_params=_params(("parallel", "parallel")),
        name="diff_attn_ctx",
    )(lamv, subln_g.reshape(1, hw), p_ctx, p_ctx, p_ctx)


LRU_TW = 256


def _gelu_tanh(x):
    return 0.5 * x * (1.0 + jnp.tanh(math.sqrt(2.0 / math.pi) * (x + 0.044715 * (x * x * x))))


def _tile_scan(a, b, reverse):
    rows = lax.broadcasted_iota(I32, a.shape, 0)
    for s in (1, 2, 4):
        if reverse:
            a_s = pltpu.roll(a, SUBLANES - s, 0)
            b_s = pltpu.roll(b, SUBLANES - s, 0)
            valid = rows < SUBLANES - s
        else:
            a_s = pltpu.roll(a, s, 0)
            b_s = pltpu.roll(b, s, 0)
            valid = rows >= s
        b = jnp.where(valid, a * b_s + b, b)
        a = jnp.where(valid, a * a_s, a)
    return a, b


def _lru_kernel(xl_ref, gl_ref, xc_ref, gc_ref, cw_ref, cb_ref, gw_ref, gb_ref, lam_ref,
                *out_and_scratch, seq, clen, need_ctx):
    if need_ctx:
        yl_ref, yc_ref, af_ref, bf_ref, ab_ref, bb_ref, hf_ref = out_and_scratch
    else:
        yl_ref, af_ref, bf_ref, ab_ref, bb_ref, hf_ref = out_and_scratch
        yc_ref = None
    tot = clen + seq
    x = jnp.concatenate([xc_ref[...], xl_ref[...]], axis=0).astype(F32)
    row = lax.broadcasted_iota(I32, x.shape, 0)
    pos = jnp.where(row < clen, row, row - clen)
    seg_len = jnp.where(row < clen, clen, seq)
    cw = cw_ref[...]
    conv = cb_ref[...] + x * cw[2:3]
    conv = conv + jnp.where(pos >= 2, pltpu.roll(x, 2, 0), 0.0) * cw[0:1]
    conv = conv + jnp.where(pos >= 1, pltpu.roll(x, 1, 0), 0.0) * cw[1:2]
    conv = conv + jnp.where(pos < seg_len - 1, pltpu.roll(x, tot - 1, 0), 0.0) * cw[3:4]
    xb = conv.astype(BF16)
    gb = gb_ref[...]
    lam = lam_ref[...]
    for d, (a_ref, b_ref) in enumerate(((af_ref, bf_ref), (ab_ref, bb_ref))):
        r = jax.nn.sigmoid(_dot(xb, gw_ref[d, 0]) + gb[2 * d:2 * d + 1])
        i = jax.nn.sigmoid(_dot(xb, gw_ref[d, 1]) + gb[2 * d + 1:2 * d + 2])
        z = -lam[d:d + 1]
        softplus = jnp.maximum(z, 0.0) + jnp.log1p(jnp.exp(-jnp.abs(z)))
        log_a = -LRU_C * r * softplus
        a = jnp.exp(log_a)
        beta = jnp.sqrt(-jnp.tanh(log_a) * (a * a + 1.0))
        a_ref[...] = a
        b_ref[...] = beta * i * conv

    n_ct = clen // SUBLANES
    n_t = tot // SUBLANES
    w = x.shape[1]

    def fwd_step(t, carry):
        off = pl.multiple_of(t * SUBLANES, SUBLANES)
        a, b = _tile_scan(af_ref[pl.ds(off, SUBLANES), :], bf_ref[pl.ds(off, SUBLANES), :], False)
        h = a * carry + b
        hf_ref[pl.ds(off, SUBLANES), :] = h
        return h[SUBLANES - 1:SUBLANES, :]

    lax.fori_loop(0, n_t, fwd_step, jnp.zeros((1, w), F32))

    def bwd_tile(off, carry):
        a, b = _tile_scan(ab_ref[pl.ds(off, SUBLANES), :], bb_ref[pl.ds(off, SUBLANES), :], True)
        h = a * carry + b
        return h, h[0:1, :]

    def bwd_ctx_step(t, carry):
        off = pl.multiple_of((n_ct - 1 - t) * SUBLANES, SUBLANES)
        h, carry = bwd_tile(off, carry)
        if need_ctx:
            g = gc_ref[pl.ds(off, SUBLANES), :].astype(F32)
            yc_ref[pl.ds(off, SUBLANES), :] = (
                (hf_ref[pl.ds(off, SUBLANES), :] + h) * _gelu_tanh(g)).astype(yc_ref.dtype)
        return carry

    carry = lax.fori_loop(0, n_ct, bwd_ctx_step, jnp.zeros((1, w), F32))

    def bwd_lat_step(t, carry):
        off = pl.multiple_of((n_t - 1 - t) * SUBLANES, SUBLANES)
        h, carry = bwd_tile(off, carry)
        lo = pl.multiple_of(off - clen, SUBLANES)
        g = gl_ref[pl.ds(lo, SUBLANES), :].astype(F32)
        yl_ref[pl.ds(lo, SUBLANES), :] = (
            (hf_ref[pl.ds(off, SUBLANES), :] + h) * _gelu_tanh(g)).astype(yl_ref.dtype)
        return carry

    lax.fori_loop(0, n_t - n_ct, bwd_lat_step, carry)


def _lru(p_lat, p_ctx, conv_w, conv_b, gate_w, gate_b, lru_lam, bsz, seq, clen, need_ctx):
    n = p_lat.shape[0]
    tw = LRU_TW
    nwt = BRANCH_WIDTH // tw
    per = tw // LRU_BLOCK_DIM
    gw = gate_w.reshape(2, 2, nwt, per, LRU_BLOCK_DIM, LRU_BLOCK_DIM)
    eye = jnp.eye(per, dtype=gate_w.dtype)
    gw = jnp.einsum('dgnpcf,pq->dgnpcqf', gw, eye).reshape(2, 2, nwt, tw, tw).astype(BF16)
    gb = gate_b.reshape(4, BRANCH_WIDTH)
    tot = seq + clen
    kern = functools.partial(_lru_kernel, seq=seq, clen=clen, need_ctx=need_ctx)
    out_shape = [jax.ShapeDtypeStruct((n, BRANCH_WIDTH), BF16)]
    out_specs = [pl.BlockSpec((seq, tw), lambda b, j: (b, j))]
    if need_ctx:
        out_shape.append(jax.ShapeDtypeStruct((bsz * clen, BRANCH_WIDTH), BF16))
        out_specs.append(pl.BlockSpec((clen, tw), lambda b, j: (b, j)))
    res = pl.pallas_call(
        kern,
        grid=(bsz, nwt),
        in_specs=[pl.BlockSpec((seq, tw), lambda b, j: (b, P_LX * nwt + j)),
                  pl.BlockSpec((seq, tw), lambda b, j: (b, P_LG * nwt + j)),
                  pl.BlockSpec((clen, tw), lambda b, j: (b, P_LX * nwt + j)),
                  pl.BlockSpec((clen, tw), lambda b, j: (b, P_LG * nwt + j)),
                  pl.BlockSpec((LRU_CONV_W, tw), lambda b, j: (0, j)),
                  pl.BlockSpec((1, tw), lambda b, j: (0, j)),
                  pl.BlockSpec((2, 2, None, tw, tw), lambda b, j: (0, 0, j, 0, 0)),
                  pl.BlockSpec((4, tw), lambda b, j: (0, j)),
                  pl.BlockSpec((2, tw), lambda b, j: (0, j))],
        out_specs=out_specs,
        out_shape=out_shape,
        scratch_shapes=[pltpu.VMEM((tot, tw), F32) for _ in range(5)],
        compiler_params=_params(("parallel", "parallel")),
        name="rglru",
    )(p_lat, p_lat, p_ctx, p_ctx, conv_w, conv_b.reshape(1, BRANCH_WIDTH), gw, gb, lru_lam)
    return (res[0], res[1]) if need_ctx else (res[0], None)


NA_ROWS_PER_TILE = 2


def _na_tables(rows, rpt):
    kr = min(NA_WIN_R, rows)
    kc = NA_WIN_C
    nkr = min(rows, rpt + kr - 1)
    col = np.arange(GRID_W)
    cs = np.clip(col - kc // 2, 0, GRID_W - kc)
    variants, keys, tile_variant, tile_kb = [], {}, [], []
    for t in range(rows // rpt):
        r0 = t * rpt
        kb = int(np.clip(r0 - kr // 2, 0, rows - nkr))
        qr = (r0 + np.arange(rpt))[:, None, None, None]
        qc = col[None, :, None, None]
        krow = (kb + np.arange(nkr))[None, None, :, None]
        kcol = col[None, None, None, :]
        rs = np.clip(qr - kr // 2, 0, rows - kr)
        csq = cs[None, :, None, None]
        mask = (krow >= rs) & (krow < rs + kr) & (kcol >= csq) & (kcol < csq + kc)
        dr = np.clip(krow - qr + (NA_WIN_R - 1), 0, 2 * NA_WIN_R - 2)
        dc = np.clip(kcol - qc + (NA_WIN_C - 1), 0, 2 * NA_WIN_C - 2)
        shape = (rpt * GRID_W, nkr * GRID_W)
        dr = np.broadcast_to(dr, mask.shape).reshape(shape)
        dc = np.broadcast_to(dc, mask.shape).reshape(shape)
        mask = mask.reshape(shape)
        key = (dr.tobytes(), dc.tobytes(), mask.tobytes())
        if key not in keys:
            keys[key] = len(variants)
            variants.append((dr, dc, mask))
        tile_variant.append(keys[key])
        tile_kb.append(kb)
    dr = np.stack([v[0] for v in variants])
    dc = np.stack([v[1] for v in variants])
    mask = np.stack([v[2] for v in variants])
    return nkr, np.array(tile_variant, np.int32), np.array(tile_kb, np.int32), dr, dc, mask


def _na_kernel(var_ref, kb_ref, q_ref, k_ref, v_ref, kc_ref, vc_ref, bias_ref, o_ref, *, nk):
    t = pl.program_id(1)
    start = pl.multiple_of(kb_ref[t] * GRID_W, GRID_W)
    scale = NA_HEAD_DIM ** -0.5
    for h in range(NA_HEADS):
        sl = slice(h * NA_HEAD_DIM, (h + 1) * NA_HEAD_DIM)
        qh = q_ref[:, sl]
        s_w = _dot_nt(qh, k_ref[pl.ds(start, nk), sl]) * scale + bias_ref[h]
        s_c = _dot_nt(qh, kc_ref[:, sl]) * scale
        m = jnp.maximum(jnp.max(s_w, axis=-1, keepdims=True), jnp.max(s_c, axis=-1, keepdims=True))
        e_w = jnp.exp(s_w - m)
        e_c = jnp.exp(s_c - m)
        inv = 1.0 / (jnp.sum(e_w, axis=-1, keepdims=True) + jnp.sum(e_c, axis=-1, keepdims=True))
        o = (_dot((e_w * inv).astype(BF16), v_ref[pl.ds(start, nk), sl])
             + _dot((e_c * inv).astype(BF16), vc_ref[:, sl]))
        o_ref[:, sl] = o.astype(o_ref.dtype)


def _na_lat(p_lat, p_ctx, rpb, bsz, seq, clen):
    n = p_lat.shape[0]
    rows = seq // GRID_W
    rpt = min(NA_ROWS_PER_TILE, rows)
    nkr, tile_variant, tile_kb, dr, dc, mask = _na_tables(rows, rpt)
    bias = jnp.where(mask[None], rpb[:, dr, dc], NEG)
    bias = jnp.transpose(bias, (1, 0, 2, 3)).astype(F32)
    nt = rows // rpt
    rq, nk = rpt * GRID_W, nkr * GRID_W
    bw = BRANCH_WIDTH
    kern = functools.partial(_na_kernel, nk=nk)
    return pl.pallas_call(
        kern,
        grid_spec=pltpu.PrefetchScalarGridSpec(
            num_scalar_prefetch=2,
            grid=(bsz, nt),
            in_specs=[pl.BlockSpec((rq, bw), lambda b, t, var, kb: (b * nt + t, P_NQ)),
                      pl.BlockSpec((seq, bw), lambda b, t, var, kb: (b, P_NK)),
                      pl.BlockSpec((seq, bw), lambda b, t, var, kb: (b, P_NV)),
                      pl.BlockSpec((clen, bw), lambda b, t, var, kb: (b, P_NK)),
                      pl.BlockSpec((clen, bw), lambda b, t, var, kb: (b, P_NV)),
                      pl.BlockSpec((None, NA_HEADS, rq, nk), lambda b, t, var, kb: (var[t], 0, 0, 0))],
            out_specs=pl.BlockSpec((rq, bw), lambda b, t, var, kb: (b * nt + t, 0))),
        out_shape=jax.ShapeDtypeStruct((n, bw), BF16),
        compiler_params=_params(("parallel", "arbitrary")),
        name="na_attn",
    )(jnp.asarray(tile_variant), jnp.asarray(tile_kb), p_lat, p_lat, p_lat, p_ctx, p_ctx, bias)


def _na_ctx_kernel(q_ref, k_ref, v_ref, o_ref):
    scale = NA_HEAD_DIM ** -0.5
    for h in range(NA_HEADS):
        sl = slice(h * NA_HEAD_DIM, (h + 1) * NA_HEAD_DIM)
        s = _dot_nt(q_ref[:, sl], k_ref[:, sl]) * scale
        e = jnp.exp(s - jnp.max(s, axis=-1, keepdims=True))
        p = e * (1.0 / jnp.sum(e, axis=-1, keepdims=True))
        o_ref[:, sl] = _dot(p.astype(BF16), v_ref[:, sl]).astype(o_ref.dtype)


def _na_ctx(p_ctx, bsz, clen):
    bw = BRANCH_WIDTH
    return pl.pallas_call(
        _na_ctx_kernel,
        grid=(bsz,),
        in_specs=[pl.BlockSpec((clen, bw), lambda b: (b, P_NQ)),
                  pl.BlockSpec((clen, bw), lambda b: (b, P_NK)),
                  pl.BlockSpec((clen, bw), lambda b: (b, P_NV))],
        out_specs=pl.BlockSpec((clen, bw), lambda b: (b, 0)),
        out_shape=jax.ShapeDtypeStruct((bsz * clen, bw), BF16),
        compiler_params=_params(("parallel",)),
        name="na_attn_ctx",
    )(p_ctx, p_ctx, p_ctx)


def _merge_kernel(yd_ref, yl_ref, yn_ref, g0_ref, g1_ref, g2_ref, wb_ref, wo_ref, x_ref,
                  n1_ref, gate_ref, n2_ref, sc_ref, sh_ref, xm_ref, h2_ref):
    m = (g0_ref[...].astype(F32) * _dot(yd_ref[...], wb_ref[0])
         + g1_ref[...].astype(F32) * _dot(yl_ref[...], wb_ref[1])
         + g2_ref[...].astype(F32) * _dot(yn_ref[...], wb_ref[2]))
    mo = _dot(m.astype(BF16), wo_ref[...])
    xm = x_ref[...] + gate_ref[...] * _rms(mo, n1_ref[...])
    xm_ref[...] = xm
    h2 = _rms(xm, n2_ref[...]) * (1.0 + sc_ref[...]) + sh_ref[...]
    h2_ref[...] = h2.astype(h2_ref.dtype)


def _merge(yd, yl, yn, p, wb, wo, x, n1, n2, mod, row_fn, tm, h2_dtype):
    n, d = x.shape
    bw = BRANCH_WIDTH
    gcol = IN_WIDTH // d
    resident = dict(pipeline_mode=pl.Buffered(1))
    return pl.pallas_call(
        _merge_kernel,
        grid=(n // tm,),
        in_specs=[pl.BlockSpec((tm, bw), lambda i: (i, 0)),
                  pl.BlockSpec((tm, bw), lambda i: (i, 0)),
                  pl.BlockSpec((tm, bw), lambda i: (i, 0)),
                  pl.BlockSpec((tm, d), lambda i: (i, gcol)),
                  pl.BlockSpec((tm, d), lambda i: (i, gcol + 1)),
                  pl.BlockSpec((tm, d), lambda i: (i, gcol + 2)),
                  pl.BlockSpec((3, bw, d), lambda i: (0, 0, 0), **resident),
                  pl.BlockSpec((d, d), lambda i: (0, 0), **resident),
                  pl.BlockSpec((tm, d), lambda i: (i, 0)),
                  _vec_spec(d),
                  _mod_spec(d, M_G1, row_fn),
                  _vec_spec(d),
                  _mod_spec(d, M_SC2, row_fn),
                  _mod_spec(d, M_SH2, row_fn)],
        out_specs=[pl.BlockSpec((tm, d), lambda i: (i, 0)),
                   pl.BlockSpec((tm, d), lambda i: (i, 0))],
        out_shape=[jax.ShapeDtypeStruct((n, d), F32),
                   jax.ShapeDtypeStruct((n, d), h2_dtype)],
        compiler_params=_params(("parallel",)),
        name="merge",
    )(yd, yl, yn, p, p, p, wb, wo, x, n1.reshape(1, d), mod, n2.reshape(1, d), mod, mod)


def _swiglu_step(x_ref, wg_ref, wu_ref, wd_ref, acc_ref, f):
    @pl.when(f == 0)
    def _():
        acc_ref[...] = jnp.zeros_like(acc_ref)
    xb = x_ref[...].astype(BF16)
    g = _dot(xb, wg_ref[...])
    u = _dot(xb, wu_ref[...])
    a = (g * jax.nn.sigmoid(g) * u).astype(BF16)
    acc_ref[...] += _dot(a, wd_ref[...])


def _ffn_dense_kernel(x_ref, wg_ref, wu_ref, wd_ref, xm_ref, gate_ref, n3_ref, *rest, nf, with_next):
    if with_next:
        n0_ref, sc_ref, sh_ref, xo_ref, hn_ref, acc_ref = rest
    else:
        xo_ref, acc_ref = rest
    f = pl.program_id(1)
    _swiglu_step(x_ref, wg_ref, wu_ref, wd_ref, acc_ref, f)

    @pl.when(f == nf - 1)
    def _():
        xo = xm_ref[...] + gate_ref[...] * _rms(acc_ref[...], n3_ref[...])
        xo_ref[...] = xo
        if with_next:
            hn = _rms(xo, n0_ref[...]) * (1.0 + sc_ref[...]) + sh_ref[...]
            hn_ref[...] = hn.astype(hn_ref.dtype)


def _ffn_dense(h2, wg, wu, wd, xm, n3, mod, row_fn, tm, tf, nxt):
    n, d = h2.shape
    fp = wg.shape[1]
    nf = fp // tf
    with_next = nxt is not None
    in_specs = [pl.BlockSpec((tm, d), lambda i, f: (i, 0)),
                pl.BlockSpec((d, tf), lambda i, f: (0, f)),
                pl.BlockSpec((d, tf), lambda i, f: (0, f)),
                pl.BlockSpec((tf, d), lambda i, f: (f, 0)),
                pl.BlockSpec((tm, d), lambda i, f: (i, 0)),
                _mod_spec(d, M_G2, row_fn),
                _vec_spec(d)]
    args = [h2, wg, wu, wd, xm, mod, n3.reshape(1, d)]
    out_specs = [pl.BlockSpec((tm, d), lambda i, f: (i, 0))]
    out_shape = [jax.ShapeDtypeStruct((n, d), F32)]
    if with_next:
        n0, mod_next = nxt
        in_specs += [_vec_spec(d), _mod_spec(d, M_SC1, row_fn), _mod_spec(d, M_SH1, row_fn)]
        args += [n0.reshape(1, d), mod_next, mod_next]
        out_specs.append(pl.BlockSpec((tm, d), lambda i, f: (i, 0)))
        out_shape.append(jax.ShapeDtypeStruct((n, d), BF16))
    kern = functools.partial(_ffn_dense_kernel, nf=nf, with_next=with_next)
    res = pl.pallas_call(
        kern,
        grid=(n // tm, nf),
        in_specs=in_specs,
        out_specs=out_specs,
        out_shape=out_shape,
        scratch_shapes=[pltpu.VMEM((tm, d), F32)],
        compiler_params=_params(("parallel", "arbitrary")),
        name="ffn_dense",
    )(*args)
    return (res[0], res[1]) if with_next else (res[0], None)


def _ffn_grouped_kernel(te_ref, nt_ref, x_ref, wg_ref, wu_ref, wd_ref, o_ref, acc_ref, *, nf):
    i = pl.program_id(0)
    f = pl.program_id(1)

    @pl.when(i < nt_ref[0])
    def _():
        _swiglu_step(x_ref, wg_ref, wu_ref, wd_ref, acc_ref, f)

        @pl.when(f == nf - 1)
        def _():
            o_ref[...] = acc_ref[...]

    @pl.when((i >= nt_ref[0]) & (f == nf - 1))
    def _():
        o_ref[...] = jnp.zeros_like(o_ref)


def _ffn_grouped(xs, wg, wu, wd, tile_expert, n_tiles, tm, tf):
    p, d = xs.shape
    fe = wg.shape[2]
    nf = fe // tf

    def live(i, nt):
        return jnp.minimum(i, nt[0] - 1)

    def fidx(i, f, nt):
        return jnp.where(i < nt[0], f, nf - 1)

    kern = functools.partial(_ffn_grouped_kernel, nf=nf)
    return pl.pallas_call(
        kern,
        grid_spec=pltpu.PrefetchScalarGridSpec(
            num_scalar_prefetch=2,
            grid=(p // tm, nf),
            in_specs=[pl.BlockSpec((tm, d), lambda i, f, te, nt: (live(i, nt), 0)),
                      pl.BlockSpec((None, d, tf), lambda i, f, te, nt: (te[i], 0, fidx(i, f, nt))),
                      pl.BlockSpec((None, d, tf), lambda i, f, te, nt: (te[i], 0, fidx(i, f, nt))),
                      pl.BlockSpec((None, tf, d), lambda i, f, te, nt: (te[i], fidx(i, f, nt), 0))],
            out_specs=pl.BlockSpec((tm, d), lambda i, f, te, nt: (i, 0)),
            scratch_shapes=[pltpu.VMEM((tm, d), F32)]),
        out_shape=jax.ShapeDtypeStruct((p, d), F32),
        compiler_params=_params(("arbitrary", "arbitrary")),
        name="ffn_grouped",
    )(tile_expert, n_tiles, xs, wg, wu, wd)


def _router_kernel(h_ref, wr_ref, o_ref, cnt_ref, carry_ref, *, n_experts):
    i = pl.program_id(0)

    @pl.when(i == 0)
    def _():
        carry_ref[...] = jnp.zeros_like(carry_ref)

    logits = _dot(h_ref[...].astype(BF16), wr_ref[...])
    tm = logits.shape[0]
    lane_i = lax.broadcasted_iota(I32, logits.shape, 1)
    lane = lane_i.astype(F32)
    logits = jnp.where(lane_i < n_experts, logits, -jnp.inf)
    m1 = jnp.max(logits, axis=-1, keepdims=True)
    i1 = jnp.min(jnp.where(logits == m1, lane, float(LANES)), axis=-1, keepdims=True)
    rest = jnp.where(lane == i1, -jnp.inf, logits)
    m2 = jnp.max(rest, axis=-1, keepdims=True)
    i2 = jnp.min(jnp.where(rest == m2, lane, float(LANES)), axis=-1, keepdims=True)
    e21 = jnp.exp(m2 - m1)
    w1 = 1.0 / (1.0 + e21)
    w2 = e21 * w1

    sel1 = lane == i1
    sel2 = lane == i2
    onehot = jnp.where(sel1 | sel2, 1.0, 0.0).astype(BF16)
    r = lax.broadcasted_iota(I32, (tm, tm), 0)
    c = lax.broadcasted_iota(I32, (tm, tm), 1)
    strict_lower = jnp.where(c < r, 1.0, 0.0).astype(BF16)
    before = _dot(strict_lower, onehot) + carry_ref[0:1, :]
    r1 = jnp.sum(jnp.where(sel1, before, 0.0), axis=-1, keepdims=True)
    r2 = jnp.sum(jnp.where(sel2, before, 0.0), axis=-1, keepdims=True)
    total = carry_ref[0:1, :] + jnp.sum(onehot.astype(F32), axis=0, keepdims=True)
    carry_ref[...] = jnp.broadcast_to(total, carry_ref.shape)
    cnt_ref[...] = jnp.broadcast_to(total, cnt_ref.shape)

    packed = jnp.where(lane_i == 0, i1, 0.0)
    packed = jnp.where(lane_i == 1, i2, packed)
    packed = jnp.where(lane_i == 2, r1, packed)
    packed = jnp.where(lane_i == 3, r2, packed)
    packed = jnp.where(lane_i == 4, w1, packed)
    packed = jnp.where(lane_i == 5, w2, packed)
    o_ref[...] = packed


def _router(h2, w_router, tm):
    n, d = h2.shape
    n_experts = w_router.shape[1]
    wr = jnp.pad(w_router, ((0, 0), (0, LANES - n_experts))).astype(BF16)
    kern = functools.partial(_router_kernel, n_experts=n_experts)
    return pl.pallas_call(
        kern,
        grid=(n // tm,),
        in_specs=[pl.BlockSpec((tm, d), lambda i: (i, 0)),
                  pl.BlockSpec((d, LANES), lambda i: (0, 0))],
        out_specs=[pl.BlockSpec((tm, LANES), lambda i: (i, 0)),
                   pl.BlockSpec((SUBLANES, LANES), lambda i: (0, 0))],
        out_shape=[jax.ShapeDtypeStruct((n, LANES), F32),
                   jax.ShapeDtypeStruct((SUBLANES, LANES), F32)],
        scratch_shapes=[pltpu.VMEM((SUBLANES, LANES), F32)],
        compiler_params=_params(("arbitrary",)),
        name="router",
    )(h2, wr)


def _row_copy(src_hbm, src_row, dst_ref, dst_row, sem):
    return pltpu.make_async_copy(src_hbm.at[pl.ds(src_row, 1)], dst_ref.at[pl.ds(dst_row, 1)], sem)


def _dispatch_kernel(src_ref, x_hbm, o_hbm, sem, *, rows):
    base = pl.program_id(0) * rows

    def issue(r, c):
        _row_copy(x_hbm, src_ref[base + r], o_hbm, base + r, sem).start()
        return c
    lax.fori_loop(0, rows, issue, 0)

    def drain(r, c):
        _row_copy(x_hbm, 0, o_hbm, base + r, sem).wait()
        return c
    lax.fori_loop(0, rows, drain, 0)


def _dispatch(h2, src, p_max, rows):
    d = h2.shape[1]
    kern = functools.partial(_dispatch_kernel, rows=rows)
    return pl.pallas_call(
        kern,
        grid_spec=pltpu.PrefetchScalarGridSpec(
            num_scalar_prefetch=1,
            grid=(p_max // rows,),
            in_specs=[pl.BlockSpec(memory_space=pl.ANY)],
            out_specs=pl.BlockSpec(memory_space=pl.ANY),
            scratch_shapes=[pltpu.SemaphoreType.DMA(())]),
        out_shape=jax.ShapeDtypeStruct((p_max, d), h2.dtype),
        compiler_params=pltpu.CompilerParams(dimension_semantics=("arbitrary",),
                                             disable_bounds_checks=True),
        name="moe_dispatch",
    )(src, h2)


def _combine_kernel(d1_ref, d2_ref, y_hbm, r_ref, xm_ref, gate_ref, n3_ref, xo_ref, buf_ref, sem, *, tm):
    base = pl.program_id(0) * tm

    def issue(r, c):
        _row_copy(y_hbm, d1_ref[base + r], buf_ref.at[0], r, sem).start()
        _row_copy(y_hbm, d2_ref[base + r], buf_ref.at[1], r, sem).start()
        return c
    lax.fori_loop(0, tm, issue, 0)

    def drain(r, c):
        _row_copy(y_hbm, 0, buf_ref.at[0], r, sem).wait()
        _row_copy(y_hbm, 0, buf_ref.at[1], r, sem).wait()
        return c
    lax.fori_loop(0, tm, drain, 0)

    route = r_ref[...]
    w1 = route[:, 4:5]
    w2 = route[:, 5:6]
    y = w1 * buf_ref[0] + w2 * buf_ref[1]
    xo_ref[...] = xm_ref[...] + gate_ref[...] * _rms(y, n3_ref[...])


def _combine(yo, dest1, dest2, route, xm, n3, mod, row_fn, tm):
    n, d = xm.shape
    kern = functools.partial(_combine_kernel, tm=tm)
    return pl.pallas_call(
        kern,
        grid_spec=pltpu.PrefetchScalarGridSpec(
            num_scalar_prefetch=2,
            grid=(n // tm,),
            in_specs=[pl.BlockSpec(memory_space=pl.ANY),
                      pl.BlockSpec((tm, LANES), lambda i, *_: (i, 0)),
                      pl.BlockSpec((tm, d), lambda i, *_: (i, 0)),
                      _mod_spec(d, M_G2, row_fn),
                      _vec_spec(d)],
            out_specs=pl.BlockSpec((tm, d), lambda i, *_: (i, 0)),
            scratch_shapes=[pltpu.VMEM((2, tm, d), F32), pltpu.SemaphoreType.DMA(())]),
        out_shape=jax.ShapeDtypeStruct((n, d), F32),
        compiler_params=pltpu.CompilerParams(dimension_semantics=("arbitrary",),
                                             vmem_limit_bytes=VMEM_LIMIT,
                                             disable_bounds_checks=True),
        name="moe_combine",
    )(dest1, dest2, yo, route, xm, mod, n3.reshape(1, d))


def _moe(h2, xm, w_router, wg, wu, wd, n3, mod, row_fn, tm_g, tf):
    n, d = h2.shape
    n_experts = w_router.shape[1]
    route, counts = _router(h2, w_router, min(512, n))
    e1 = route[:, 0].astype(I32)
    e2 = route[:, 1].astype(I32)
    r1 = route[:, 2].astype(I32)
    r2 = route[:, 3].astype(I32)
    cnt = counts[0, :n_experts].astype(I32)
    padded = ((cnt + tm_g - 1) // tm_g) * tm_g
    ends = jnp.cumsum(padded)
    offs = ends - padded
    dest1 = offs[e1] + r1
    dest2 = offs[e2] + r2
    p_max = ((TOP_K * n + n_experts * (tm_g - 1)) // tm_g) * tm_g
    n_rows = ends[-1:]
    tok = jnp.arange(n, dtype=I32)
    src = jnp.zeros((p_max,), I32).at[dest1].set(tok).at[dest2].set(tok)
    tile_start = jnp.arange(p_max // tm_g, dtype=I32) * tm_g
    tile_expert = jnp.minimum(jnp.sum(tile_start[:, None] >= ends[None, :], axis=1),
                              n_experts - 1).astype(I32)
    last_expert = tile_expert[jnp.maximum(n_rows[0] // tm_g - 1, 0)]
    tile_expert = jnp.where(tile_start < n_rows[0], tile_expert, last_expert)
    xs = _dispatch(h2, src, p_max, min(256, tm_g))
    yo = _ffn_grouped(xs, wg, wu, wd, tile_expert, n_rows // tm_g, tm_g, tf)
    return _combine(yo, dest1, dest2, route, xm, n3, mod, row_fn, min(256, n))


def _rope_tables(seq):
    inv = ROPE_BASE ** (-jnp.arange(ROPE_FREQ, dtype=F32) / ROPE_FREQ)
    t = jnp.arange(seq, dtype=I32)
    pos = jnp.stack([t // GRID_W, t % GRID_W], axis=-1).astype(F32)
    ang = pos[:, :, None] * inv
    cos, sin = jnp.cos(ang), jnp.sin(ang)
    cos = jnp.concatenate([cos, cos], axis=-1).reshape(seq, DIFF_HEAD_DIM)
    sin = jnp.concatenate([-sin, sin], axis=-1).reshape(seq, DIFF_HEAD_DIM)
    reps = LANES // DIFF_HEAD_DIM
    return jnp.tile(cos, (1, reps)), jnp.tile(sin, (1, reps))


def _pad_cols(w, mult):
    pad = (-w.shape[-1]) % mult
    return jnp.pad(w, [(0, 0)] * (w.ndim - 1) + [(0, pad)]) if pad else w


def _pad_rows(w, mult):
    pad = (-w.shape[-2]) % mult
    return jnp.pad(w, [(0, 0)] * (w.ndim - 2) + [(0, pad), (0, 0)]) if pad else w


def kernel(x, c, ctx, c_ctx, norm_g, w_ada, b_ada, w_in, diff_lambda, diff_subln_g, lru_conv_w, lru_conv_b, lru_gate_w, lru_gate_b, lru_lambda, na_rpb, w_branch, w_merge, b_merge, w_out, ffn_w_gate, ffn_w_up, ffn_w_down, moe_w_router, moe_w_gate, moe_w_up, moe_w_down):
    bsz, seq, d = x.shape
    clen = ctx.shape[1]
    depth = w_in.shape[0]
    n, nc = bsz * seq, bsz * clen
    x_lat = x.reshape(n, d)
    x_ctx = ctx.reshape(nc, d)

    mr = -(-(bsz + 1) // SUBLANES) * SUBLANES
    cvec = jnp.concatenate([c, c_ctx[None], jnp.zeros((mr - bsz - 1, d), F32)], axis=0)
    mod_all = _ada(cvec, w_ada, b_ada).reshape(depth, mr, 1, 6 * d)
    cos_t, sin_t = _rope_tables(seq)

    tm_lat, tm_ctx = min(1024, seq), min(1024, nc, seq)
    tm_mix_lat, tm_mix_ctx = min(256, seq), min(256, nc)
    tm_ffn_lat, tm_ffn_ctx = min(512, seq), min(512, nc)
    tf = min(512, d)
    tq = min(256, seq)

    def lat_row(tile_rows):
        per_batch = seq // tile_rows
        return lambda i: i // per_batch

    def ctx_row(i):
        return bsz

    h_lat = _prenorm(x_lat, norm_g[0, 0], mod_all[0], lat_row(tm_lat), tm_lat)
    h_ctx = _prenorm(x_ctx, norm_g[0, 0], mod_all[0], ctx_row, tm_ctx)

    for l in range(depth):
        need_ctx = l < depth - 1
        mod = mod_all[l]
        lam_init = 0.8 - 0.6 * math.exp(-0.3 * l)
        wcat = jnp.concatenate([w_in[l], w_merge[l]], axis=1).astype(BF16)
        bcat = jnp.concatenate([jnp.zeros((IN_WIDTH,), F32), b_merge[l]])[None]
        p_lat = _proj(h_lat, wcat, bcat, cos_t, sin_t, seq, True, tm_lat)
        p_ctx = _proj(h_ctx, wcat, bcat, cos_t, sin_t, seq, False, tm_ctx)

        y_diff = _diff_attn_lat(p_lat, p_ctx, diff_lambda[l], diff_subln_g[l], lam_init,
                                bsz, seq, clen, tq)
        y_lru, y_lru_c = _lru(p_lat, p_ctx, lru_conv_w[l], lru_conv_b[l], lru_gate_w[l],
                              lru_gate_b[l], lru_lambda[l], bsz, seq, clen, need_ctx)
        y_na = _na_lat(p_lat, p_ctx, na_rpb[l], bsz, seq, clen)

        wb = w_branch[l].astype(BF16)
        wo = w_out[l].astype(BF16)
        is_moe = l % 2 == 1
        h2_dtype = F32 if is_moe else BF16
        xm_lat, h2_lat = _merge(y_diff, y_lru, y_na, p_lat, wb, wo, x_lat, norm_g[l, 1],
                                norm_g[l, 2], mod, lat_row(tm_mix_lat), tm_mix_lat, h2_dtype)
        if need_ctx:
            y_diff_c = _diff_attn_ctx(p_ctx, diff_lambda[l], diff_subln_g[l], lam_init, bsz, clen)
            y_na_c = _na_ctx(p_ctx, bsz, clen)
            xm_ctx, h2_ctx = _merge(y_diff_c, y_lru_c, y_na_c, p_ctx, wb, wo, x_ctx, norm_g[l, 1],
                                    norm_g[l, 2], mod, ctx_row, tm_mix_ctx, h2_dtype)

        nxt = (norm_g[l + 1, 0], mod_all[l + 1]) if need_ctx else None
        i = l // 2
        if is_moe:
            wg = moe_w_gate[i].astype(BF16)
            wu = moe_w_up[i].astype(BF16)
            wd = moe_w_down[i].astype(BF16)
            tm_g = min(512, n)
            x_lat = _moe(h2_lat, xm_lat, moe_w_router[i], wg, wu, wd, norm_g[l, 3], mod,
                         lat_row(min(256, n)), tm_g, tf)
            if need_ctx:
                x_ctx = _moe(h2_ctx, xm_ctx, moe_w_router[i], wg, wu, wd, norm_g[l, 3], mod,
                             ctx_row, min(512, nc), tf)
                h_lat = _prenorm(x_lat, nxt[0], nxt[1], lat_row(tm_lat), tm_lat)
                h_ctx = _prenorm(x_ctx, nxt[0], nxt[1], ctx_row, tm_ctx)
        else:
            wg = _pad_cols(ffn_w_gate[i], tf).astype(BF16)
            wu = _pad_cols(ffn_w_up[i], tf).astype(BF16)
            wd = _pad_rows(ffn_w_down[i], tf).astype(BF16)
            x_lat, h_lat = _ffn_dense(h2_lat, wg, wu, wd, xm_lat, norm_g[l, 3], mod,
                                      lat_row(tm_ffn_lat), tm_ffn_lat, tf, nxt)
            if need_ctx:
                x_ctx, h_ctx = _ffn_dense(h2_ctx, wg, wu, wd, xm_ctx, norm_g[l, 3], mod,
                                          ctx_row, tm_ffn_ctx, tf, nxt)
    return x_lat.reshape(bsz, seq, d)
```

```python
import functools
import math

import numpy as np
import jax
import jax.numpy as jnp
from jax import lax
from jax.experimental import pallas as pl
from jax.experimental.pallas import tpu as pltpu

F32 = jnp.float32
BF16 = jnp.bfloat16
I32 = jnp.int32

EPS = 1e-6
GRID_W = 64
BRANCH_WIDTH = 1024
N_IN_PARTS = 8
IN_WIDTH = N_IN_PARTS * BRANCH_WIDTH
DIFF_HEADS = 8
DIFF_HEAD_DIM = 64
DIFF_V_DIM = 2 * DIFF_HEAD_DIM
ROPE_BASE = 10000.0
ROPE_FREQ = DIFF_HEAD_DIM // 4
LRU_BLOCKS = 8
LRU_BLOCK_DIM = BRANCH_WIDTH // LRU_BLOCKS
LRU_CONV_W = 4
LRU_C = 8.0
NA_HEADS = 8
NA_HEAD_DIM = BRANCH_WIDTH // NA_HEADS
NA_WIN_R = 8
NA_WIN_C = 16
TOP_K = 2

LANES = 128
SUBLANES = 8
VMEM_LIMIT = 56 * 1024 * 1024
NEG = -1e30
LOG2E = 1.0 / math.log(2.0)
DIFF_QSCALE = DIFF_HEAD_DIM ** -0.5 * LOG2E
NA_QSCALE = NA_HEAD_DIM ** -0.5 * LOG2E

P_DQ, P_DK, P_DV, P_LX, P_LG, P_NQ, P_NK, P_NV = range(8)
M_SH1, M_SC1, M_G1, M_SH2, M_SC2, M_G2 = range(6)


def _params(sem):
    return pltpu.CompilerParams(dimension_semantics=sem, vmem_limit_bytes=VMEM_LIMIT)


def _rms(x, g):
    return x * lax.rsqrt(jnp.mean(x * x, axis=-1, keepdims=True) + EPS) * g


def _dot(a, b):
    return jnp.dot(a, b, preferred_element_type=F32)


def _dot_nt(a, b):
    return lax.dot_general(a, b, (((1,), (1,)), ((), ())), preferred_element_type=F32)


def _ada_kernel(c_ref, w_ref, b_ref, o_ref):
    c = c_ref[...]
    s = (c * jax.nn.sigmoid(c)).astype(BF16)
    o_ref[...] = _dot(s, w_ref[...].astype(BF16)) + b_ref[...]


def _ada(cvec, w_ada, b_ada):
    n_layers, d, n6 = w_ada.shape
    mr = cvec.shape[0]
    tn = min(1024, d)
    return pl.pallas_call(
        _ada_kernel,
        grid=(n_layers, n6 // tn),
        in_specs=[pl.BlockSpec((mr, d), lambda l, j: (0, 0)),
                  pl.BlockSpec((None, d, tn), lambda l, j: (l, 0, j)),
                  pl.BlockSpec((None, 1, tn), lambda l, j: (l, 0, j))],
        out_specs=pl.BlockSpec((None, mr, tn), lambda l, j: (l, 0, j)),
        out_shape=jax.ShapeDtypeStruct((n_layers, mr, n6), F32),
        compiler_params=_params(("arbitrary", "arbitrary")),
        name="ada_mod",
    )(cvec, w_ada, b_ada.reshape(n_layers, 1, n6))


def _mod_spec(d, chunk, row_fn):
    return pl.BlockSpec((None, 1, d), lambda i, *_: (row_fn(i), 0, chunk))


def _vec_spec(d):
    return pl.BlockSpec((1, d), lambda i, *_: (0, 0))


def _prenorm_kernel(x_ref, g_ref, sc_ref, sh_ref, o_ref):
    y = _rms(x_ref[...], g_ref[...])
    o_ref[...] = (y * (1.0 + sc_ref[...]) + sh_ref[...]).astype(o_ref.dtype)


def _prenorm(x, g, mod, row_fn, tm):
    n, d = x.shape
    return pl.pallas_call(
        _prenorm_kernel,
        grid=(n // tm,),
        in_specs=[pl.BlockSpec((tm, d), lambda i: (i, 0)),
                  _vec_spec(d),
                  _mod_spec(d, M_SC1, row_fn),
                  _mod_spec(d, M_SH1, row_fn)],
        out_specs=pl.BlockSpec((tm, d), lambda i: (i, 0)),
        out_shape=jax.ShapeDtypeStruct((n, d), BF16),
        compiler_params=_params(("parallel",)),
        name="prenorm",
    )(x, g.reshape(1, d), mod, mod)


def _proj_kernel(h_ref, w_ref, b_ref, cos_ref, sin_ref, o_ref, *, rope, tiles_per_part):
    part = pl.program_id(1) // tiles_per_part
    acc = _dot(h_ref[...], w_ref[...])
    tn = acc.shape[1]

    def rotated():
        lane = lax.broadcasted_iota(I32, acc.shape, 1)
        first_half = (lane % (2 * ROPE_FREQ)) < ROPE_FREQ
        partner = jnp.where(first_half, pltpu.roll(acc, tn - ROPE_FREQ, 1),
                            pltpu.roll(acc, ROPE_FREQ, 1))
        cos = jnp.tile(cos_ref[...], (1, tn // LANES))
        sin = jnp.tile(sin_ref[...], (1, tn // LANES))
        return acc * cos + partner * sin

    def emit(cond, fn):
        @pl.when(cond)
        def _():
            o_ref[...] = fn().astype(o_ref.dtype)

    emit(part == P_DQ, lambda: (rotated() if rope else acc) * DIFF_QSCALE)
    emit(part == P_DK, lambda: rotated() if rope else acc)
    emit(part == P_NQ, lambda: acc * NA_QSCALE)
    emit((part > P_DK) & (part != P_NQ) & (part < N_IN_PARTS), lambda: acc)
    emit(part >= N_IN_PARTS, lambda: jax.nn.sigmoid(acc + b_ref[...]))


def _proj(h, wcat, bcat, cos_t, sin_t, seq, rope, tm):
    n, d = h.shape
    nw = wcat.shape[1]
    tn = min(1024, d)
    tpb = seq // tm
    kern = functools.partial(_proj_kernel, rope=rope, tiles_per_part=BRANCH_WIDTH // tn)
    return pl.pallas_call(
        kern,
        grid=(n // tm, nw // tn),
        in_specs=[pl.BlockSpec((tm, d), lambda i, j: (i, 0)),
                  pl.BlockSpec((d, tn), lambda i, j: (0, j)),
                  pl.BlockSpec((1, tn), lambda i, j: (0, j)),
                  pl.BlockSpec((tm, LANES), lambda i, j: (i % tpb, 0)),
                  pl.BlockSpec((tm, LANES), lambda i, j: (i % tpb, 0))],
        out_specs=pl.BlockSpec((tm, tn), lambda i, j: (i, j)),
        out_shape=jax.ShapeDtypeStruct((n, nw), BF16),
        compiler_params=_params(("parallel", "arbitrary")),
        name="proj_rope" if rope else "proj",
    )(h, wcat, bcat, cos_t, sin_t)


def _diff_kernel(*refs, lam_init, has_lat):
    if has_lat:
        lamv_ref, sg_ref, q_ref, k_ref, v_ref, kc_ref, vc_ref, o_ref = refs
    else:
        lamv_ref, sg_ref, q_ref, kc_ref, vc_ref, o_ref = refs
    lv = lamv_ref[...]
    lam = (jnp.exp(jnp.sum(lv[0:1] * lv[1:2], axis=-1, keepdims=True))
           - jnp.exp(jnp.sum(lv[2:3] * lv[3:4], axis=-1, keepdims=True)) + lam_init)
    q = q_ref[...]
    kc = kc_ref[...]

    def scores(c):
        sl = slice(c * DIFF_HEAD_DIM, (c + 1) * DIFF_HEAD_DIM)
        qc = q[:, sl]
        s_c = _dot_nt(qc, kc[:, sl])
        m = jnp.max(s_c, axis=-1, keepdims=True)
        e_l = None
        if has_lat:
            s_l = _dot_nt(qc, k_ref[:, sl])
            m = jnp.maximum(m, jnp.max(s_l, axis=-1, keepdims=True))
            e_l = jnp.exp2(s_l - m)
        e_c = jnp.exp2(s_c - m)
        den = jnp.sum(e_c, axis=-1, keepdims=True)
        if has_lat:
            den = den + jnp.sum(e_l, axis=-1, keepdims=True)
        return e_l, e_c, den

    e1l, e1c, den1 = scores(0)
    e2l, e2c, den2 = scores(1)
    ratio = lam * den1 * (1.0 / den2)
    o = _dot((e1c - ratio * e2c).astype(BF16), vc_ref[...])
    if has_lat:
        o = o + _dot((e1l - ratio * e2l).astype(BF16), v_ref[...])
    o = o * (1.0 / den1)
    y = _rms(o, sg_ref[...]) * (1.0 - lam_init)
    o_ref[...] = y.astype(o_ref.dtype)


def _diff_attn_lat(p_lat, p_ctx, lamv, subln_g, lam_init, bsz, seq, clen, tq):
    n = p_lat.shape[0]
    nq = seq // tq
    hw = DIFF_V_DIM
    hpb = BRANCH_WIDTH // hw
    kern = functools.partial(_diff_kernel, lam_init=lam_init, has_lat=True)
    return pl.pallas_call(
        kern,
        grid=(bsz, DIFF_HEADS, nq),
        in_specs=[pl.BlockSpec((4, DIFF_HEAD_DIM), lambda b, h, i: (0, 0)),
                  pl.BlockSpec((1, hw), lambda b, h, i: (0, 0)),
                  pl.BlockSpec((tq, hw), lambda b, h, i: (b * nq + i, P_DQ * hpb + h)),
                  pl.BlockSpec((seq, hw), lambda b, h, i: (b, P_DK * hpb + h)),
                  pl.BlockSpec((seq, hw), lambda b, h, i: (b, P_DV * hpb + h)),
                  pl.BlockSpec((clen, hw), lambda b, h, i: (b, P_DK * hpb + h)),
                  pl.BlockSpec((clen, hw), lambda b, h, i: (b, P_DV * hpb + h))],
        out_specs=pl.BlockSpec((tq, hw), lambda b, h, i: (b * nq + i, h)),
        out_shape=jax.ShapeDtypeStruct((n, BRANCH_WIDTH), BF16),
        compiler_params=_params(("parallel", "parallel", "arbitrary")),
        name="diff_attn",
    )(lamv, subln_g.reshape(1, hw), p_lat, p_lat, p_lat, p_ctx, p_ctx)


def _diff_attn_ctx(p_ctx, lamv, subln_g, lam_init, bsz, clen):
    hw = DIFF_V_DIM
    hpb = BRANCH_WIDTH // hw
    kern = functools.partial(_diff_kernel, lam_init=lam_init, has_lat=False)
    return pl.pallas_call(
        kern,
        grid=(bsz, DIFF_HEADS),
        in_specs=[pl.BlockSpec((4, DIFF_HEAD_DIM), lambda b, h: (0, 0)),
                  pl.BlockSpec((1, hw), lambda b, h: (0, 0)),
                  pl.BlockSpec((clen, hw), lambda b, h: (b, P_DQ * hpb + h)),
                  pl.BlockSpec((clen, hw), lambda b, h: (b, P_DK * hpb + h)),
                  pl.BlockSpec((clen, hw), lambda b, h: (b, P_DV * hpb + h))],
        out_specs=pl.BlockSpec((clen, hw), lambda b, h: (b, h)),
        out_shape=jax.ShapeDtypeStruct((bsz * clen, BRANCH_WIDTH), BF16),
        compiler_params=_params(("parallel", "parallel")),
        name="diff_attn_ctx",
    )(lamv, subln_g.reshape(1, hw), p_ctx, p_ctx, p_ctx)


LRU_TW = 256


def _gelu_tanh(x):
    return 0.5 * x * (1.0 + jnp.tanh(math.sqrt(2.0 / math.pi) * (x + 0.044715 * (x * x * x))))


def _tile_scan(a, b, reverse):
    rows = lax.broadcasted_iota(I32, a.shape, 0)
    for s in (1, 2, 4):
        if reverse:
            a_s = pltpu.roll(a, SUBLANES - s, 0)
            b_s = pltpu.roll(b, SUBLANES - s, 0)
            valid = rows < SUBLANES - s
        else:
            a_s = pltpu.roll(a, s, 0)
            b_s = pltpu.roll(b, s, 0)
            valid = rows >= s
        b = jnp.where(valid, a * b_s + b, b)
        a = jnp.where(valid, a * a_s, a)
    return a, b


def _lru_kernel(xl_ref, gl_ref, xc_ref, gc_ref, cw_ref, cb_ref, gw_ref, gb_ref, lam_ref,
                *out_and_scratch, seq, clen, need_ctx):
    if need_ctx:
        yl_ref, yc_ref, af_ref, bf_ref, ab_ref, bb_ref, hf_ref = out_and_scratch
    else:
        yl_ref, af_ref, bf_ref, ab_ref, bb_ref, hf_ref = out_and_scratch
        yc_ref = None
    tot = clen + seq
    x = jnp.concatenate([xc_ref[...], xl_ref[...]], axis=0).astype(F32)
    row = lax.broadcasted_iota(I32, x.shape, 0)
    pos = jnp.where(row < clen, row, row - clen)
    seg_len = jnp.where(row < clen, clen, seq)
    cw = cw_ref[...]
    conv = cb_ref[...] + x * cw[2:3]
    conv = conv + jnp.where(pos >= 2, pltpu.roll(x, 2, 0), 0.0) * cw[0:1]
    conv = conv + jnp.where(pos >= 1, pltpu.roll(x, 1, 0), 0.0) * cw[1:2]
    conv = conv + jnp.where(pos < seg_len - 1, pltpu.roll(x, tot - 1, 0), 0.0) * cw[3:4]
    xb = conv.astype(BF16)
    gb = gb_ref[...]
    lam = lam_ref[...]
    for d, (a_ref, b_ref) in enumerate(((af_ref, bf_ref), (ab_ref, bb_ref))):
        r = jax.nn.sigmoid(_dot(xb, gw_ref[d, 0]) + gb[2 * d:2 * d + 1])
        i = jax.nn.sigmoid(_dot(xb, gw_ref[d, 1]) + gb[2 * d + 1:2 * d + 2])
        z = -lam[d:d + 1]
        softplus = jnp.maximum(z, 0.0) + jnp.log1p(jnp.exp(-jnp.abs(z)))
        log_a = -LRU_C * r * softplus
        a = jnp.exp(log_a)
        beta = jnp.sqrt(-jnp.tanh(log_a) * (a * a + 1.0))
        a_ref[...] = a
        b_ref[...] = beta * i * conv

    n_ct = clen // SUBLANES
    n_t = tot // SUBLANES
    w = x.shape[1]

    def fwd_step(t, carry):
        off = pl.multiple_of(t * SUBLANES, SUBLANES)
        a, b = _tile_scan(af_ref[pl.ds(off, SUBLANES), :], bf_ref[pl.ds(off, SUBLANES), :], False)
        h = a * carry + b
        hf_ref[pl.ds(off, SUBLANES), :] = h
        return h[SUBLANES - 1:SUBLANES, :]

    lax.fori_loop(0, n_t, fwd_step, jnp.zeros((1, w), F32))

    def bwd_tile(off, carry):
        a, b = _tile_scan(ab_ref[pl.ds(off, SUBLANES), :], bb_ref[pl.ds(off, SUBLANES), :], True)
        h = a * carry + b
        return h, h[0:1, :]

    def bwd_ctx_step(t, carry):
        off = pl.multiple_of((n_ct - 1 - t) * SUBLANES, SUBLANES)
        h, carry = bwd_tile(off, carry)
        if need_ctx:
            g = gc_ref[pl.ds(off, SUBLANES), :].astype(F32)
            yc_ref[pl.ds(off, SUBLANES), :] = (
                (hf_ref[pl.ds(off, SUBLANES), :] + h) * _gelu_tanh(g)).astype(yc_ref.dtype)
        return carry

    carry = lax.fori_loop(0, n_ct, bwd_ctx_step, jnp.zeros((1, w), F32))

    def bwd_lat_step(t, carry):
        off = pl.multiple_of((n_t - 1 - t) * SUBLANES, SUBLANES)
        h, carry = bwd_tile(off, carry)
        lo = pl.multiple_of(off - clen, SUBLANES)
        g = gl_ref[pl.ds(lo, SUBLANES), :].astype(F32)
        yl_ref[pl.ds(lo, SUBLANES), :] = (
            (hf_ref[pl.ds(off, SUBLANES), :] + h) * _gelu_tanh(g)).astype(yl_ref.dtype)
        return carry

    lax.fori_loop(0, n_t - n_ct, bwd_lat_step, carry)


def _lru(p_lat, p_ctx, conv_w, conv_b, gate_w, gate_b, lru_lam, bsz, seq, clen, need_ctx):
    n = p_lat.shape[0]
    tw = LRU_TW
    nwt = BRANCH_WIDTH // tw
    per = tw // LRU_BLOCK_DIM
    gw = gate_w.reshape(2, 2, nwt, per, LRU_BLOCK_DIM, LRU_BLOCK_DIM)
    eye = jnp.eye(per, dtype=gate_w.dtype)
    gw = jnp.einsum('dgnpcf,pq->dgnpcqf', gw, eye).reshape(2, 2, nwt, tw, tw).astype(BF16)
    gb = gate_b.reshape(4, BRANCH_WIDTH)
    tot = seq + clen
    kern = functools.partial(_lru_kernel, seq=seq, clen=clen, need_ctx=need_ctx)
    out_shape = [jax.ShapeDtypeStruct((n, BRANCH_WIDTH), BF16)]
    out_specs = [pl.BlockSpec((seq, tw), lambda b, j: (b, j))]
    if need_ctx:
        out_shape.append(jax.ShapeDtypeStruct((bsz * clen, BRANCH_WIDTH), BF16))
        out_specs.append(pl.BlockSpec((clen, tw), lambda b, j: (b, j)))
    res = pl.pallas_call(
        kern,
        grid=(bsz, nwt),
        in_specs=[pl.BlockSpec((seq, tw), lambda b, j: (b, P_LX * nwt + j)),
                  pl.BlockSpec((seq, tw), lambda b, j: (b, P_LG * nwt + j)),
                  pl.BlockSpec((clen, tw), lambda b, j: (b, P_LX * nwt + j)),
                  pl.BlockSpec((clen, tw), lambda b, j: (b, P_LG * nwt + j)),
                  pl.BlockSpec((LRU_CONV_W, tw), lambda b, j: (0, j)),
                  pl.BlockSpec((1, tw), lambda b, j: (0, j)),
                  pl.BlockSpec((2, 2, None, tw, tw), lambda b, j: (0, 0, j, 0, 0)),
                  pl.BlockSpec((4, tw), lambda b, j: (0, j)),
                  pl.BlockSpec((2, tw), lambda b, j: (0, j))],
        out_specs=out_specs,
        out_shape=out_shape,
        scratch_shapes=[pltpu.VMEM((tot, tw), F32) for _ in range(5)],
        compiler_params=_params(("parallel", "parallel")),
        name="rglru",
    )(p_lat, p_lat, p_ctx, p_ctx, conv_w, conv_b.reshape(1, BRANCH_WIDTH), gw, gb, lru_lam)
    return (res[0], res[1]) if need_ctx else (res[0], None)


NA_ROWS_PER_TILE = 2


def _na_tables(rows, rpt):
    kr = min(NA_WIN_R, rows)
    nkr = min(rows, rpt + kr - 1)
    variants, index, tile_variant, tile_kb = [], {}, [], []
    for t in range(rows // rpt):
        r0 = t * rpt
        kb = int(np.clip(r0 - kr // 2, 0, rows - nkr))
        geom = []
        for a in range(rpt):
            r = r0 + a
            rs = int(np.clip(r - kr // 2, 0, rows - kr))
            geom.append(tuple((kb + k - r + NA_WIN_R - 1) if rs <= kb + k < rs + kr else None
                              for k in range(nkr)))
        geom = tuple(geom)
        if geom not in index:
            index[geom] = len(variants)
            variants.append(geom)
        tile_variant.append(index[geom])
        tile_kb.append(kb)
    return nkr, np.array(tile_variant, np.int32), np.array(tile_kb, np.int32), variants


def _na_bias(rpb, variants):
    heads = rpb.shape[0]
    pad = GRID_W - NA_WIN_C
    padded = jnp.pad(rpb * LOG2E, ((0, 0), (0, 0), (pad, pad)))
    toep = jnp.stack([padded[..., GRID_W - 1 - qc:2 * GRID_W - 1 - qc] for qc in range(GRID_W)],
                     axis=2)
    col = np.arange(GRID_W)
    cs = np.clip(col - NA_WIN_C // 2, 0, GRID_W - NA_WIN_C)
    in_cols = (col[None, :] >= cs[:, None]) & (col[None, :] < cs[:, None] + NA_WIN_C)
    toep = jnp.where(in_cols, toep, NEG).astype(F32)
    outside = jnp.full((heads, GRID_W, GRID_W), NEG, F32)
    return jnp.stack([
        jnp.concatenate([
            jnp.concatenate([outside if dr is None else toep[:, dr] for dr in row], axis=-1)
            for row in geom], axis=-2)
        for geom in variants])


def _na_kernel(var_ref, kb_ref, q_ref, k_ref, v_ref, kc_ref, vc_ref, bias_ref, o_ref, *, nk):
    t = pl.program_id(1)
    start = pl.multiple_of(kb_ref[t] * GRID_W, GRID_W)
    for h in range(NA_HEADS):
        sl = slice(h * NA_HEAD_DIM, (h + 1) * NA_HEAD_DIM)
        qh = q_ref[:, sl]
        s_w = _dot_nt(qh, k_ref[pl.ds(start, nk), sl]) + bias_ref[h]
        s_c = _dot_nt(qh, kc_ref[:, sl])
        m = jnp.maximum(jnp.max(s_w, axis=-1, keepdims=True), jnp.max(s_c, axis=-1, keepdims=True))
        e_w = jnp.exp2(s_w - m)
        e_c = jnp.exp2(s_c - m)
        inv = 1.0 / (jnp.sum(e_w, axis=-1, keepdims=True) + jnp.sum(e_c, axis=-1, keepdims=True))
        o = (_dot(e_w.astype(BF16), v_ref[pl.ds(start, nk), sl])
             + _dot(e_c.astype(BF16), vc_ref[:, sl]))
        o_ref[:, sl] = (o * inv).astype(o_ref.dtype)


def _na_lat(p_lat, p_ctx, rpb, bsz, seq, clen):
    n = p_lat.shape[0]
    rows = seq // GRID_W
    rpt = min(NA_ROWS_PER_TILE, rows)
    nkr, tile_variant, tile_kb, variants = _na_tables(rows, rpt)
    bias = _na_bias(rpb, variants)
    nt = rows // rpt
    rq, nk = rpt * GRID_W, nkr * GRID_W
    bw = BRANCH_WIDTH
    kern = functools.partial(_na_kernel, nk=nk)
    return pl.pallas_call(
        kern,
        grid_spec=pltpu.PrefetchScalarGridSpec(
            num_scalar_prefetch=2,
            grid=(bsz, nt),
            in_specs=[pl.BlockSpec((rq, bw), lambda b, t, var, kb: (b * nt + t, P_NQ)),
                      pl.BlockSpec((seq, bw), lambda b, t, var, kb: (b, P_NK)),
                      pl.BlockSpec((seq, bw), lambda b, t, var, kb: (b, P_NV)),
                      pl.BlockSpec((clen, bw), lambda b, t, var, kb: (b, P_NK)),
                      pl.BlockSpec((clen, bw), lambda b, t, var, kb: (b, P_NV)),
                      pl.BlockSpec((None, NA_HEADS, rq, nk), lambda b, t, var, kb: (var[t], 0, 0, 0))],
            out_specs=pl.BlockSpec((rq, bw), lambda b, t, var, kb: (b * nt + t, 0))),
        out_shape=jax.ShapeDtypeStruct((n, bw), BF16),
        compiler_params=_params(("parallel", "arbitrary")),
        name="na_attn",
    )(jnp.asarray(tile_variant), jnp.asarray(tile_kb), p_lat, p_lat, p_lat, p_ctx, p_ctx, bias)


def _na_ctx_kernel(q_ref, k_ref, v_ref, o_ref):
    for h in range(NA_HEADS):
        sl = slice(h * NA_HEAD_DIM, (h + 1) * NA_HEAD_DIM)
        s = _dot_nt(q_ref[:, sl], k_ref[:, sl])
        e = jnp.exp2(s - jnp.max(s, axis=-1, keepdims=True))
        inv = 1.0 / jnp.sum(e, axis=-1, keepdims=True)
        o_ref[:, sl] = (_dot(e.astype(BF16), v_ref[:, sl]) * inv).astype(o_ref.dtype)


def _na_ctx(p_ctx, bsz, clen):
    bw = BRANCH_WIDTH
    return pl.pallas_call(
        _na_ctx_kernel,
        grid=(bsz,),
        in_specs=[pl.BlockSpec((clen, bw), lambda b: (b, P_NQ)),
                  pl.BlockSpec((clen, bw), lambda b: (b, P_NK)),
                  pl.BlockSpec((clen, bw), lambda b: (b, P_NV))],
        out_specs=pl.BlockSpec((clen, bw), lambda b: (b, 0)),
        out_shape=jax.ShapeDtypeStruct((bsz * clen, bw), BF16),
        compiler_params=_params(("parallel",)),
        name="na_attn_ctx",
    )(p_ctx, p_ctx, p_ctx)


def _merge_kernel(yd_ref, yl_ref, yn_ref, g0_ref, g1_ref, g2_ref, wb_ref, wo_ref, x_ref,
                  n1_ref, gate_ref, n2_ref, sc_ref, sh_ref, xm_ref, h2_ref):
    m = (g0_ref[...].astype(F32) * _dot(yd_ref[...], wb_ref[0])
         + g1_ref[...].astype(F32) * _dot(yl_ref[...], wb_ref[1])
         + g2_ref[...].astype(F32) * _dot(yn_ref[...], wb_ref[2]))
    mo = _dot(m.astype(BF16), wo_ref[...])
    xm = x_ref[...] + gate_ref[...] * _rms(mo, n1_ref[...])
    xm_ref[...] = xm
    h2 = _rms(xm, n2_ref[...]) * (1.0 + sc_ref[...]) + sh_ref[...]
    h2_ref[...] = h2.astype(h2_ref.dtype)


def _merge(yd, yl, yn, p, wb, wo, x, n1, n2, mod, row_fn, tm, h2_dtype):
    n, d = x.shape
    bw = BRANCH_WIDTH
    gcol = IN_WIDTH // d
    resident = dict(pipeline_mode=pl.Buffered(1))
    return pl.pallas_call(
        _merge_kernel,
        grid=(n // tm,),
        in_specs=[pl.BlockSpec((tm, bw), lambda i: (i, 0)),
                  pl.BlockSpec((tm, bw), lambda i: (i, 0)),
                  pl.BlockSpec((tm, bw), lambda i: (i, 0)),
                  pl.BlockSpec((tm, d), lambda i: (i, gcol)),
                  pl.BlockSpec((tm, d), lambda i: (i, gcol + 1)),
                  pl.BlockSpec((tm, d), lambda i: (i, gcol + 2)),
                  pl.BlockSpec((3, bw, d), lambda i: (0, 0, 0), **resident),
                  pl.BlockSpec((d, d), lambda i: (0, 0), **resident),
                  pl.BlockSpec((tm, d), lambda i: (i, 0)),
                  _vec_spec(d),
                  _mod_spec(d, M_G1, row_fn),
                  _vec_spec(d),
                  _mod_spec(d, M_SC2, row_fn),
                  _mod_spec(d, M_SH2, row_fn)],
        out_specs=[pl.BlockSpec((tm, d), lambda i: (i, 0)),
                   pl.BlockSpec((tm, d), lambda i: (i, 0))],
        out_shape=[jax.ShapeDtypeStruct((n, d), F32),
                   jax.ShapeDtypeStruct((n, d), h2_dtype)],
        compiler_params=_params(("parallel",)),
        name="merge",
    )(yd, yl, yn, p, p, p, wb, wo, x, n1.reshape(1, d), mod, n2.reshape(1, d), mod, mod)


def _swiglu_step(x_ref, wg_ref, wu_ref, wd_ref, acc_ref, f):
    @pl.when(f == 0)
    def _():
        acc_ref[...] = jnp.zeros_like(acc_ref)
    xb = x_ref[...].astype(BF16)
    g = _dot(xb, wg_ref[...])
    u = _dot(xb, wu_ref[...])
    a = (g * jax.nn.sigmoid(g) * u).astype(BF16)
    acc_ref[...] += _dot(a, wd_ref[...])


def _ffn_dense_kernel(x_ref, wg_ref, wu_ref, wd_ref, xm_ref, gate_ref, n3_ref, *rest, nf, with_next):
    if with_next:
        n0_ref, sc_ref, sh_ref, xo_ref, hn_ref, acc_ref = rest
    else:
        xo_ref, acc_ref = rest
    f = pl.program_id(1)
    _swiglu_step(x_ref, wg_ref, wu_ref, wd_ref, acc_ref, f)

    @pl.when(f == nf - 1)
    def _():
        xo = xm_ref[...] + gate_ref[...] * _rms(acc_ref[...], n3_ref[...])
        xo_ref[...] = xo
        if with_next:
            hn = _rms(xo, n0_ref[...]) * (1.0 + sc_ref[...]) + sh_ref[...]
            hn_ref[...] = hn.astype(hn_ref.dtype)


def _ffn_dense(h2, wg, wu, wd, xm, n3, mod, row_fn, tm, tf, nxt):
    n, d = h2.shape
    fp = wg.shape[1]
    nf = fp // tf
    with_next = nxt is not None
    in_specs = [pl.BlockSpec((tm, d), lambda i, f: (i, 0)),
                pl.BlockSpec((d, tf), lambda i, f: (0, f)),
                pl.BlockSpec((d, tf), lambda i, f: (0, f)),
                pl.BlockSpec((tf, d), lambda i, f: (f, 0)),
                pl.BlockSpec((tm, d), lambda i, f: (i, 0)),
                _mod_spec(d, M_G2, row_fn),
                _vec_spec(d)]
    args = [h2, wg, wu, wd, xm, mod, n3.reshape(1, d)]
    out_specs = [pl.BlockSpec((tm, d), lambda i, f: (i, 0))]
    out_shape = [jax.ShapeDtypeStruct((n, d), F32)]
    if with_next:
        n0, mod_next = nxt
        in_specs += [_vec_spec(d), _mod_spec(d, M_SC1, row_fn), _mod_spec(d, M_SH1, row_fn)]
        args += [n0.reshape(1, d), mod_next, mod_next]
        out_specs.append(pl.BlockSpec((tm, d), lambda i, f: (i, 0)))
        out_shape.append(jax.ShapeDtypeStruct((n, d), BF16))
    kern = functools.partial(_ffn_dense_kernel, nf=nf, with_next=with_next)
    res = pl.pallas_call(
        kern,
        grid=(n // tm, nf),
        in_specs=in_specs,
        out_specs=out_specs,
        out_shape=out_shape,
        scratch_shapes=[pltpu.VMEM((tm, d), F32)],
        compiler_params=_params(("parallel", "arbitrary")),
        name="ffn_dense",
    )(*args)
    return (res[0], res[1]) if with_next else (res[0], None)


def _row_copy(src_hbm, src_row, dst_ref, dst_row, sem):
    return pltpu.make_async_copy(src_hbm.at[pl.ds(src_row, 1)], dst_ref.at[pl.ds(dst_row, 1)], sem)


def _ffn_grouped_kernel(te_ref, nt_ref, src_ref, x_hbm, wg_ref, wu_ref, wd_ref, o_ref,
                        rows_ref, xb_ref, acc_ref, sem, *, nf, tm):
    i = pl.program_id(0)
    f = pl.program_id(1)
    nt = nt_ref[0]
    slot = i % 2

    def gather(tile, slot_, start):
        base = tile * tm

        def body(r, c):
            cp = _row_copy(x_hbm, src_ref[base + r] if start else 0, rows_ref.at[slot_], r,
                           sem.at[slot_])
            cp.start() if start else cp.wait()
            return c
        lax.fori_loop(0, tm, body, 0)

    @pl.when((i == 0) & (f == 0))
    def _():
        gather(0, 0, True)

    @pl.when((i < nt) & (f == 0))
    def _():
        gather(i, slot, False)
        xb_ref[...] = rows_ref[slot].astype(BF16)

    @pl.when((i + 1 < nt) & (f == 0))
    def _():
        gather(i + 1, 1 - slot, True)

    @pl.when(i < nt)
    def _():
        _swiglu_step(xb_ref, wg_ref, wu_ref, wd_ref, acc_ref, f)

        @pl.when(f == nf - 1)
        def _():
            o_ref[...] = acc_ref[...]

    @pl.when((i >= nt) & (f == nf - 1))
    def _():
        o_ref[...] = jnp.zeros_like(o_ref)


def _ffn_grouped(h2, src, wg, wu, wd, tile_expert, n_tiles, p_max, tm, tf):
    d = h2.shape[1]
    fe = wg.shape[2]
    nf = fe // tf

    def fidx(i, f, nt):
        return jnp.where(i < nt[0], f, nf - 1)

    kern = functools.partial(_ffn_grouped_kernel, nf=nf, tm=tm)
    return pl.pallas_call(
        kern,
        grid_spec=pltpu.PrefetchScalarGridSpec(
            num_scalar_prefetch=3,
            grid=(p_max // tm, nf),
            in_specs=[pl.BlockSpec(memory_space=pl.ANY),
                      pl.BlockSpec((None, d, tf), lambda i, f, te, nt, src: (te[i], 0, fidx(i, f, nt))),
                      pl.BlockSpec((None, d, tf), lambda i, f, te, nt, src: (te[i], 0, fidx(i, f, nt))),
                      pl.BlockSpec((None, tf, d), lambda i, f, te, nt, src: (te[i], fidx(i, f, nt), 0))],
            out_specs=pl.BlockSpec((tm, d), lambda i, f, te, nt, src: (i, 0)),
            scratch_shapes=[pltpu.VMEM((2, tm, d), h2.dtype),
                            pltpu.VMEM((tm, d), BF16),
                            pltpu.VMEM((tm, d), F32),
                            pltpu.SemaphoreType.DMA((2,))]),
        out_shape=jax.ShapeDtypeStruct((p_max, d), F32),
        compiler_params=pltpu.CompilerParams(dimension_semantics=("arbitrary", "arbitrary"),
                                             vmem_limit_bytes=VMEM_LIMIT,
                                             disable_bounds_checks=True),
        name="ffn_grouped",
    )(tile_expert, n_tiles, src, h2, wg, wu, wd)


def _router_kernel(h_ref, wr_ref, o_ref, cnt_ref, carry_ref, *, n_experts):
    i = pl.program_id(0)

    @pl.when(i == 0)
    def _():
        carry_ref[...] = jnp.zeros_like(carry_ref)

    logits = _dot(h_ref[...].astype(BF16), wr_ref[...])
    tm = logits.shape[0]
    lane_i = lax.broadcasted_iota(I32, logits.shape, 1)
    lane = lane_i.astype(F32)
    logits = jnp.where(lane_i < n_experts, logits, -jnp.inf)
    m1 = jnp.max(logits, axis=-1, keepdims=True)
    i1 = jnp.min(jnp.where(logits == m1, lane, float(LANES)), axis=-1, keepdims=True)
    rest = jnp.where(lane == i1, -jnp.inf, logits)
    m2 = jnp.max(rest, axis=-1, keepdims=True)
    i2 = jnp.min(jnp.where(rest == m2, lane, float(LANES)), axis=-1, keepdims=True)
    e21 = jnp.exp(m2 - m1)
    w1 = 1.0 / (1.0 + e21)
    w2 = e21 * w1

    sel1 = lane == i1
    sel2 = lane == i2
    onehot = jnp.where(sel1 | sel2, 1.0, 0.0).astype(BF16)
    r = lax.broadcasted_iota(I32, (tm, tm), 0)
    c = lax.broadcasted_iota(I32, (tm, tm), 1)
    strict_lower = jnp.where(c < r, 1.0, 0.0).astype(BF16)
    before = _dot(strict_lower, onehot) + carry_ref[0:1, :]
    r1 = jnp.sum(jnp.where(sel1, before, 0.0), axis=-1, keepdims=True)
    r2 = jnp.sum(jnp.where(sel2, before, 0.0), axis=-1, keepdims=True)
    total = carry_ref[0:1, :] + jnp.sum(onehot.astype(F32), axis=0, keepdims=True)
    carry_ref[...] = jnp.broadcast_to(total, carry_ref.shape)
    cnt_ref[...] = jnp.broadcast_to(total, cnt_ref.shape)

    packed = jnp.where(lane_i == 0, i1, 0.0)
    packed = jnp.where(lane_i == 1, i2, packed)
    packed = jnp.where(lane_i == 2, r1, packed)
    packed = jnp.where(lane_i == 3, r2, packed)
    packed = jnp.where(lane_i == 4, w1, packed)
    packed = jnp.where(lane_i == 5, w2, packed)
    o_ref[...] = packed


def _router(h2, w_router, tm):
    n, d = h2.shape
    n_experts = w_router.shape[1]
    wr = jnp.pad(w_router, ((0, 0), (0, LANES - n_experts))).astype(BF16)
    kern = functools.partial(_router_kernel, n_experts=n_experts)
    return pl.pallas_call(
        kern,
        grid=(n // tm,),
        in_specs=[pl.BlockSpec((tm, d), lambda i: (i, 0)),
                  pl.BlockSpec((d, LANES), lambda i: (0, 0))],
        out_specs=[pl.BlockSpec((tm, LANES), lambda i: (i, 0)),
                   pl.BlockSpec((SUBLANES, LANES), lambda i: (0, 0))],
        out_shape=[jax.ShapeDtypeStruct((n, LANES), F32),
                   jax.ShapeDtypeStruct((SUBLANES, LANES), F32)],
        scratch_shapes=[pltpu.VMEM((SUBLANES, LANES), F32)],
        compiler_params=_params(("arbitrary",)),
        name="router",
    )(h2, wr)


def _combine_kernel(d1_ref, d2_ref, y_hbm, r_ref, xm_ref, gate_ref, n3_ref, xo_ref, buf_ref, sem, *, tm):
    base = pl.program_id(0) * tm

    def issue(r, c):
        _row_copy(y_hbm, d1_ref[base + r], buf_ref.at[0], r, sem).start()
        _row_copy(y_hbm, d2_ref[base + r], buf_ref.at[1], r, sem).start()
        return c
    lax.fori_loop(0, tm, issue, 0)

    def drain(r, c):
        _row_copy(y_hbm, 0, buf_ref.at[0], r, sem).wait()
        _row_copy(y_hbm, 0, buf_ref.at[1], r, sem).wait()
        return c
    lax.fori_loop(0, tm, drain, 0)

    route = r_ref[...]
    w1 = route[:, 4:5]
    w2 = route[:, 5:6]
    y = w1 * buf_ref[0] + w2 * buf_ref[1]
    xo_ref[...] = xm_ref[...] + gate_ref[...] * _rms(y, n3_ref[...])


def _combine(yo, dest1, dest2, route, xm, n3, mod, row_fn, tm):
    n, d = xm.shape
    kern = functools.partial(_combine_kernel, tm=tm)
    return pl.pallas_call(
        kern,
        grid_spec=pltpu.PrefetchScalarGridSpec(
            num_scalar_prefetch=2,
            grid=(n // tm,),
            in_specs=[pl.BlockSpec(memory_space=pl.ANY),
                      pl.BlockSpec((tm, LANES), lambda i, *_: (i, 0)),
                      pl.BlockSpec((tm, d), lambda i, *_: (i, 0)),
                      _mod_spec(d, M_G2, row_fn),
                      _vec_spec(d)],
            out_specs=pl.BlockSpec((tm, d), lambda i, *_: (i, 0)),
            scratch_shapes=[pltpu.VMEM((2, tm, d), F32), pltpu.SemaphoreType.DMA(())]),
        out_shape=jax.ShapeDtypeStruct((n, d), F32),
        compiler_params=pltpu.CompilerParams(dimension_semantics=("arbitrary",),
                                             vmem_limit_bytes=VMEM_LIMIT,
                                             disable_bounds_checks=True),
        name="moe_combine",
    )(dest1, dest2, yo, route, xm, mod, n3.reshape(1, d))


def _moe(h2, xm, w_router, wg, wu, wd, n3, mod, row_fn, tm_g, tf):
    n, d = h2.shape
    n_experts = w_router.shape[1]
    route, counts = _router(h2, w_router, min(512, n))
    e1 = route[:, 0].astype(I32)
    e2 = route[:, 1].astype(I32)
    r1 = route[:, 2].astype(I32)
    r2 = route[:, 3].astype(I32)
    cnt = counts[0, :n_experts].astype(I32)
    padded = ((cnt + tm_g - 1) // tm_g) * tm_g
    ends = jnp.cumsum(padded)
    offs = ends - padded
    dest1 = offs[e1] + r1
    dest2 = offs[e2] + r2
    p_max = ((TOP_K * n + n_experts * (tm_g - 1)) // tm_g) * tm_g
    n_rows = ends[-1:]
    tok = jnp.arange(n, dtype=I32)
    src = jnp.zeros((p_max,), I32).at[dest1].set(tok).at[dest2].set(tok)
    tile_start = jnp.arange(p_max // tm_g, dtype=I32) * tm_g
    tile_expert = jnp.minimum(jnp.sum(tile_start[:, None] >= ends[None, :], axis=1),
                              n_experts - 1).astype(I32)
    last_expert = tile_expert[jnp.maximum(n_rows[0] // tm_g - 1, 0)]
    tile_expert = jnp.where(tile_start < n_rows[0], tile_expert, last_expert)
    yo = _ffn_grouped(h2, src, wg, wu, wd, tile_expert, n_rows // tm_g, p_max, tm_g, tf)
    return _combine(yo, dest1, dest2, route, xm, n3, mod, row_fn, min(256, n))


def _rope_tables(seq):
    inv = ROPE_BASE ** (-jnp.arange(ROPE_FREQ, dtype=F32) / ROPE_FREQ)
    t = jnp.arange(seq, dtype=I32)
    pos = jnp.stack([t // GRID_W, t % GRID_W], axis=-1).astype(F32)
    ang = pos[:, :, None] * inv
    cos, sin = jnp.cos(ang), jnp.sin(ang)
    cos = jnp.concatenate([cos, cos], axis=-1).reshape(seq, DIFF_HEAD_DIM)
    sin = jnp.concatenate([-sin, sin], axis=-1).reshape(seq, DIFF_HEAD_DIM)
    reps = LANES // DIFF_HEAD_DIM
    return jnp.tile(cos, (1, reps)), jnp.tile(sin, (1, reps))


def _pad_cols(w, mult):
    pad = (-w.shape[-1]) % mult
    return jnp.pad(w, [(0, 0)] * (w.ndim - 1) + [(0, pad)]) if pad else w


def _pad_rows(w, mult):
    pad = (-w.shape[-2]) % mult
    return jnp.pad(w, [(0, 0)] * (w.ndim - 2) + [(0, pad), (0, 0)]) if pad else w


def kernel(x, c, ctx, c_ctx, norm_g, w_ada, b_ada, w_in, diff_lambda, diff_subln_g, lru_conv_w, lru_conv_b, lru_gate_w, lru_gate_b, lru_lambda, na_rpb, w_branch, w_merge, b_merge, w_out, ffn_w_gate, ffn_w_up, ffn_w_down, moe_w_router, moe_w_gate, moe_w_up, moe_w_down):
    bsz, seq, d = x.shape
    clen = ctx.shape[1]
    depth = w_in.shape[0]
    n, nc = bsz * seq, bsz * clen
    x_lat = x.reshape(n, d)
    x_ctx = ctx.reshape(nc, d)

    mr = -(-(bsz + 1) // SUBLANES) * SUBLANES
    cvec = jnp.concatenate([c, c_ctx[None], jnp.zeros((mr - bsz - 1, d), F32)], axis=0)
    mod_all = _ada(cvec, w_ada, b_ada).reshape(depth, mr, 1, 6 * d)
    cos_t, sin_t = _rope_tables(seq)

    tm_lat, tm_ctx = min(1024, seq), min(1024, nc, seq)
    tm_mix_lat, tm_mix_ctx = min(256, seq), min(256, nc)
    tm_ffn_lat, tm_ffn_ctx = min(512, seq), min(512, nc)
    tf = min(512, d)
    tq = min(512, seq)

    def lat_row(tile_rows):
        per_batch = seq // tile_rows
        return lambda i: i // per_batch

    def ctx_row(i):
        return bsz

    h_lat = _prenorm(x_lat, norm_g[0, 0], mod_all[0], lat_row(tm_lat), tm_lat)
    h_ctx = _prenorm(x_ctx, norm_g[0, 0], mod_all[0], ctx_row, tm_ctx)

    for l in range(depth):
        need_ctx = l < depth - 1
        mod = mod_all[l]
        lam_init = 0.8 - 0.6 * math.exp(-0.3 * l)
        wcat = jnp.concatenate([w_in[l], w_merge[l]], axis=1).astype(BF16)
        bcat = jnp.concatenate([jnp.zeros((IN_WIDTH,), F32), b_merge[l]])[None]
        p_lat = _proj(h_lat, wcat, bcat, cos_t, sin_t, seq, True, tm_lat)
        p_ctx = _proj(h_ctx, wcat, bcat, cos_t, sin_t, seq, False, tm_ctx)

        y_diff = _diff_attn_lat(p_lat, p_ctx, diff_lambda[l], diff_subln_g[l], lam_init,
                                bsz, seq, clen, tq)
        y_lru, y_lru_c = _lru(p_lat, p_ctx, lru_conv_w[l], lru_conv_b[l], lru_gate_w[l],
                              lru_gate_b[l], lru_lambda[l], bsz, seq, clen, need_ctx)
        y_na = _na_lat(p_lat, p_ctx, na_rpb[l], bsz, seq, clen)

        wb = w_branch[l].astype(BF16)
        wo = w_out[l].astype(BF16)
        is_moe = l % 2 == 1
        h2_dtype = F32 if is_moe else BF16
        xm_lat, h2_lat = _merge(y_diff, y_lru, y_na, p_lat, wb, wo, x_lat, norm_g[l, 1],
                                norm_g[l, 2], mod, lat_row(tm_mix_lat), tm_mix_lat, h2_dtype)
        if need_ctx:
            y_diff_c = _diff_attn_ctx(p_ctx, diff_lambda[l], diff_subln_g[l], lam_init, bsz, clen)
            y_na_c = _na_ctx(p_ctx, bsz, clen)
            xm_ctx, h2_ctx = _merge(y_diff_c, y_lru_c, y_na_c, p_ctx, wb, wo, x_ctx, norm_g[l, 1],
                                    norm_g[l, 2], mod, ctx_row, tm_mix_ctx, h2_dtype)

        nxt = (norm_g[l + 1, 0], mod_all[l + 1]) if need_ctx else None
        i = l // 2
        if is_moe:
            wg = moe_w_gate[i].astype(BF16)
            wu = moe_w_up[i].astype(BF16)
            wd = moe_w_down[i].astype(BF16)
            tm_g = min(512, n)
            x_lat = _moe(h2_lat, xm_lat, moe_w_router[i], wg, wu, wd, norm_g[l, 3], mod,
                         lat_row(min(256, n)), tm_g, tf)
            if need_ctx:
                x_ctx = _moe(h2_ctx, xm_ctx, moe_w_router[i], wg, wu, wd, norm_g[l, 3], mod,
                             ctx_row, min(512, nc), tf)
                h_lat = _prenorm(x_lat, nxt[0], nxt[1], lat_row(tm_lat), tm_lat)
                h_ctx = _prenorm(x_ctx, nxt[0], nxt[1], ctx_row, tm_ctx)
        else:
            wg = _pad_cols(ffn_w_gate[i], tf).astype(BF16)
            wu = _pad_cols(ffn_w_up[i], tf).astype(BF16)
            wd = _pad_rows(ffn_w_down[i], tf).astype(BF16)
            x_lat, h_lat = _ffn_dense(h2_lat, wg, wu, wd, xm_lat, norm_g[l, 3], mod,
                                      lat_row(tm_ffn_lat), tm_ffn_lat, tf, nxt)
            if need_ctx:
                x_ctx, h_ctx = _ffn_dense(h2_ctx, wg, wu, wd, xm_ctx, norm_g[l, 3], mod,
                                          ctx_row, tm_ffn_ctx, tf, nxt)
    return x_lat.reshape(bsz, seq, d)
```

```python
import functools
import math

import numpy as np
import jax
import jax.numpy as jnp
from jax import lax
from jax.experimental import pallas as pl
from jax.experimental.pallas import tpu as pltpu

F32 = jnp.float32
BF16 = jnp.bfloat16
I32 = jnp.int32

EPS = 1e-6
GRID_W = 64
BRANCH_WIDTH = 1024
N_IN_PARTS = 8
IN_WIDTH = N_IN_PARTS * BRANCH_WIDTH
DIFF_HEADS = 8
DIFF_HEAD_DIM = 64
DIFF_V_DIM = 2 * DIFF_HEAD_DIM
ROPE_BASE = 10000.0
ROPE_FREQ = DIFF_HEAD_DIM // 4
LRU_BLOCKS = 8
LRU_BLOCK_DIM = BRANCH_WIDTH // LRU_BLOCKS
LRU_CONV_W = 4
LRU_C = 8.0
NA_HEADS = 8
NA_HEAD_DIM = BRANCH_WIDTH // NA_HEADS
NA_WIN_R = 8
NA_WIN_C = 16
TOP_K = 2
MOE_TILE = 560

LANES = 128
SUBLANES = 8
VMEM_LIMIT = 56 * 1024 * 1024
NEG = -1e30
LOG2E = 1.0 / math.log(2.0)
DIFF_QSCALE = DIFF_HEAD_DIM ** -0.5 * LOG2E
NA_QSCALE = NA_HEAD_DIM ** -0.5 * LOG2E

P_DQ, P_DK, P_DV, P_LX, P_LG, P_NQ, P_NK, P_NV = range(8)
M_SH1, M_SC1, M_G1, M_SH2, M_SC2, M_G2 = range(6)


def _params(sem):
    return pltpu.CompilerParams(dimension_semantics=sem, vmem_limit_bytes=VMEM_LIMIT)


def _rms(x, g):
    return x * lax.rsqrt(jnp.mean(x * x, axis=-1, keepdims=True) + EPS) * g


def _dot(a, b):
    return jnp.dot(a, b, preferred_element_type=F32)


def _dot_nt(a, b):
    return lax.dot_general(a, b, (((1,), (1,)), ((), ())), preferred_element_type=F32)


def _ada_kernel(c_ref, w_ref, b_ref, o_ref):
    c = c_ref[...]
    s = (c * jax.nn.sigmoid(c)).astype(BF16)
    o_ref[...] = _dot(s, w_ref[...].astype(BF16)) + b_ref[...]


def _ada(cvec, w_ada, b_ada):
    n_layers, d, n6 = w_ada.shape
    mr = cvec.shape[0]
    tn = min(1024, d)
    return pl.pallas_call(
        _ada_kernel,
        grid=(n_layers, n6 // tn),
        in_specs=[pl.BlockSpec((mr, d), lambda l, j: (0, 0)),
                  pl.BlockSpec((None, d, tn), lambda l, j: (l, 0, j)),
                  pl.BlockSpec((None, 1, tn), lambda l, j: (l, 0, j))],
        out_specs=pl.BlockSpec((None, mr, tn), lambda l, j: (l, 0, j)),
        out_shape=jax.ShapeDtypeStruct((n_layers, mr, n6), F32),
        compiler_params=_params(("arbitrary", "arbitrary")),
        name="ada_mod",
    )(cvec, w_ada, b_ada.reshape(n_layers, 1, n6))


def _mod_spec(d, chunk, row_fn):
    return pl.BlockSpec((None, 1, d), lambda i, *_: (row_fn(i), 0, chunk))


def _vec_spec(d):
    return pl.BlockSpec((1, d), lambda i, *_: (0, 0))


def _prenorm_kernel(x_ref, g_ref, sc_ref, sh_ref, o_ref):
    y = _rms(x_ref[...], g_ref[...])
    o_ref[...] = (y * (1.0 + sc_ref[...]) + sh_ref[...]).astype(o_ref.dtype)


def _prenorm(x, g, mod, row_fn, tm):
    n, d = x.shape
    return pl.pallas_call(
        _prenorm_kernel,
        grid=(n // tm,),
        in_specs=[pl.BlockSpec((tm, d), lambda i: (i, 0)),
                  _vec_spec(d),
                  _mod_spec(d, M_SC1, row_fn),
                  _mod_spec(d, M_SH1, row_fn)],
        out_specs=pl.BlockSpec((tm, d), lambda i: (i, 0)),
        out_shape=jax.ShapeDtypeStruct((n, d), BF16),
        compiler_params=_params(("parallel",)),
        name="prenorm",
    )(x, g.reshape(1, d), mod, mod)


def _proj_kernel(h_ref, w_ref, b_ref, cos_ref, sin_ref, o_ref, *, rope, tiles_per_part):
    part = pl.program_id(1) // tiles_per_part
    acc = _dot(h_ref[...], w_ref[...])
    tn = acc.shape[1]

    def rotated():
        lane = lax.broadcasted_iota(I32, acc.shape, 1)
        first_half = (lane % (2 * ROPE_FREQ)) < ROPE_FREQ
        partner = jnp.where(first_half, pltpu.roll(acc, tn - ROPE_FREQ, 1),
                            pltpu.roll(acc, ROPE_FREQ, 1))
        cos = jnp.tile(cos_ref[...], (1, tn // LANES))
        sin = jnp.tile(sin_ref[...], (1, tn // LANES))
        return acc * cos + partner * sin

    def emit(cond, fn):
        @pl.when(cond)
        def _():
            o_ref[...] = fn().astype(o_ref.dtype)

    emit(part == P_DQ, lambda: (rotated() if rope else acc) * DIFF_QSCALE)
    emit(part == P_DK, lambda: rotated() if rope else acc)
    emit(part == P_NQ, lambda: acc * NA_QSCALE)
    emit((part > P_DK) & (part != P_NQ) & (part < N_IN_PARTS), lambda: acc)
    emit(part >= N_IN_PARTS, lambda: jax.nn.sigmoid(acc + b_ref[...]))


def _proj(h, wcat, bcat, cos_t, sin_t, seq, rope, tm):
    n, d = h.shape
    nw = wcat.shape[1]
    tn = min(1024, d)
    tpb = seq // tm
    kern = functools.partial(_proj_kernel, rope=rope, tiles_per_part=BRANCH_WIDTH // tn)
    return pl.pallas_call(
        kern,
        grid=(n // tm, nw // tn),
        in_specs=[pl.BlockSpec((tm, d), lambda i, j: (i, 0)),
                  pl.BlockSpec((d, tn), lambda i, j: (0, j)),
                  pl.BlockSpec((1, tn), lambda i, j: (0, j)),
                  pl.BlockSpec((tm, LANES), lambda i, j: (i % tpb, 0)),
                  pl.BlockSpec((tm, LANES), lambda i, j: (i % tpb, 0))],
        out_specs=pl.BlockSpec((tm, tn), lambda i, j: (i, j)),
        out_shape=jax.ShapeDtypeStruct((n, nw), BF16),
        compiler_params=_params(("parallel", "arbitrary")),
        name="proj_rope" if rope else "proj",
    )(h, wcat, bcat, cos_t, sin_t)


def _proj_lat_kernel(h_ref, w_ref, b_ref, cos_ref, sin_ref, o_ref, *, kind, scaled_part, scale,
                     tiles_per_part):
    acc = _dot(h_ref[...], w_ref[...])
    tn = acc.shape[1]
    if kind == "gate":
        o_ref[...] = jax.nn.sigmoid(acc + b_ref[...]).astype(o_ref.dtype)
        return
    s = jnp.where(pl.program_id(1) // tiles_per_part == scaled_part, scale, 1.0).astype(F32)
    if kind == "rope":
        lane = lax.broadcasted_iota(I32, acc.shape, 1)
        first_half = (lane % (2 * ROPE_FREQ)) < ROPE_FREQ
        partner = jnp.where(first_half, pltpu.roll(acc, tn - ROPE_FREQ, 1),
                            pltpu.roll(acc, ROPE_FREQ, 1))
        cos = jnp.tile(cos_ref[...], (1, tn // LANES)) * s
        sin = jnp.tile(sin_ref[...], (1, tn // LANES)) * s
        o_ref[...] = (acc * cos + partner * sin).astype(o_ref.dtype)
    else:
        o_ref[...] = (acc * s).astype(o_ref.dtype)


def _proj_lat(h, w, b, cos_t, sin_t, seq, tm, kind, scaled_part=-1, scale=1.0):
    n, d = h.shape
    nw = w.shape[1]
    tn = min(1024, d)
    tpb = seq // tm
    kern = functools.partial(_proj_lat_kernel, kind=kind, scaled_part=scaled_part, scale=scale,
                             tiles_per_part=BRANCH_WIDTH // tn)
    return pl.pallas_call(
        kern,
        grid=(n // tm, nw // tn),
        in_specs=[pl.BlockSpec((tm, d), lambda i, j: (i, 0)),
                  pl.BlockSpec((d, tn), lambda i, j: (0, j)),
                  pl.BlockSpec((1, tn), lambda i, j: (0, j)),
                  pl.BlockSpec((tm, LANES), lambda i, j: (i % tpb, 0)),
                  pl.BlockSpec((tm, LANES), lambda i, j: (i % tpb, 0))],
        out_specs=pl.BlockSpec((tm, tn), lambda i, j: (i, j)),
        out_shape=jax.ShapeDtypeStruct((n, nw), BF16),
        compiler_params=_params(("parallel", "arbitrary")),
        name="proj_" + kind,
    )(h, w, b, cos_t, sin_t)


DIFF_KEY_CHUNK = 512


def _diff_kernel(*refs, lam_init, has_lat):
    if has_lat:
        lamv_ref, sg_ref, q_ref, k_ref, v_ref, kc_ref, vc_ref, o_ref = refs
    else:
        lamv_ref, sg_ref, q_ref, kc_ref, vc_ref, o_ref = refs
    lv = lamv_ref[...]
    lam = (jnp.exp(jnp.sum(lv[0:1] * lv[1:2], axis=-1, keepdims=True))
           - jnp.exp(jnp.sum(lv[2:3] * lv[3:4], axis=-1, keepdims=True)) + lam_init)
    q = q_ref[...]
    first = lax.broadcasted_iota(I32, q.shape, 1) < DIFF_HEAD_DIM
    zero = jnp.zeros_like(q)
    qz = (jnp.where(first, q, zero), jnp.where(first, zero, q))
    chunks = [(kc_ref, vc_ref, 0, kc_ref.shape[0])]
    if has_lat:
        kt = min(DIFF_KEY_CHUNK, k_ref.shape[0])
        chunks += [(k_ref, v_ref, j * kt, kt) for j in range(k_ref.shape[0] // kt)]

    def qk(c, chunk):
        k, _, off, n = chunk
        return _dot_nt(qz[c], k[off:off + n, :])

    def row_max(parts):
        m = jnp.max(parts[0], axis=-1, keepdims=True)
        for p in parts[1:]:
            m = jnp.maximum(m, jnp.max(p, axis=-1, keepdims=True))
        return m

    def softmax_av(s_parts, between=None):
        m = row_max(s_parts)
        den, out = 0.0, 0.0
        for j, (_, v, off, n) in enumerate(chunks):
            if between is not None:
                between(j)
            e = jnp.exp2(s_parts[j] - m)
            den = den + jnp.sum(e, axis=-1, keepdims=True)
            out = out + _dot(e.astype(BF16), v[off:off + n, :])
        return out, den

    s1 = [qk(0, ch) for ch in chunks]
    s2 = []
    o1, den1 = softmax_av(s1, between=lambda j: s2.append(qk(1, chunks[j])))
    o2, den2 = softmax_av(s2)
    o = o1 * (1.0 / den1) - o2 * (lam / den2)
    y = _rms(o, sg_ref[...]) * (1.0 - lam_init)
    o_ref[...] = y.astype(o_ref.dtype)


def _diff_attn_lat(lat, ctxp, lamv, subln_g, lam_init, bsz, seq, clen, tq):
    nq = seq // tq
    hw = DIFF_V_DIM
    hpb = BRANCH_WIDTH // hw

    def spec(rows, src, row_fn):
        blk = src[1] * hpb
        return pl.BlockSpec((rows, hw), lambda b, h, i: (row_fn(b, i), blk + h))

    kern = functools.partial(_diff_kernel, lam_init=lam_init, has_lat=True)
    return pl.pallas_call(
        kern,
        grid=(bsz, DIFF_HEADS, nq),
        in_specs=[pl.BlockSpec((4, DIFF_HEAD_DIM), lambda b, h, i: (0, 0)),
                  pl.BlockSpec((1, hw), lambda b, h, i: (0, 0)),
                  spec(tq, lat["dq"], lambda b, i: b * nq + i),
                  spec(seq, lat["dk"], lambda b, i: b),
                  spec(seq, lat["dv"], lambda b, i: b),
                  spec(clen, ctxp["dk"], lambda b, i: b),
                  spec(clen, ctxp["dv"], lambda b, i: b)],
        out_specs=pl.BlockSpec((tq, hw), lambda b, h, i: (b * nq + i, h)),
        out_shape=jax.ShapeDtypeStruct((bsz * seq, BRANCH_WIDTH), BF16),
        compiler_params=_params(("parallel", "parallel", "arbitrary")),
        name="diff_attn",
    )(lamv, subln_g.reshape(1, hw), lat["dq"][0], lat["dk"][0], lat["dv"][0],
      ctxp["dk"][0], ctxp["dv"][0])


def _diff_attn_ctx(ctxp, lamv, subln_g, lam_init, bsz, clen):
    hw = DIFF_V_DIM
    hpb = BRANCH_WIDTH // hw

    def spec(src):
        blk = src[1] * hpb
        return pl.BlockSpec((clen, hw), lambda b, h: (b, blk + h))

    kern = functools.partial(_diff_kernel, lam_init=lam_init, has_lat=False)
    return pl.pallas_call(
        kern,
        grid=(bsz, DIFF_HEADS),
        in_specs=[pl.BlockSpec((4, DIFF_HEAD_DIM), lambda b, h: (0, 0)),
                  pl.BlockSpec((1, hw), lambda b, h: (0, 0)),
                  spec(ctxp["dq"]), spec(ctxp["dk"]), spec(ctxp["dv"])],
        out_specs=pl.BlockSpec((clen, hw), lambda b, h: (b, h)),
        out_shape=jax.ShapeDtypeStruct((bsz * clen, BRANCH_WIDTH), BF16),
        compiler_params=_params(("parallel", "parallel")),
        name="diff_attn_ctx",
    )(lamv, subln_g.reshape(1, hw), ctxp["dq"][0], ctxp["dk"][0], ctxp["dv"][0])


LRU_TW = 256


def _gelu_tanh(x):
    return 0.5 * x * (1.0 + jnp.tanh(math.sqrt(2.0 / math.pi) * (x + 0.044715 * (x * x * x))))


def _tile_scan(a, b, reverse):
    rows = lax.broadcasted_iota(I32, a.shape, 0)
    for s in (1, 2, 4):
        if reverse:
            a_s = pltpu.roll(a, SUBLANES - s, 0)
            b_s = pltpu.roll(b, SUBLANES - s, 0)
            valid = rows < SUBLANES - s
        else:
            a_s = pltpu.roll(a, s, 0)
            b_s = pltpu.roll(b, s, 0)
            valid = rows >= s
        b = jnp.where(valid, a * b_s + b, b)
        a = jnp.where(valid, a * a_s, a)
    return a, b


def _lru_kernel(xl_ref, gl_ref, xc_ref, gc_ref, cw_ref, cb_ref, gw_ref, gb_ref, lam_ref,
                *out_and_scratch, seq, clen, need_ctx):
    if need_ctx:
        yl_ref, yc_ref, af_ref, bf_ref, ab_ref, bb_ref, hf_ref = out_and_scratch
    else:
        yl_ref, af_ref, bf_ref, ab_ref, bb_ref, hf_ref = out_and_scratch
        yc_ref = None
    tot = clen + seq
    x = jnp.concatenate([xc_ref[...], xl_ref[...]], axis=0).astype(F32)
    row = lax.broadcasted_iota(I32, x.shape, 0)
    pos = jnp.where(row < clen, row, row - clen)
    seg_len = jnp.where(row < clen, clen, seq)
    cw = cw_ref[...]
    conv = cb_ref[...] + x * cw[2:3]
    conv = conv + jnp.where(pos >= 2, pltpu.roll(x, 2, 0), 0.0) * cw[0:1]
    conv = conv + jnp.where(pos >= 1, pltpu.roll(x, 1, 0), 0.0) * cw[1:2]
    conv = conv + jnp.where(pos < seg_len - 1, pltpu.roll(x, tot - 1, 0), 0.0) * cw[3:4]
    xb = conv.astype(BF16)
    gb = gb_ref[...]
    lam = lam_ref[...]
    for d, (a_ref, b_ref) in enumerate(((af_ref, bf_ref), (ab_ref, bb_ref))):
        r = jax.nn.sigmoid(_dot(xb, gw_ref[d, 0]) + gb[2 * d:2 * d + 1])
        i = jax.nn.sigmoid(_dot(xb, gw_ref[d, 1]) + gb[2 * d + 1:2 * d + 2])
        z = -lam[d:d + 1]
        softplus = jnp.maximum(z, 0.0) + jnp.log1p(jnp.exp(-jnp.abs(z)))
        log_a = -LRU_C * r * softplus
        a = jnp.exp(log_a)
        beta = jnp.sqrt(-jnp.tanh(log_a) * (a * a + 1.0))
        a_ref[...] = a
        b_ref[...] = beta * i * conv

    n_ct = clen // SUBLANES
    n_t = tot // SUBLANES
    w = x.shape[1]

    def fwd_step(t, carry):
        off = pl.multiple_of(t * SUBLANES, SUBLANES)
        a, b = _tile_scan(af_ref[pl.ds(off, SUBLANES), :], bf_ref[pl.ds(off, SUBLANES), :], False)
        h = a * carry + b
        hf_ref[pl.ds(off, SUBLANES), :] = h
        return h[SUBLANES - 1:SUBLANES, :]

    lax.fori_loop(0, n_t, fwd_step, jnp.zeros((1, w), F32))

    def bwd_tile(off, carry):
        a, b = _tile_scan(ab_ref[pl.ds(off, SUBLANES), :], bb_ref[pl.ds(off, SUBLANES), :], True)
        h = a * carry + b
        return h, h[0:1, :]

    def bwd_ctx_step(t, carry):
        off = pl.multiple_of((n_ct - 1 - t) * SUBLANES, SUBLANES)
        h, carry = bwd_tile(off, carry)
        if need_ctx:
            g = gc_ref[pl.ds(off, SUBLANES), :].astype(F32)
            yc_ref[pl.ds(off, SUBLANES), :] = (
                (hf_ref[pl.ds(off, SUBLANES), :] + h) * _gelu_tanh(g)).astype(yc_ref.dtype)
        return carry

    carry = lax.fori_loop(0, n_ct, bwd_ctx_step, jnp.zeros((1, w), F32))

    def bwd_lat_step(t, carry):
        off = pl.multiple_of((n_t - 1 - t) * SUBLANES, SUBLANES)
        h, carry = bwd_tile(off, carry)
        lo = pl.multiple_of(off - clen, SUBLANES)
        g = gl_ref[pl.ds(lo, SUBLANES), :].astype(F32)
        yl_ref[pl.ds(lo, SUBLANES), :] = (
            (hf_ref[pl.ds(off, SUBLANES), :] + h) * _gelu_tanh(g)).astype(yl_ref.dtype)
        return carry

    lax.fori_loop(0, n_t - n_ct, bwd_lat_step, carry)


def _lru(lat, ctxp, conv_w, conv_b, gate_w, gate_b, lru_lam, bsz, seq, clen, need_ctx):
    n = bsz * seq
    tw = LRU_TW
    nwt = BRANCH_WIDTH // tw
    per = tw // LRU_BLOCK_DIM
    gw = gate_w.reshape(2, 2, nwt, per, LRU_BLOCK_DIM, LRU_BLOCK_DIM)
    eye = jnp.eye(per, dtype=gate_w.dtype)
    gw = jnp.einsum('dgnpcf,pq->dgnpcqf', gw, eye).reshape(2, 2, nwt, tw, tw).astype(BF16)
    gb = gate_b.reshape(4, BRANCH_WIDTH)
    tot = seq + clen
    kern = functools.partial(_lru_kernel, seq=seq, clen=clen, need_ctx=need_ctx)

    def part_spec(rows, src):
        blk = src[1] * nwt
        return pl.BlockSpec((rows, tw), lambda b, j: (b, blk + j))

    out_shape = [jax.ShapeDtypeStruct((n, BRANCH_WIDTH), BF16)]
    out_specs = [pl.BlockSpec((seq, tw), lambda b, j: (b, j))]
    if need_ctx:
        out_shape.append(jax.ShapeDtypeStruct((bsz * clen, BRANCH_WIDTH), BF16))
        out_specs.append(pl.BlockSpec((clen, tw), lambda b, j: (b, j)))
    res = pl.pallas_call(
        kern,
        grid=(bsz, nwt),
        in_specs=[part_spec(seq, lat["lx"]), part_spec(seq, lat["lg"]),
                  part_spec(clen, ctxp["lx"]), part_spec(clen, ctxp["lg"]),
                  pl.BlockSpec((LRU_CONV_W, tw), lambda b, j: (0, j)),
                  pl.BlockSpec((1, tw), lambda b, j: (0, j)),
                  pl.BlockSpec((2, 2, None, tw, tw), lambda b, j: (0, 0, j, 0, 0)),
                  pl.BlockSpec((4, tw), lambda b, j: (0, j)),
                  pl.BlockSpec((2, tw), lambda b, j: (0, j))],
        out_specs=out_specs,
        out_shape=out_shape,
        scratch_shapes=[pltpu.VMEM((tot, tw), F32) for _ in range(5)],
        compiler_params=_params(("parallel", "parallel")),
        name="rglru",
    )(lat["lx"][0], lat["lg"][0], ctxp["lx"][0], ctxp["lg"][0], conv_w,
      conv_b.reshape(1, BRANCH_WIDTH), gw, gb, lru_lam)
    return (res[0], res[1]) if need_ctx else (res[0], None)


NA_ROWS_PER_TILE = 4


def _na_tables(rows, rpt):
    kr = min(NA_WIN_R, rows)
    nkr = min(rows, rpt + kr - 1)
    variants, index, tile_variant, tile_kb = [], {}, [], []
    for t in range(rows // rpt):
        r0 = t * rpt
        kb = int(np.clip(r0 - kr // 2, 0, rows - nkr))
        geom = []
        for a in range(rpt):
            r = r0 + a
            rs = int(np.clip(r - kr // 2, 0, rows - kr))
            geom.append(tuple((kb + k - r + NA_WIN_R - 1) if rs <= kb + k < rs + kr else None
                              for k in range(nkr)))
        geom = tuple(geom)
        if geom not in index:
            index[geom] = len(variants)
            variants.append(geom)
        tile_variant.append(index[geom])
        tile_kb.append(kb)
    return nkr, np.array(tile_variant, np.int32), np.array(tile_kb, np.int32), variants


def _na_bias(rpb, variants):
    heads = rpb.shape[0]
    pad = GRID_W - NA_WIN_C
    padded = jnp.pad(rpb * LOG2E, ((0, 0), (0, 0), (pad, pad)))
    toep = jnp.stack([padded[..., GRID_W - 1 - qc:2 * GRID_W - 1 - qc] for qc in range(GRID_W)],
                     axis=2)
    col = np.arange(GRID_W)
    cs = np.clip(col - NA_WIN_C // 2, 0, GRID_W - NA_WIN_C)
    in_cols = (col[None, :] >= cs[:, None]) & (col[None, :] < cs[:, None] + NA_WIN_C)
    toep = jnp.where(in_cols, toep, NEG).astype(F32)
    outside = jnp.full((heads, GRID_W, GRID_W), NEG, F32)
    return jnp.stack([
        jnp.concatenate([
            jnp.concatenate([outside if dr is None else toep[:, dr] for dr in row], axis=-1)
            for row in geom], axis=-2)
        for geom in variants])


def _na_kernel(var_ref, kb_ref, q_ref, k_ref, v_ref, kc_ref, vc_ref, bias_ref, o_ref, *, nk):
    t = pl.program_id(1)
    start = pl.multiple_of(kb_ref[t] * GRID_W, GRID_W)
    for h in range(NA_HEADS):
        sl = slice(h * NA_HEAD_DIM, (h + 1) * NA_HEAD_DIM)
        qh = q_ref[:, sl]
        s_w = _dot_nt(qh, k_ref[pl.ds(start, nk), sl]) + bias_ref[h]
        s_c = _dot_nt(qh, kc_ref[:, sl])
        m = jnp.maximum(jnp.max(s_w, axis=-1, keepdims=True), jnp.max(s_c, axis=-1, keepdims=True))
        e_w = jnp.exp2(s_w - m)
        e_c = jnp.exp2(s_c - m)
        inv = 1.0 / (jnp.sum(e_w, axis=-1, keepdims=True) + jnp.sum(e_c, axis=-1, keepdims=True))
        o = (_dot(e_w.astype(BF16), v_ref[pl.ds(start, nk), sl])
             + _dot(e_c.astype(BF16), vc_ref[:, sl]))
        o_ref[:, sl] = (o * inv).astype(o_ref.dtype)


def _na_lat(lat, ctxp, rpb, bsz, seq, clen):
    n = bsz * seq
    rows = seq // GRID_W
    rpt = min(NA_ROWS_PER_TILE, rows)
    nkr, tile_variant, tile_kb, variants = _na_tables(rows, rpt)
    bias = _na_bias(rpb, variants)
    nt = rows // rpt
    rq, nk = rpt * GRID_W, nkr * GRID_W
    bw = BRANCH_WIDTH
    kern = functools.partial(_na_kernel, nk=nk)
    return pl.pallas_call(
        kern,
        grid_spec=pltpu.PrefetchScalarGridSpec(
            num_scalar_prefetch=2,
            grid=(bsz, nt),
            in_specs=[pl.BlockSpec((rq, bw), lambda b, t, var, kb: (b * nt + t, lat["nq"][1])),
                      pl.BlockSpec((seq, bw), lambda b, t, var, kb: (b, lat["nk"][1])),
                      pl.BlockSpec((seq, bw), lambda b, t, var, kb: (b, lat["nv"][1])),
                      pl.BlockSpec((clen, bw), lambda b, t, var, kb: (b, ctxp["nk"][1])),
                      pl.BlockSpec((clen, bw), lambda b, t, var, kb: (b, ctxp["nv"][1])),
                      pl.BlockSpec((None, NA_HEADS, rq, nk), lambda b, t, var, kb: (var[t], 0, 0, 0))],
            out_specs=pl.BlockSpec((rq, bw), lambda b, t, var, kb: (b * nt + t, 0))),
        out_shape=jax.ShapeDtypeStruct((n, bw), BF16),
        compiler_params=_params(("parallel", "arbitrary")),
        name="na_attn",
    )(jnp.asarray(tile_variant), jnp.asarray(tile_kb), lat["nq"][0], lat["nk"][0], lat["nv"][0],
      ctxp["nk"][0], ctxp["nv"][0], bias)


def _na_ctx_kernel(q_ref, k_ref, v_ref, o_ref):
    for h in range(NA_HEADS):
        sl = slice(h * NA_HEAD_DIM, (h + 1) * NA_HEAD_DIM)
        s = _dot_nt(q_ref[:, sl], k_ref[:, sl])
        e = jnp.exp2(s - jnp.max(s, axis=-1, keepdims=True))
        inv = 1.0 / jnp.sum(e, axis=-1, keepdims=True)
        o_ref[:, sl] = (_dot(e.astype(BF16), v_ref[:, sl]) * inv).astype(o_ref.dtype)


def _na_ctx(ctxp, bsz, clen):
    bw = BRANCH_WIDTH
    return pl.pallas_call(
        _na_ctx_kernel,
        grid=(bsz,),
        in_specs=[pl.BlockSpec((clen, bw), lambda b: (b, ctxp["nq"][1])),
                  pl.BlockSpec((clen, bw), lambda b: (b, ctxp["nk"][1])),
                  pl.BlockSpec((clen, bw), lambda b: (b, ctxp["nv"][1]))],
        out_specs=pl.BlockSpec((clen, bw), lambda b: (b, 0)),
        out_shape=jax.ShapeDtypeStruct((bsz * clen, bw), BF16),
        compiler_params=_params(("parallel",)),
        name="na_attn_ctx",
    )(ctxp["nq"][0], ctxp["nk"][0], ctxp["nv"][0])


def _merge_kernel(yd_ref, yl_ref, yn_ref, g0_ref, g1_ref, g2_ref, wb_ref, wo_ref, x_ref,
                  n1_ref, gate_ref, n2_ref, sc_ref, sh_ref, xm_ref, h2_ref):
    m = (g0_ref[...].astype(F32) * _dot(yd_ref[...], wb_ref[0])
         + g1_ref[...].astype(F32) * _dot(yl_ref[...], wb_ref[1])
         + g2_ref[...].astype(F32) * _dot(yn_ref[...], wb_ref[2]))
    mo = _dot(m.astype(BF16), wo_ref[...])
    xm = x_ref[...] + gate_ref[...] * _rms(mo, n1_ref[...])
    xm_ref[...] = xm
    h2 = _rms(xm, n2_ref[...]) * (1.0 + sc_ref[...]) + sh_ref[...]
    h2_ref[...] = h2.astype(h2_ref.dtype)


def _merge(yd, yl, yn, gates, wb, wo, x, n1, n2, mod, row_fn, tm, h2_dtype):
    n, d = x.shape
    bw = BRANCH_WIDTH
    p, gcol = gates
    resident = dict(pipeline_mode=pl.Buffered(1))
    return pl.pallas_call(
        _merge_kernel,
        grid=(n // tm,),
        in_specs=[pl.BlockSpec((tm, bw), lambda i: (i, 0)),
                  pl.BlockSpec((tm, bw), lambda i: (i, 0)),
                  pl.BlockSpec((tm, bw), lambda i: (i, 0)),
                  pl.BlockSpec((tm, d), lambda i: (i, gcol)),
                  pl.BlockSpec((tm, d), lambda i: (i, gcol + 1)),
                  pl.BlockSpec((tm, d), lambda i: (i, gcol + 2)),
                  pl.BlockSpec((3, bw, d), lambda i: (0, 0, 0), **resident),
                  pl.BlockSpec((d, d), lambda i: (0, 0), **resident),
                  pl.BlockSpec((tm, d), lambda i: (i, 0)),
                  _vec_spec(d),
                  _mod_spec(d, M_G1, row_fn),
                  _vec_spec(d),
                  _mod_spec(d, M_SC2, row_fn),
                  _mod_spec(d, M_SH2, row_fn)],
        out_specs=[pl.BlockSpec((tm, d), lambda i: (i, 0)),
                   pl.BlockSpec((tm, d), lambda i: (i, 0))],
        out_shape=[jax.ShapeDtypeStruct((n, d), F32),
                   jax.ShapeDtypeStruct((n, d), h2_dtype)],
        compiler_params=_params(("parallel",)),
        name="merge",
    )(yd, yl, yn, p, p, p, wb, wo, x, n1.reshape(1, d), mod, n2.reshape(1, d), mod, mod)


def _swiglu_step(x_ref, wg_ref, wu_ref, wd_ref, acc_ref, f):
    @pl.when(f == 0)
    def _():
        acc_ref[...] = jnp.zeros_like(acc_ref)
    xb = x_ref[...].astype(BF16)
    g = _dot(xb, wg_ref[...])
    u = _dot(xb, wu_ref[...])
    a = (g * jax.nn.sigmoid(g) * u).astype(BF16)
    acc_ref[...] += _dot(a, wd_ref[...])


def _ffn_dense_kernel(x_ref, wg_ref, wu_ref, wd_ref, xm_ref, gate_ref, n3_ref, *rest, nf, with_next):
    if with_next:
        n0_ref, sc_ref, sh_ref, xo_ref, hn_ref, acc_ref = rest
    else:
        xo_ref, acc_ref = rest
    f = pl.program_id(1)
    _swiglu_step(x_ref, wg_ref, wu_ref, wd_ref, acc_ref, f)

    @pl.when(f == nf - 1)
    def _():
        xo = xm_ref[...] + gate_ref[...] * _rms(acc_ref[...], n3_ref[...])
        xo_ref[...] = xo
        if with_next:
            hn = _rms(xo, n0_ref[...]) * (1.0 + sc_ref[...]) + sh_ref[...]
            hn_ref[...] = hn.astype(hn_ref.dtype)


def _ffn_dense(h2, wg, wu, wd, xm, n3, mod, row_fn, tm, tf, nxt):
    n, d = h2.shape
    fp = wg.shape[1]
    nf = fp // tf
    with_next = nxt is not None
    in_specs = [pl.BlockSpec((tm, d), lambda i, f: (i, 0)),
                pl.BlockSpec((d, tf), lambda i, f: (0, f)),
                pl.BlockSpec((d, tf), lambda i, f: (0, f)),
                pl.BlockSpec((tf, d), lambda i, f: (f, 0)),
                pl.BlockSpec((tm, d), lambda i, f: (i, 0)),
                _mod_spec(d, M_G2, row_fn),
                _vec_spec(d)]
    args = [h2, wg, wu, wd, xm, mod, n3.reshape(1, d)]
    out_specs = [pl.BlockSpec((tm, d), lambda i, f: (i, 0))]
    out_shape = [jax.ShapeDtypeStruct((n, d), F32)]
    if with_next:
        n0, mod_next = nxt
        in_specs += [_vec_spec(d), _mod_spec(d, M_SC1, row_fn), _mod_spec(d, M_SH1, row_fn)]
        args += [n0.reshape(1, d), mod_next, mod_next]
        out_specs.append(pl.BlockSpec((tm, d), lambda i, f: (i, 0)))
        out_shape.append(jax.ShapeDtypeStruct((n, d), BF16))
    kern = functools.partial(_ffn_dense_kernel, nf=nf, with_next=with_next)
    res = pl.pallas_call(
        kern,
        grid=(n // tm, nf),
        in_specs=in_specs,
        out_specs=out_specs,
        out_shape=out_shape,
        scratch_shapes=[pltpu.VMEM((tm, d), F32)],
        compiler_params=_params(("parallel", "arbitrary")),
        name="ffn_dense",
    )(*args)
    return (res[0], res[1]) if with_next else (res[0], None)


def _row_copy(src_hbm, src_row, dst_ref, dst_row, sem):
    return pltpu.make_async_copy(src_hbm.at[pl.ds(src_row, 1)], dst_ref.at[pl.ds(dst_row, 1)], sem)


def _ffn_grouped_kernel(te_ref, nt_ref, src_ref, x_hbm, wg_ref, wu_ref, wd_ref, o_ref,
                        rows_ref, xb_ref, acc_ref, sem, *, nf, tm, grid_tiles):
    i = pl.program_id(0)
    f = pl.program_id(1)
    nt = nt_ref[0]
    slot = i % 2
    rows_per_step = tm // nf

    def start_row(tile, slot_, r):
        _row_copy(x_hbm, src_ref[tile * tm + r], rows_ref.at[slot_], r, sem.at[slot_]).start()

    def wait_rows(slot_):
        pltpu.make_async_copy(x_hbm.at[pl.ds(0, tm)], rows_ref.at[slot_], sem.at[slot_]).wait()

    @pl.when((i == 0) & (f == 0))
    def _():
        def body(r, c):
            start_row(0, 0, r)
            return c
        lax.fori_loop(0, tm, body, 0)

    @pl.when((i < nt) & (f == 0))
    def _():
        wait_rows(slot)
        xb_ref[...] = rows_ref[slot].astype(BF16)
        acc_ref[...] = jnp.zeros_like(acc_ref)

    @pl.when(i < nt)
    def _():
        nxt = jnp.minimum(i + 1, grid_tiles - 1)
        for j in range(rows_per_step):
            start_row(nxt, 1 - slot, f * rows_per_step + j)
        xb = xb_ref[...]
        g = _dot(xb, wg_ref[...])
        u = _dot(xb, wu_ref[...])
        a = (g * jax.nn.sigmoid(g) * u).astype(BF16)
        acc_ref[...] += _dot(a, wd_ref[...])

    @pl.when((i < nt) & (f == nf - 1))
    def _():
        o_ref[...] = acc_ref[...]

    @pl.when((i == nt - 1) & (f == nf - 1))
    def _():
        wait_rows(1 - slot)

    @pl.when((i >= nt) & (f == nf - 1))
    def _():
        o_ref[...] = jnp.zeros_like(o_ref)


def _ffn_grouped(h2, src, wg, wu, wd, tile_expert, n_tiles, p_max, tm, tf):
    d = h2.shape[1]
    fe = wg.shape[2]
    nf = fe // tf
    assert tm % nf == 0, (tm, nf)

    def fidx(i, f, nt):
        return jnp.where(i < nt[0], f, nf - 1)

    kern = functools.partial(_ffn_grouped_kernel, nf=nf, tm=tm, grid_tiles=p_max // tm)
    return pl.pallas_call(
        kern,
        grid_spec=pltpu.PrefetchScalarGridSpec(
            num_scalar_prefetch=3,
            grid=(p_max // tm, nf),
            in_specs=[pl.BlockSpec(memory_space=pl.ANY),
                      pl.BlockSpec((None, d, tf), lambda i, f, te, nt, src: (te[i], 0, fidx(i, f, nt))),
                      pl.BlockSpec((None, d, tf), lambda i, f, te, nt, src: (te[i], 0, fidx(i, f, nt))),
                      pl.BlockSpec((None, tf, d), lambda i, f, te, nt, src: (te[i], fidx(i, f, nt), 0))],
            out_specs=pl.BlockSpec((tm, d), lambda i, f, te, nt, src: (i, 0)),
            scratch_shapes=[pltpu.VMEM((2, tm, d), h2.dtype),
                            pltpu.VMEM((tm, d), BF16),
                            pltpu.VMEM((tm, d), F32),
                            pltpu.SemaphoreType.DMA((2,))]),
        out_shape=jax.ShapeDtypeStruct((p_max, d), F32),
        compiler_params=pltpu.CompilerParams(dimension_semantics=("arbitrary", "arbitrary"),
                                             vmem_limit_bytes=VMEM_LIMIT,
                                             disable_bounds_checks=True),
        name="ffn_grouped",
    )(tile_expert, n_tiles, src, h2, wg, wu, wd)


def _router_kernel(h_ref, wr_ref, o_ref, cnt_ref, carry_ref, *, n_experts):
    i = pl.program_id(0)

    @pl.when(i == 0)
    def _():
        carry_ref[...] = jnp.zeros_like(carry_ref)

    logits = _dot(h_ref[...].astype(BF16), wr_ref[...])
    tm = logits.shape[0]
    lane_i = lax.broadcasted_iota(I32, logits.shape, 1)
    lane = lane_i.astype(F32)
    logits = jnp.where(lane_i < n_experts, logits, -jnp.inf)
    m1 = jnp.max(logits, axis=-1, keepdims=True)
    i1 = jnp.min(jnp.where(logits == m1, lane, float(LANES)), axis=-1, keepdims=True)
    rest = jnp.where(lane == i1, -jnp.inf, logits)
    m2 = jnp.max(rest, axis=-1, keepdims=True)
    i2 = jnp.min(jnp.where(rest == m2, lane, float(LANES)), axis=-1, keepdims=True)
    e21 = jnp.exp(m2 - m1)
    w1 = 1.0 / (1.0 + e21)
    w2 = e21 * w1

    sel1 = lane == i1
    sel2 = lane == i2
    onehot = jnp.where(sel1 | sel2, 1.0, 0.0).astype(BF16)
    r = lax.broadcasted_iota(I32, (tm, tm), 0)
    c = lax.broadcasted_iota(I32, (tm, tm), 1)
    strict_lower = jnp.where(c < r, 1.0, 0.0).astype(BF16)
    before = _dot(strict_lower, onehot) + carry_ref[0:1, :]
    r1 = jnp.sum(jnp.where(sel1, before, 0.0), axis=-1, keepdims=True)
    r2 = jnp.sum(jnp.where(sel2, before, 0.0), axis=-1, keepdims=True)
    total = carry_ref[0:1, :] + jnp.sum(onehot.astype(F32), axis=0, keepdims=True)
    carry_ref[...] = jnp.broadcast_to(total, carry_ref.shape)
    cnt_ref[...] = jnp.broadcast_to(total, cnt_ref.shape)

    packed = jnp.where(lane_i == 0, i1, 0.0)
    packed = jnp.where(lane_i == 1, i2, packed)
    packed = jnp.where(lane_i == 2, r1, packed)
    packed = jnp.where(lane_i == 3, r2, packed)
    packed = jnp.where(lane_i == 4, w1, packed)
    packed = jnp.where(lane_i == 5, w2, packed)
    o_ref[...] = packed


def _router(h2, w_router, tm):
    n, d = h2.shape
    n_experts = w_router.shape[1]
    wr = jnp.pad(w_router, ((0, 0), (0, LANES - n_experts))).astype(BF16)
    kern = functools.partial(_router_kernel, n_experts=n_experts)
    return pl.pallas_call(
        kern,
        grid=(n // tm,),
        in_specs=[pl.BlockSpec((tm, d), lambda i: (i, 0)),
                  pl.BlockSpec((d, LANES), lambda i: (0, 0))],
        out_specs=[pl.BlockSpec((tm, LANES), lambda i: (i, 0)),
                   pl.BlockSpec((SUBLANES, LANES), lambda i: (0, 0))],
        out_shape=[jax.ShapeDtypeStruct((n, LANES), F32),
                   jax.ShapeDtypeStruct((SUBLANES, LANES), F32)],
        scratch_shapes=[pltpu.VMEM((SUBLANES, LANES), F32)],
        compiler_params=_params(("arbitrary",)),
        name="router",
    )(h2, wr)


def _combine_kernel(d1_ref, d2_ref, y_hbm, r_ref, xm_ref, gate_ref, n3_ref, xo_ref, buf_ref, sem, *, tm):
    base = pl.program_id(0) * tm

    def issue(r, c):
        _row_copy(y_hbm, d1_ref[base + r], buf_ref.at[0], r, sem).start()
        _row_copy(y_hbm, d2_ref[base + r], buf_ref.at[1], r, sem).start()
        return c
    lax.fori_loop(0, tm, issue, 0)

    for k in range(TOP_K):
        pltpu.make_async_copy(y_hbm.at[pl.ds(0, tm)], buf_ref.at[k], sem).wait()

    route = r_ref[...]
    w1 = route[:, 4:5]
    w2 = route[:, 5:6]
    y = w1 * buf_ref[0] + w2 * buf_ref[1]
    xo_ref[...] = xm_ref[...] + gate_ref[...] * _rms(y, n3_ref[...])


def _combine(yo, dest1, dest2, route, xm, n3, mod, row_fn, tm):
    n, d = xm.shape
    kern = functools.partial(_combine_kernel, tm=tm)
    return pl.pallas_call(
        kern,
        grid_spec=pltpu.PrefetchScalarGridSpec(
            num_scalar_prefetch=2,
            grid=(n // tm,),
            in_specs=[pl.BlockSpec(memory_space=pl.ANY),
                      pl.BlockSpec((tm, LANES), lambda i, *_: (i, 0)),
                      pl.BlockSpec((tm, d), lambda i, *_: (i, 0)),
                      _mod_spec(d, M_G2, row_fn),
                      _vec_spec(d)],
            out_specs=pl.BlockSpec((tm, d), lambda i, *_: (i, 0)),
            scratch_shapes=[pltpu.VMEM((2, tm, d), F32), pltpu.SemaphoreType.DMA(())]),
        out_shape=jax.ShapeDtypeStruct((n, d), F32),
        compiler_params=pltpu.CompilerParams(dimension_semantics=("arbitrary",),
                                             vmem_limit_bytes=VMEM_LIMIT,
                                             disable_bounds_checks=True),
        name="moe_combine",
    )(dest1, dest2, yo, route, xm, mod, n3.reshape(1, d))


def _moe(h2, xm, w_router, wg, wu, wd, n3, mod, row_fn, tm_g, tf):
    n, d = h2.shape
    n_experts = w_router.shape[1]
    route, counts = _router(h2, w_router, min(512, n))
    e1 = route[:, 0].astype(I32)
    e2 = route[:, 1].astype(I32)
    r1 = route[:, 2].astype(I32)
    r2 = route[:, 3].astype(I32)
    cnt = counts[0, :n_experts].astype(I32)
    padded = ((cnt + tm_g - 1) // tm_g) * tm_g
    ends = jnp.cumsum(padded)
    offs = ends - padded
    dest1 = offs[e1] + r1
    dest2 = offs[e2] + r2
    p_max = ((TOP_K * n + n_experts * (tm_g - 1)) // tm_g) * tm_g
    n_rows = ends[-1:]
    tok = jnp.arange(n, dtype=I32)
    src = jnp.zeros((p_max,), I32).at[dest1].set(tok).at[dest2].set(tok)
    tile_start = jnp.arange(p_max // tm_g, dtype=I32) * tm_g
    tile_expert = jnp.minimum(jnp.sum(tile_start[:, None] >= ends[None, :], axis=1),
                              n_experts - 1).astype(I32)
    last_expert = tile_expert[jnp.maximum(n_rows[0] // tm_g - 1, 0)]
    tile_expert = jnp.where(tile_start < n_rows[0], tile_expert, last_expert)
    yo = _ffn_grouped(h2, src, wg, wu, wd, tile_expert, n_rows // tm_g, p_max, tm_g, tf)
    return _combine(yo, dest1, dest2, route, xm, n3, mod, row_fn, min(256, n))


def _rope_tables(seq):
    inv = ROPE_BASE ** (-jnp.arange(ROPE_FREQ, dtype=F32) / ROPE_FREQ)
    t = jnp.arange(seq, dtype=I32)
    pos = jnp.stack([t // GRID_W, t % GRID_W], axis=-1).astype(F32)
    ang = pos[:, :, None] * inv
    cos, sin = jnp.cos(ang), jnp.sin(ang)
    cos = jnp.concatenate([cos, cos], axis=-1).reshape(seq, DIFF_HEAD_DIM)
    sin = jnp.concatenate([-sin, sin], axis=-1).reshape(seq, DIFF_HEAD_DIM)
    reps = LANES // DIFF_HEAD_DIM
    return jnp.tile(cos, (1, reps)), jnp.tile(sin, (1, reps))


def _pad_cols(w, mult):
    pad = (-w.shape[-1]) % mult
    return jnp.pad(w, [(0, 0)] * (w.ndim - 1) + [(0, pad)]) if pad else w


def _pad_rows(w, mult):
    pad = (-w.shape[-2]) % mult
    return jnp.pad(w, [(0, 0)] * (w.ndim - 2) + [(0, pad), (0, 0)]) if pad else w


def kernel(x, c, ctx, c_ctx, norm_g, w_ada, b_ada, w_in, diff_lambda, diff_subln_g, lru_conv_w, lru_conv_b, lru_gate_w, lru_gate_b, lru_lambda, na_rpb, w_branch, w_merge, b_merge, w_out, ffn_w_gate, ffn_w_up, ffn_w_down, moe_w_router, moe_w_gate, moe_w_up, moe_w_down):
    bsz, seq, d = x.shape
    clen = ctx.shape[1]
    depth = w_in.shape[0]
    n, nc = bsz * seq, bsz * clen
    x_lat = x.reshape(n, d)
    x_ctx = ctx.reshape(nc, d)

    mr = -(-(bsz + 1) // SUBLANES) * SUBLANES
    cvec = jnp.concatenate([c, c_ctx[None], jnp.zeros((mr - bsz - 1, d), F32)], axis=0)
    mod_all = _ada(cvec, w_ada, b_ada).reshape(depth, mr, 1, 6 * d)
    cos_t, sin_t = _rope_tables(seq)

    tm_lat, tm_ctx = min(1024, seq), min(1024, nc, seq)
    tm_mix_lat, tm_mix_ctx = min(256, seq), min(256, nc)
    tm_ffn_lat, tm_ffn_ctx = min(512, seq), min(512, nc)
    tf = min(512, d)
    tq = min(512, seq)

    def lat_row(tile_rows):
        per_batch = seq // tile_rows
        return lambda i: i // per_batch

    def ctx_row(i):
        return bsz

    h_lat = _prenorm(x_lat, norm_g[0, 0], mod_all[0], lat_row(tm_lat), tm_lat)
    h_ctx = _prenorm(x_ctx, norm_g[0, 0], mod_all[0], ctx_row, tm_ctx)

    part_names = ("dq", "dk", "dv", "lx", "lg", "nq", "nk", "nv")
    for l in range(depth):
        need_ctx = l < depth - 1
        mod = mod_all[l]
        lam_init = 0.8 - 0.6 * math.exp(-0.3 * l)
        w_in_b = w_in[l].astype(BF16)
        w_merge_b = w_merge[l].astype(BF16)
        b_gate = b_merge[l][None]
        no_bias = jnp.zeros((1, IN_WIDTH), F32)
        qk_lat = _proj_lat(h_lat, w_in_b[:, :2 * BRANCH_WIDTH], no_bias, cos_t, sin_t, seq, tm_lat,
                           "rope", 0, DIFF_QSCALE)
        rest_lat = _proj_lat(h_lat, w_in_b[:, 2 * BRANCH_WIDTH:], no_bias, cos_t, sin_t, seq, tm_lat,
                             "plain", P_NQ - 2, NA_QSCALE)
        gate_lat = _proj_lat(h_lat, w_merge_b, b_gate, cos_t, sin_t, seq, tm_lat, "gate")
        lat = {"dq": (qk_lat, 0), "dk": (qk_lat, 1)}
        lat.update({name: (rest_lat, k) for k, name in enumerate(part_names[2:])})
        wcat = jnp.concatenate([w_in_b, w_merge_b], axis=1)
        bcat = jnp.concatenate([no_bias, b_gate], axis=1)
        p_ctx = _proj(h_ctx, wcat, bcat, cos_t, sin_t, seq, False, tm_ctx)
        ctxp = {name: (p_ctx, k) for k, name in enumerate(part_names)}

        y_diff = _diff_attn_lat(lat, ctxp, diff_lambda[l], diff_subln_g[l], lam_init,
                                bsz, seq, clen, tq)
        y_lru, y_lru_c = _lru(lat, ctxp, lru_conv_w[l], lru_conv_b[l], lru_gate_w[l],
                              lru_gate_b[l], lru_lambda[l], bsz, seq, clen, need_ctx)
        y_na = _na_lat(lat, ctxp, na_rpb[l], bsz, seq, clen)

        wb = w_branch[l].astype(BF16)
        wo = w_out[l].astype(BF16)
        is_moe = l % 2 == 1
        h2_dtype = F32 if is_moe else BF16
        xm_lat, h2_lat = _merge(y_diff, y_lru, y_na, (gate_lat, 0), wb, wo, x_lat, norm_g[l, 1],
                                norm_g[l, 2], mod, lat_row(tm_mix_lat), tm_mix_lat, h2_dtype)
        if need_ctx:
            y_diff_c = _diff_attn_ctx(ctxp, diff_lambda[l], diff_subln_g[l], lam_init, bsz, clen)
            y_na_c = _na_ctx(ctxp, bsz, clen)
            xm_ctx, h2_ctx = _merge(y_diff_c, y_lru_c, y_na_c, (p_ctx, IN_WIDTH // d), wb, wo, x_ctx,
                                    norm_g[l, 1], norm_g[l, 2], mod, ctx_row, tm_mix_ctx, h2_dtype)

        nxt = (norm_g[l + 1, 0], mod_all[l + 1]) if need_ctx else None
        i = l // 2
        if is_moe:
            wg = moe_w_gate[i].astype(BF16)
            wu = moe_w_up[i].astype(BF16)
            wd = moe_w_down[i].astype(BF16)
            tm_g = min(MOE_TILE, n)
            x_lat = _moe(h2_lat, xm_lat, moe_w_router[i], wg, wu, wd, norm_g[l, 3], mod,
                         lat_row(min(256, n)), tm_g, tf)
            if need_ctx:
                x_ctx = _moe(h2_ctx, xm_ctx, moe_w_router[i], wg, wu, wd, norm_g[l, 3], mod,
                             ctx_row, min(MOE_TILE, nc), tf)
                h_lat = _prenorm(x_lat, nxt[0], nxt[1], lat_row(tm_lat), tm_lat)
                h_ctx = _prenorm(x_ctx, nxt[0], nxt[1], ctx_row, tm_ctx)
        else:
            wg = _pad_cols(ffn_w_gate[i], tf).astype(BF16)
            wu = _pad_cols(ffn_w_up[i], tf).astype(BF16)
            wd = _pad_rows(ffn_w_down[i], tf).astype(BF16)
            x_lat, h_lat = _ffn_dense(h2_lat, wg, wu, wd, xm_lat, norm_g[l, 3], mod,
                                      lat_row(tm_ffn_lat), tm_ffn_lat, tf, nxt)
            if need_ctx:
                x_ctx, h_ctx = _ffn_dense(h2_ctx, wg, wu, wd, xm_ctx, norm_g[l, 3], mod,
                                          ctx_row, tm_ffn_ctx, tf, nxt)
    return x_lat.reshape(bsz, seq, d)
```

```python
import functools
import math

import numpy as np
import jax
import jax.numpy as jnp
from jax import lax
from jax.experimental import pallas as pl
from jax.experimental.pallas import tpu as pltpu

F32 = jnp.float32
BF16 = jnp.bfloat16
I32 = jnp.int32

EPS = 1e-6
GRID_W = 64
BRANCH_WIDTH = 1024
N_IN_PARTS = 8
IN_WIDTH = N_IN_PARTS * BRANCH_WIDTH
DIFF_HEADS = 8
DIFF_HEAD_DIM = 64
DIFF_V_DIM = 2 * DIFF_HEAD_DIM
ROPE_BASE = 10000.0
ROPE_FREQ = DIFF_HEAD_DIM // 4
LRU_BLOCKS = 8
LRU_BLOCK_DIM = BRANCH_WIDTH // LRU_BLOCKS
LRU_CONV_W = 4
LRU_C = 8.0
NA_HEADS = 8
NA_HEAD_DIM = BRANCH_WIDTH // NA_HEADS
NA_WIN_R = 8
NA_WIN_C = 16
TOP_K = 2
MOE_TILE = 560

LANES = 128
SUBLANES = 8
VMEM_LIMIT = 56 * 1024 * 1024
NEG = -1e30
LOG2E = 1.0 / math.log(2.0)
DIFF_QSCALE = DIFF_HEAD_DIM ** -0.5 * LOG2E
NA_QSCALE = NA_HEAD_DIM ** -0.5 * LOG2E

P_DQ, P_DK, P_DV, P_LX, P_LG, P_NQ, P_NK, P_NV = range(8)
M_SH1, M_SC1, M_G1, M_SH2, M_SC2, M_G2 = range(6)


def _params(sem):
    return pltpu.CompilerParams(dimension_semantics=sem, vmem_limit_bytes=VMEM_LIMIT)


def _rms(x, g):
    return x * lax.rsqrt(jnp.mean(x * x, axis=-1, keepdims=True) + EPS) * g


def _dot(a, b):
    return jnp.dot(a, b, preferred_element_type=F32)


def _dot_nt(a, b):
    return lax.dot_general(a, b, (((1,), (1,)), ((), ())), preferred_element_type=F32)


def _ada_kernel(c_ref, w_ref, b_ref, o_ref):
    c = c_ref[...]
    s = (c * jax.nn.sigmoid(c)).astype(BF16)
    o_ref[...] = _dot(s, w_ref[...].astype(BF16)) + b_ref[...]


def _ada(cvec, w_ada, b_ada):
    n_layers, d, n6 = w_ada.shape
    mr = cvec.shape[0]
    tn = min(1024, d)
    return pl.pallas_call(
        _ada_kernel,
        grid=(n_layers, n6 // tn),
        in_specs=[pl.BlockSpec((mr, d), lambda l, j: (0, 0)),
                  pl.BlockSpec((None, d, tn), lambda l, j: (l, 0, j)),
                  pl.BlockSpec((None, 1, tn), lambda l, j: (l, 0, j))],
        out_specs=pl.BlockSpec((None, mr, tn), lambda l, j: (l, 0, j)),
        out_shape=jax.ShapeDtypeStruct((n_layers, mr, n6), F32),
        compiler_params=_params(("arbitrary", "arbitrary")),
        name="ada_mod",
    )(cvec, w_ada, b_ada.reshape(n_layers, 1, n6))


def _mod_spec(d, chunk, row_fn):
    return pl.BlockSpec((None, 1, d), lambda i, *_: (row_fn(i), 0, chunk))


def _vec_spec(d):
    return pl.BlockSpec((1, d), lambda i, *_: (0, 0))


def _prenorm_kernel(x_ref, g_ref, sc_ref, sh_ref, o_ref):
    y = _rms(x_ref[...], g_ref[...])
    o_ref[...] = (y * (1.0 + sc_ref[...]) + sh_ref[...]).astype(o_ref.dtype)


def _prenorm(x, g, mod, row_fn, tm):
    n, d = x.shape
    return pl.pallas_call(
        _prenorm_kernel,
        grid=(n // tm,),
        in_specs=[pl.BlockSpec((tm, d), lambda i: (i, 0)),
                  _vec_spec(d),
                  _mod_spec(d, M_SC1, row_fn),
                  _mod_spec(d, M_SH1, row_fn)],
        out_specs=pl.BlockSpec((tm, d), lambda i: (i, 0)),
        out_shape=jax.ShapeDtypeStruct((n, d), BF16),
        compiler_params=_params(("parallel",)),
        name="prenorm",
    )(x, g.reshape(1, d), mod, mod)


def _proj_kernel(h_ref, w_ref, b_ref, cos_ref, sin_ref, o_ref, *, kind, scaled_part, scale,
                 tiles_per_part):
    acc = _dot(h_ref[...], w_ref[...])
    tn = acc.shape[1]
    if kind == "gate":
        o_ref[...] = jax.nn.sigmoid(acc + b_ref[...]).astype(o_ref.dtype)
        return
    s = jnp.where(pl.program_id(1) // tiles_per_part == scaled_part, scale, 1.0).astype(F32)
    if kind == "rope":
        lane = lax.broadcasted_iota(I32, acc.shape, 1)
        first_half = (lane % (2 * ROPE_FREQ)) < ROPE_FREQ
        partner = jnp.where(first_half, pltpu.roll(acc, tn - ROPE_FREQ, 1),
                            pltpu.roll(acc, ROPE_FREQ, 1))
        cos = jnp.tile(cos_ref[...], (1, tn // LANES)) * s
        sin = jnp.tile(sin_ref[...], (1, tn // LANES)) * s
        o_ref[...] = (acc * cos + partner * sin).astype(o_ref.dtype)
    else:
        o_ref[...] = (acc * s).astype(o_ref.dtype)


def _proj(h, w, b, cos_t, sin_t, seq, tm, kind, scaled_part=-1, scale=1.0):
    n, d = h.shape
    nw = w.shape[1]
    tn = min(1024, d)
    tpb = seq // tm
    kern = functools.partial(_proj_kernel, kind=kind, scaled_part=scaled_part, scale=scale,
                             tiles_per_part=BRANCH_WIDTH // tn)
    return pl.pallas_call(
        kern,
        grid=(n // tm, nw // tn),
        in_specs=[pl.BlockSpec((tm, d), lambda i, j: (i, 0)),
                  pl.BlockSpec((d, tn), lambda i, j: (0, j)),
                  pl.BlockSpec((1, tn), lambda i, j: (0, j)),
                  pl.BlockSpec((tm, LANES), lambda i, j: (i % tpb, 0)),
                  pl.BlockSpec((tm, LANES), lambda i, j: (i % tpb, 0))],
        out_specs=pl.BlockSpec((tm, tn), lambda i, j: (i, j)),
        out_shape=jax.ShapeDtypeStruct((n, nw), BF16),
        compiler_params=_params(("parallel", "arbitrary")),
        name="proj_" + kind,
    )(h, w, b, cos_t, sin_t)


DIFF_KEY_CHUNK = 512


def _diff_kernel(*refs, lam_init, has_lat, n_cast):
    n_in = 7 if has_lat else 5
    for src_ref, dst_ref in zip(refs[n_in:n_in + n_cast], refs[n_in + n_cast + 1:]):
        dst_ref[...] = src_ref[...].astype(dst_ref.dtype)
    if has_lat:
        lamv_ref, sg_ref, q_ref, k_ref, v_ref, kc_ref, vc_ref = refs[:n_in]
    else:
        lamv_ref, sg_ref, q_ref, kc_ref, vc_ref = refs[:n_in]
    o_ref = refs[n_in + n_cast]
    lv = lamv_ref[...]
    lam = (jnp.exp(jnp.sum(lv[0:1] * lv[1:2], axis=-1, keepdims=True))
           - jnp.exp(jnp.sum(lv[2:3] * lv[3:4], axis=-1, keepdims=True)) + lam_init)
    q = q_ref[...]
    first = lax.broadcasted_iota(I32, q.shape, 1) < DIFF_HEAD_DIM
    zero = jnp.zeros_like(q)
    qz = (jnp.where(first, q, zero), jnp.where(first, zero, q))
    chunks = [(kc_ref, vc_ref, 0, kc_ref.shape[0])]
    if has_lat:
        kt = min(DIFF_KEY_CHUNK, k_ref.shape[0])
        chunks += [(k_ref, v_ref, j * kt, kt) for j in range(k_ref.shape[0] // kt)]

    def qk(c, chunk):
        k, _, off, n = chunk
        return _dot_nt(qz[c], k[off:off + n, :])

    def row_max(parts):
        m = jnp.max(parts[0], axis=-1, keepdims=True)
        for p in parts[1:]:
            m = jnp.maximum(m, jnp.max(p, axis=-1, keepdims=True))
        return m

    def softmax_av(s_parts, between=None):
        m = row_max(s_parts)
        den, out = 0.0, 0.0
        for j, (_, v, off, n) in enumerate(chunks):
            if between is not None:
                between(j)
            e = jnp.exp2(s_parts[j] - m)
            den = den + jnp.sum(e, axis=-1, keepdims=True)
            out = out + _dot(e.astype(BF16), v[off:off + n, :])
        return out, den

    s1 = [qk(0, ch) for ch in chunks]
    s2 = []
    o1, den1 = softmax_av(s1, between=lambda j: s2.append(qk(1, chunks[j])))
    o2, den2 = softmax_av(s2)
    o = o1 * (1.0 / den1) - o2 * (lam / den2)
    y = _rms(o, sg_ref[...]) * (1.0 - lam_init)
    o_ref[...] = y.astype(o_ref.dtype)


def _diff_attn_lat(lat, ctxp, lamv, subln_g, lam_init, bsz, seq, clen, tq, cast_jobs=()):
    nq = seq // tq
    n_steps = bsz * DIFF_HEADS * nq
    hw = DIFF_V_DIM
    hpb = BRANCH_WIDTH // hw

    def spec(rows, src, row_fn):
        blk = src[1] * hpb
        return pl.BlockSpec((rows, hw), lambda b, h, i: (row_fn(b, i), blk + h))

    cast_in, cast_out, cast_shapes = [], [], []
    for arr, row0, rows in cast_jobs:
        per_step = rows // n_steps
        assert per_step * n_steps == rows and per_step % 16 == 0 and row0 % per_step == 0
        first = row0 // per_step
        cols = arr.shape[1]

        def step(b, h, i):
            return (b * DIFF_HEADS + h) * nq + i

        cast_in.append(pl.BlockSpec((per_step, cols), lambda b, h, i, first=first: (first + step(b, h, i), 0)))
        cast_out.append(pl.BlockSpec((per_step, cols), lambda b, h, i: (step(b, h, i), 0)))
        cast_shapes.append(jax.ShapeDtypeStruct((rows, cols), BF16))
    kern = functools.partial(_diff_kernel, lam_init=lam_init, has_lat=True, n_cast=len(cast_jobs))
    res = pl.pallas_call(
        kern,
        grid=(bsz, DIFF_HEADS, nq),
        in_specs=[pl.BlockSpec((4, DIFF_HEAD_DIM), lambda b, h, i: (0, 0)),
                  pl.BlockSpec((1, hw), lambda b, h, i: (0, 0)),
                  spec(tq, lat["dq"], lambda b, i: b * nq + i),
                  spec(seq, lat["dk"], lambda b, i: b),
                  spec(seq, lat["dv"], lambda b, i: b),
                  spec(clen, ctxp["dk"], lambda b, i: b),
                  spec(clen, ctxp["dv"], lambda b, i: b)] + cast_in,
        out_specs=[pl.BlockSpec((tq, hw), lambda b, h, i: (b * nq + i, h))] + cast_out,
        out_shape=[jax.ShapeDtypeStruct((bsz * seq, BRANCH_WIDTH), BF16)] + cast_shapes,
        compiler_params=_params(("parallel", "parallel", "arbitrary")),
        name="diff_attn",
    )(lamv, subln_g.reshape(1, hw), lat["dq"][0], lat["dk"][0], lat["dv"][0],
      ctxp["dk"][0], ctxp["dv"][0], *[job[0] for job in cast_jobs])
    return res[0], res[1:]


def _diff_attn_ctx(ctxp, lamv, subln_g, lam_init, bsz, clen):
    hw = DIFF_V_DIM
    hpb = BRANCH_WIDTH // hw

    def spec(src):
        blk = src[1] * hpb
        return pl.BlockSpec((clen, hw), lambda b, h: (b, blk + h))

    kern = functools.partial(_diff_kernel, lam_init=lam_init, has_lat=False, n_cast=0)
    return pl.pallas_call(
        kern,
        grid=(bsz, DIFF_HEADS),
        in_specs=[pl.BlockSpec((4, DIFF_HEAD_DIM), lambda b, h: (0, 0)),
                  pl.BlockSpec((1, hw), lambda b, h: (0, 0)),
                  spec(ctxp["dq"]), spec(ctxp["dk"]), spec(ctxp["dv"])],
        out_specs=pl.BlockSpec((clen, hw), lambda b, h: (b, h)),
        out_shape=jax.ShapeDtypeStruct((bsz * clen, BRANCH_WIDTH), BF16),
        compiler_params=_params(("parallel", "parallel")),
        name="diff_attn_ctx",
    )(lamv, subln_g.reshape(1, hw), ctxp["dq"][0], ctxp["dk"][0], ctxp["dv"][0])


LRU_TW = 256
LRU_UNROLL = 4


def _gelu_tanh(x):
    return 0.5 * x * (1.0 + jnp.tanh(math.sqrt(2.0 / math.pi) * (x + 0.044715 * (x * x * x))))


def _tile_scan(a, b, reverse):
    rows = lax.broadcasted_iota(I32, a.shape, 0)
    for s in (1, 2, 4):
        if reverse:
            a_s = pltpu.roll(a, SUBLANES - s, 0)
            b_s = pltpu.roll(b, SUBLANES - s, 0)
            valid = rows < SUBLANES - s
        else:
            a_s = pltpu.roll(a, s, 0)
            b_s = pltpu.roll(b, s, 0)
            valid = rows >= s
        b = jnp.where(valid, a * b_s + b, b)
        a = jnp.where(valid, a * a_s, a)
    return a, b


def _lru_kernel(xl_ref, gl_ref, xc_ref, gc_ref, cw_ref, cb_ref, gw_ref, gb_ref, lam_ref,
                *out_and_scratch, seq, clen, need_ctx):
    if need_ctx:
        yl_ref, yc_ref, af_ref, bf_ref, ab_ref, bb_ref, hf_ref = out_and_scratch
    else:
        yl_ref, af_ref, bf_ref, ab_ref, bb_ref, hf_ref = out_and_scratch
        yc_ref = None
    tot = clen + seq
    x = jnp.concatenate([xc_ref[...], xl_ref[...]], axis=0).astype(F32)
    row = lax.broadcasted_iota(I32, x.shape, 0)
    pos = jnp.where(row < clen, row, row - clen)
    seg_len = jnp.where(row < clen, clen, seq)
    cw = cw_ref[...]
    conv = cb_ref[...] + x * cw[2:3]
    conv = conv + jnp.where(pos >= 2, pltpu.roll(x, 2, 0), 0.0) * cw[0:1]
    conv = conv + jnp.where(pos >= 1, pltpu.roll(x, 1, 0), 0.0) * cw[1:2]
    conv = conv + jnp.where(pos < seg_len - 1, pltpu.roll(x, tot - 1, 0), 0.0) * cw[3:4]
    xb = conv.astype(BF16)
    gb = gb_ref[...]
    lam = lam_ref[...]
    for d, (a_ref, b_ref) in enumerate(((af_ref, bf_ref), (ab_ref, bb_ref))):
        r = jax.nn.sigmoid(_dot(xb, gw_ref[d, 0]) + gb[2 * d:2 * d + 1])
        i = jax.nn.sigmoid(_dot(xb, gw_ref[d, 1]) + gb[2 * d + 1:2 * d + 2])
        z = -lam[d:d + 1]
        softplus = jnp.maximum(z, 0.0) + jnp.log1p(jnp.exp(-jnp.abs(z)))
        log_a = -LRU_C * r * softplus
        a = jnp.exp(log_a)
        beta = jnp.sqrt(-jnp.tanh(log_a) * (a * a + 1.0))
        a_ref[...] = a
        b_ref[...] = beta * i * conv

    n_ct = clen // SUBLANES
    n_t = tot // SUBLANES
    w = x.shape[1]
    assert n_ct % LRU_UNROLL == 0 and n_t % LRU_UNROLL == 0
    rows_per_iter = LRU_UNROLL * SUBLANES

    def fwd_step(t, carry):
        base = pl.multiple_of(t * rows_per_iter, rows_per_iter)
        scans = [_tile_scan(af_ref[pl.ds(base + k * SUBLANES, SUBLANES), :],
                            bf_ref[pl.ds(base + k * SUBLANES, SUBLANES), :], False)
                 for k in range(LRU_UNROLL)]
        for k, (a, b) in enumerate(scans):
            h = a * carry + b
            hf_ref[pl.ds(base + k * SUBLANES, SUBLANES), :] = h
            carry = h[SUBLANES - 1:SUBLANES, :]
        return carry

    lax.fori_loop(0, n_t // LRU_UNROLL, fwd_step, jnp.zeros((1, w), F32))

    def bwd_iter(first_tile, n_tiles, g_ref, y_ref, seg_off):
        def step(t, carry):
            base = pl.multiple_of((first_tile + n_tiles) * SUBLANES - (t + 1) * rows_per_iter,
                                  rows_per_iter)
            offs = [base + k * SUBLANES for k in reversed(range(LRU_UNROLL))]
            scans = [_tile_scan(ab_ref[pl.ds(off, SUBLANES), :], bb_ref[pl.ds(off, SUBLANES), :], True)
                     for off in offs]
            for off, (a, b) in zip(offs, scans):
                h = a * carry + b
                carry = h[0:1, :]
                if y_ref is not None:
                    g = g_ref[pl.ds(off - seg_off, SUBLANES), :].astype(F32)
                    y_ref[pl.ds(off - seg_off, SUBLANES), :] = (
                        (hf_ref[pl.ds(off, SUBLANES), :] + h) * _gelu_tanh(g)).astype(y_ref.dtype)
            return carry
        return step

    carry = lax.fori_loop(0, n_ct // LRU_UNROLL, bwd_iter(0, n_ct, gc_ref, yc_ref, 0),
                          jnp.zeros((1, w), F32))
    lax.fori_loop(0, (n_t - n_ct) // LRU_UNROLL, bwd_iter(n_ct, n_t - n_ct, gl_ref, yl_ref, clen), carry)


def _lru(lat, ctxp, conv_w, conv_b, gate_w, gate_b, lru_lam, bsz, seq, clen, need_ctx):
    n = bsz * seq
    tw = LRU_TW
    nwt = BRANCH_WIDTH // tw
    per = tw // LRU_BLOCK_DIM
    gw = gate_w.reshape(2, 2, nwt, per, LRU_BLOCK_DIM, LRU_BLOCK_DIM)
    eye = jnp.eye(per, dtype=gate_w.dtype)
    gw = jnp.einsum('dgnpcf,pq->dgnpcqf', gw, eye).reshape(2, 2, nwt, tw, tw).astype(BF16)
    gb = gate_b.reshape(4, BRANCH_WIDTH)
    tot = seq + clen
    kern = functools.partial(_lru_kernel, seq=seq, clen=clen, need_ctx=need_ctx)

    def part_spec(rows, src):
        blk = src[1] * nwt
        return pl.BlockSpec((rows, tw), lambda b, j: (b, blk + j))

    out_shape = [jax.ShapeDtypeStruct((n, BRANCH_WIDTH), BF16)]
    out_specs = [pl.BlockSpec((seq, tw), lambda b, j: (b, j))]
    if need_ctx:
        out_shape.append(jax.ShapeDtypeStruct((bsz * clen, BRANCH_WIDTH), BF16))
        out_specs.append(pl.BlockSpec((clen, tw), lambda b, j: (b, j)))
    res = pl.pallas_call(
        kern,
        grid=(bsz, nwt),
        in_specs=[part_spec(seq, lat["lx"]), part_spec(seq, lat["lg"]),
                  part_spec(clen, ctxp["lx"]), part_spec(clen, ctxp["lg"]),
                  pl.BlockSpec((LRU_CONV_W, tw), lambda b, j: (0, j)),
                  pl.BlockSpec((1, tw), lambda b, j: (0, j)),
                  pl.BlockSpec((2, 2, None, tw, tw), lambda b, j: (0, 0, j, 0, 0)),
                  pl.BlockSpec((4, tw), lambda b, j: (0, j)),
                  pl.BlockSpec((2, tw), lambda b, j: (0, j))],
        out_specs=out_specs,
        out_shape=out_shape,
        scratch_shapes=[pltpu.VMEM((tot, tw), F32) for _ in range(5)],
        compiler_params=_params(("parallel", "parallel")),
        name="rglru",
    )(lat["lx"][0], lat["lg"][0], ctxp["lx"][0], ctxp["lg"][0], conv_w,
      conv_b.reshape(1, BRANCH_WIDTH), gw, gb, lru_lam)
    return (res[0], res[1]) if need_ctx else (res[0], None)


NA_ROWS_PER_TILE = 4


def _na_tables(rows, rpt):
    kr = min(NA_WIN_R, rows)
    nkr = min(rows, rpt + kr - 1)
    variants, index, tile_variant, tile_kb = [], {}, [], []
    for t in range(rows // rpt):
        r0 = t * rpt
        kb = int(np.clip(r0 - kr // 2, 0, rows - nkr))
        geom = []
        for a in range(rpt):
            r = r0 + a
            rs = int(np.clip(r - kr // 2, 0, rows - kr))
            geom.append(tuple((kb + k - r + NA_WIN_R - 1) if rs <= kb + k < rs + kr else None
                              for k in range(nkr)))
        geom = tuple(geom)
        if geom not in index:
            index[geom] = len(variants)
            variants.append(geom)
        tile_variant.append(index[geom])
        tile_kb.append(kb)
    return nkr, np.array(tile_variant, np.int32), np.array(tile_kb, np.int32), variants


def _na_bias(rpb, variants):
    heads = rpb.shape[0]
    pad = GRID_W - NA_WIN_C
    padded = jnp.pad(rpb * LOG2E, ((0, 0), (0, 0), (pad, pad)))
    toep = jnp.stack([padded[..., GRID_W - 1 - qc:2 * GRID_W - 1 - qc] for qc in range(GRID_W)],
                     axis=2)
    col = np.arange(GRID_W)
    cs = np.clip(col - NA_WIN_C // 2, 0, GRID_W - NA_WIN_C)
    in_cols = (col[None, :] >= cs[:, None]) & (col[None, :] < cs[:, None] + NA_WIN_C)
    toep = jnp.where(in_cols, toep, NEG).astype(F32)
    outside = jnp.full((heads, GRID_W, GRID_W), NEG, F32)
    return jnp.stack([
        jnp.concatenate([
            jnp.concatenate([outside if dr is None else toep[:, dr] for dr in row], axis=-1)
            for row in geom], axis=-2)
        for geom in variants])


def _na_kernel(var_ref, kb_ref, q_ref, k_ref, v_ref, kc_ref, vc_ref, bias_ref, o_ref, *, nk):
    t = pl.program_id(1)
    start = pl.multiple_of(kb_ref[t] * GRID_W, GRID_W)
    for h in range(NA_HEADS):
        sl = slice(h * NA_HEAD_DIM, (h + 1) * NA_HEAD_DIM)
        qh = q_ref[:, sl]
        s_w = _dot_nt(qh, k_ref[pl.ds(start, nk), sl]) + bias_ref[h]
        s_c = _dot_nt(qh, kc_ref[:, sl])
        m = jnp.maximum(jnp.max(s_w, axis=-1, keepdims=True), jnp.max(s_c, axis=-1, keepdims=True))
        e_w = jnp.exp2(s_w - m)
        e_c = jnp.exp2(s_c - m)
        inv = 1.0 / (jnp.sum(e_w, axis=-1, keepdims=True) + jnp.sum(e_c, axis=-1, keepdims=True))
        o = (_dot(e_w.astype(BF16), v_ref[pl.ds(start, nk), sl])
             + _dot(e_c.astype(BF16), vc_ref[:, sl]))
        o_ref[:, sl] = (o * inv).astype(o_ref.dtype)


def _na_lat(lat, ctxp, rpb, bsz, seq, clen):
    n = bsz * seq
    rows = seq // GRID_W
    rpt = min(NA_ROWS_PER_TILE, rows)
    nkr, tile_variant, tile_kb, variants = _na_tables(rows, rpt)
    bias = _na_bias(rpb, variants)
    nt = rows // rpt
    rq, nk = rpt * GRID_W, nkr * GRID_W
    bw = BRANCH_WIDTH
    kern = functools.partial(_na_kernel, nk=nk)
    return pl.pallas_call(
        kern,
        grid_spec=pltpu.PrefetchScalarGridSpec(
            num_scalar_prefetch=2,
            grid=(bsz, nt),
            in_specs=[pl.BlockSpec((rq, bw), lambda b, t, var, kb: (b * nt + t, lat["nq"][1])),
                      pl.BlockSpec((seq, bw), lambda b, t, var, kb: (b, lat["nk"][1])),
                      pl.BlockSpec((seq, bw), lambda b, t, var, kb: (b, lat["nv"][1])),
                      pl.BlockSpec((clen, bw), lambda b, t, var, kb: (b, ctxp["nk"][1])),
                      pl.BlockSpec((clen, bw), lambda b, t, var, kb: (b, ctxp["nv"][1])),
                      pl.BlockSpec((None, NA_HEADS, rq, nk), lambda b, t, var, kb: (var[t], 0, 0, 0))],
            out_specs=pl.BlockSpec((rq, bw), lambda b, t, var, kb: (b * nt + t, 0))),
        out_shape=jax.ShapeDtypeStruct((n, bw), BF16),
        compiler_params=_params(("parallel", "arbitrary")),
        name="na_attn",
    )(jnp.asarray(tile_variant), jnp.asarray(tile_kb), lat["nq"][0], lat["nk"][0], lat["nv"][0],
      ctxp["nk"][0], ctxp["nv"][0], bias)


def _na_ctx_kernel(q_ref, k_ref, v_ref, o_ref):
    for h in range(NA_HEADS):
        sl = slice(h * NA_HEAD_DIM, (h + 1) * NA_HEAD_DIM)
        s = _dot_nt(q_ref[:, sl], k_ref[:, sl])
        e = jnp.exp2(s - jnp.max(s, axis=-1, keepdims=True))
        inv = 1.0 / jnp.sum(e, axis=-1, keepdims=True)
        o_ref[:, sl] = (_dot(e.astype(BF16), v_ref[:, sl]) * inv).astype(o_ref.dtype)


def _na_ctx(ctxp, bsz, clen):
    bw = BRANCH_WIDTH
    return pl.pallas_call(
        _na_ctx_kernel,
        grid=(bsz,),
        in_specs=[pl.BlockSpec((clen, bw), lambda b: (b, ctxp["nq"][1])),
                  pl.BlockSpec((clen, bw), lambda b: (b, ctxp["nk"][1])),
                  pl.BlockSpec((clen, bw), lambda b: (b, ctxp["nv"][1]))],
        out_specs=pl.BlockSpec((clen, bw), lambda b: (b, 0)),
        out_shape=jax.ShapeDtypeStruct((bsz * clen, bw), BF16),
        compiler_params=_params(("parallel",)),
        name="na_attn_ctx",
    )(ctxp["nq"][0], ctxp["nk"][0], ctxp["nv"][0])


def _merge_kernel(yd_ref, yl_ref, yn_ref, g0_ref, g1_ref, g2_ref, wb_ref, wo_ref, x_ref,
                  n1_ref, gate_ref, n2_ref, sc_ref, sh_ref, xm_ref, h2_ref):
    m = (g0_ref[...].astype(F32) * _dot(yd_ref[...], wb_ref[0])
         + g1_ref[...].astype(F32) * _dot(yl_ref[...], wb_ref[1])
         + g2_ref[...].astype(F32) * _dot(yn_ref[...], wb_ref[2]))
    mo = _dot(m.astype(BF16), wo_ref[...])
    xm = x_ref[...] + gate_ref[...] * _rms(mo, n1_ref[...])
    xm_ref[...] = xm
    h2 = _rms(xm, n2_ref[...]) * (1.0 + sc_ref[...]) + sh_ref[...]
    h2_ref[...] = h2.astype(h2_ref.dtype)


def _merge(yd, yl, yn, gates, wb, wo, x, n1, n2, mod, row_fn, tm, h2_dtype):
    n, d = x.shape
    bw = BRANCH_WIDTH
    p, gcol = gates
    resident = dict(pipeline_mode=pl.Buffered(1))
    return pl.pallas_call(
        _merge_kernel,
        grid=(n // tm,),
        in_specs=[pl.BlockSpec((tm, bw), lambda i: (i, 0)),
                  pl.BlockSpec((tm, bw), lambda i: (i, 0)),
                  pl.BlockSpec((tm, bw), lambda i: (i, 0)),
                  pl.BlockSpec((tm, d), lambda i: (i, gcol)),
                  pl.BlockSpec((tm, d), lambda i: (i, gcol + 1)),
                  pl.BlockSpec((tm, d), lambda i: (i, gcol + 2)),
                  pl.BlockSpec((3, bw, d), lambda i: (0, 0, 0), **resident),
                  pl.BlockSpec((d, d), lambda i: (0, 0), **resident),
                  pl.BlockSpec((tm, d), lambda i: (i, 0)),
                  _vec_spec(d),
                  _mod_spec(d, M_G1, row_fn),
                  _vec_spec(d),
                  _mod_spec(d, M_SC2, row_fn),
                  _mod_spec(d, M_SH2, row_fn)],
        out_specs=[pl.BlockSpec((tm, d), lambda i: (i, 0)),
                   pl.BlockSpec((tm, d), lambda i: (i, 0))],
        out_shape=[jax.ShapeDtypeStruct((n, d), F32),
                   jax.ShapeDtypeStruct((n, d), h2_dtype)],
        compiler_params=_params(("parallel",)),
        name="merge",
    )(yd, yl, yn, p, p, p, wb, wo, x, n1.reshape(1, d), mod, n2.reshape(1, d), mod, mod)


def _swiglu_step(x_ref, wg_ref, wu_ref, wd_ref, acc_ref, f):
    @pl.when(f == 0)
    def _():
        acc_ref[...] = jnp.zeros_like(acc_ref)
    xb = x_ref[...].astype(BF16)
    g = _dot(xb, wg_ref[...])
    u = _dot(xb, wu_ref[...])
    a = (g * jax.nn.sigmoid(g) * u).astype(BF16)
    acc_ref[...] += _dot(a, wd_ref[...])


def _ffn_dense_kernel(x_ref, wg_ref, wu_ref, wd_ref, xm_ref, gate_ref, n3_ref, *rest, nf, with_next):
    if with_next:
        n0_ref, sc_ref, sh_ref, xo_ref, hn_ref, acc_ref = rest
    else:
        xo_ref, acc_ref = rest
    f = pl.program_id(1)
    _swiglu_step(x_ref, wg_ref, wu_ref, wd_ref, acc_ref, f)

    @pl.when(f == nf - 1)
    def _():
        xo = xm_ref[...] + gate_ref[...] * _rms(acc_ref[...], n3_ref[...])
        xo_ref[...] = xo
        if with_next:
            hn = _rms(xo, n0_ref[...]) * (1.0 + sc_ref[...]) + sh_ref[...]
            hn_ref[...] = hn.astype(hn_ref.dtype)


def _ffn_dense(h2, wg, wu, wd, xm, n3, mod, row_fn, tm, tf, nxt):
    n, d = h2.shape
    fp = wg.shape[1]
    nf = fp // tf
    with_next = nxt is not None
    in_specs = [pl.BlockSpec((tm, d), lambda i, f: (i, 0)),
                pl.BlockSpec((d, tf), lambda i, f: (0, f)),
                pl.BlockSpec((d, tf), lambda i, f: (0, f)),
                pl.BlockSpec((tf, d), lambda i, f: (f, 0)),
                pl.BlockSpec((tm, d), lambda i, f: (i, 0)),
                _mod_spec(d, M_G2, row_fn),
                _vec_spec(d)]
    args = [h2, wg, wu, wd, xm, mod, n3.reshape(1, d)]
    out_specs = [pl.BlockSpec((tm, d), lambda i, f: (i, 0))]
    out_shape = [jax.ShapeDtypeStruct((n, d), F32)]
    if with_next:
        n0, mod_next = nxt
        in_specs += [_vec_spec(d), _mod_spec(d, M_SC1, row_fn), _mod_spec(d, M_SH1, row_fn)]
        args += [n0.reshape(1, d), mod_next, mod_next]
        out_specs.append(pl.BlockSpec((tm, d), lambda i, f: (i, 0)))
        out_shape.append(jax.ShapeDtypeStruct((n, d), BF16))
    kern = functools.partial(_ffn_dense_kernel, nf=nf, with_next=with_next)
    res = pl.pallas_call(
        kern,
        grid=(n // tm, nf),
        in_specs=in_specs,
        out_specs=out_specs,
        out_shape=out_shape,
        scratch_shapes=[pltpu.VMEM((tm, d), F32)],
        compiler_params=_params(("parallel", "arbitrary")),
        name="ffn_dense",
    )(*args)
    return (res[0], res[1]) if with_next else (res[0], None)


def _row_copy(src_hbm, src_row, dst_ref, dst_row, sem):
    return pltpu.make_async_copy(src_hbm.at[pl.ds(src_row, 1)], dst_ref.at[pl.ds(dst_row, 1)], sem)


def _ffn_grouped_kernel(te_ref, nt_ref, src_ref, x_hbm, wg_lo, wu_lo, wd_lo, wg_hi, wu_hi, wd_hi,
                        o_ref, rows_ref, xb_ref, sem, *, nf, tm, grid_tiles, n_lo):
    i = pl.program_id(0)
    f = pl.program_id(1)
    nt = nt_ref[0]
    slot = i % 2
    rows_per_step = tm // nf

    def start_row(tile, slot_, r):
        _row_copy(x_hbm, src_ref[tile * tm + r], rows_ref.at[slot_], r, sem.at[slot_]).start()

    def wait_rows(slot_):
        pltpu.make_async_copy(x_hbm.at[pl.ds(0, tm)], rows_ref.at[slot_], sem.at[slot_]).wait()

    @pl.when((i == 0) & (f == 0))
    def _():
        def body(r, c):
            start_row(0, 0, r)
            return c
        lax.fori_loop(0, tm, body, 0)

    @pl.when((i < nt) & (f == 0))
    def _():
        wait_rows(slot)
        xb_ref[...] = rows_ref[slot].astype(BF16)
        o_ref[...] = jnp.zeros_like(o_ref)

    def step(wg_ref, wu_ref, wd_ref):
        nxt = jnp.minimum(i + 1, grid_tiles - 1)
        for j in range(rows_per_step):
            start_row(nxt, 1 - slot, f * rows_per_step + j)
        xb = xb_ref[...]
        g = _dot(xb, wg_ref[...])
        u = _dot(xb, wu_ref[...])
        a = (g * jax.nn.sigmoid(g) * u).astype(BF16)
        o_ref[...] += _dot(a, wd_ref[...])

    is_lo = te_ref[i] < n_lo
    pl.when((i < nt) & is_lo)(lambda: step(wg_lo, wu_lo, wd_lo))
    pl.when((i < nt) & jnp.logical_not(is_lo))(lambda: step(wg_hi, wu_hi, wd_hi))

    @pl.when((i == nt - 1) & (f == nf - 1))
    def _():
        wait_rows(1 - slot)

    @pl.when((i >= nt) & (f == nf - 1))
    def _():
        o_ref[...] = jnp.zeros_like(o_ref)


def _ffn_grouped(h2, src, w_lo, w_hi, tile_expert, n_tiles, p_max, tm, tf):
    d = h2.shape[1]
    n_lo = w_lo[0].shape[0]
    n_hi = w_hi[0].shape[0]
    fe = w_lo[0].shape[2]
    nf = fe // tf
    assert tm % nf == 0, (tm, nf)

    def lo_idx(i, f, te, nt):
        used = (i < nt[0]) & (te[i] < n_lo)
        return jnp.minimum(te[i], n_lo - 1), jnp.where(used, f, nf - 1)

    def hi_idx(i, f, te, nt):
        used = (i < nt[0]) & (te[i] >= n_lo)
        return jnp.clip(te[i] - n_lo, 0, n_hi - 1), jnp.where(used, f, jnp.where(te[i] >= n_lo, nf - 1, 0))

    def specs(idx):
        def col(i, f, te, nt, src):
            e, c = idx(i, f, te, nt)
            return e, 0, c

        def row(i, f, te, nt, src):
            e, c = idx(i, f, te, nt)
            return e, c, 0
        return [pl.BlockSpec((None, d, tf), col), pl.BlockSpec((None, d, tf), col),
                pl.BlockSpec((None, tf, d), row)]

    kern = functools.partial(_ffn_grouped_kernel, nf=nf, tm=tm, grid_tiles=p_max // tm, n_lo=n_lo)
    return pl.pallas_call(
        kern,
        grid_spec=pltpu.PrefetchScalarGridSpec(
            num_scalar_prefetch=3,
            grid=(p_max // tm, nf),
            in_specs=[pl.BlockSpec(memory_space=pl.ANY)] + specs(lo_idx) + specs(hi_idx),
            out_specs=pl.BlockSpec((tm, d), lambda i, f, te, nt, src: (i, 0)),
            scratch_shapes=[pltpu.VMEM((2, tm, d), h2.dtype),
                            pltpu.VMEM((tm, d), BF16),
                            pltpu.SemaphoreType.DMA((2,))]),
        out_shape=jax.ShapeDtypeStruct((p_max, d), F32),
        compiler_params=pltpu.CompilerParams(dimension_semantics=("arbitrary", "arbitrary"),
                                             vmem_limit_bytes=VMEM_LIMIT,
                                             disable_bounds_checks=True),
        name="ffn_grouped",
    )(tile_expert, n_tiles, src, h2, *w_lo, *w_hi)


def _router_kernel(h_ref, wr_ref, o_ref, cnt_ref, carry_ref, *, n_experts):
    i = pl.program_id(0)

    @pl.when(i == 0)
    def _():
        carry_ref[...] = jnp.zeros_like(carry_ref)

    logits = _dot(h_ref[...].astype(BF16), wr_ref[...])
    tm = logits.shape[0]
    lane_i = lax.broadcasted_iota(I32, logits.shape, 1)
    lane = lane_i.astype(F32)
    logits = jnp.where(lane_i < n_experts, logits, -jnp.inf)
    m1 = jnp.max(logits, axis=-1, keepdims=True)
    i1 = jnp.min(jnp.where(logits == m1, lane, float(LANES)), axis=-1, keepdims=True)
    rest = jnp.where(lane == i1, -jnp.inf, logits)
    m2 = jnp.max(rest, axis=-1, keepdims=True)
    i2 = jnp.min(jnp.where(rest == m2, lane, float(LANES)), axis=-1, keepdims=True)
    e21 = jnp.exp(m2 - m1)
    w1 = 1.0 / (1.0 + e21)
    w2 = e21 * w1

    sel1 = lane == i1
    sel2 = lane == i2
    onehot = jnp.where(sel1 | sel2, 1.0, 0.0).astype(BF16)
    r = lax.broadcasted_iota(I32, (tm, tm), 0)
    c = lax.broadcasted_iota(I32, (tm, tm), 1)
    strict_lower = jnp.where(c < r, 1.0, 0.0).astype(BF16)
    before = _dot(strict_lower, onehot) + carry_ref[0:1, :]
    r1 = jnp.sum(jnp.where(sel1, before, 0.0), axis=-1, keepdims=True)
    r2 = jnp.sum(jnp.where(sel2, before, 0.0), axis=-1, keepdims=True)
    total = carry_ref[0:1, :] + jnp.sum(onehot.astype(F32), axis=0, keepdims=True)
    carry_ref[...] = jnp.broadcast_to(total, carry_ref.shape)
    cnt_ref[...] = jnp.broadcast_to(total, cnt_ref.shape)

    packed = jnp.where(lane_i == 0, i1, 0.0)
    packed = jnp.where(lane_i == 1, i2, packed)
    packed = jnp.where(lane_i == 2, r1, packed)
    packed = jnp.where(lane_i == 3, r2, packed)
    packed = jnp.where(lane_i == 4, w1, packed)
    packed = jnp.where(lane_i == 5, w2, packed)
    o_ref[...] = packed


def _router(h2, w_router, tm):
    n, d = h2.shape
    n_experts = w_router.shape[1]
    wr = jnp.pad(w_router, ((0, 0), (0, LANES - n_experts))).astype(BF16)
    kern = functools.partial(_router_kernel, n_experts=n_experts)
    return pl.pallas_call(
        kern,
        grid=(n // tm,),
        in_specs=[pl.BlockSpec((tm, d), lambda i: (i, 0)),
                  pl.BlockSpec((d, LANES), lambda i: (0, 0))],
        out_specs=[pl.BlockSpec((tm, LANES), lambda i: (i, 0)),
                   pl.BlockSpec((SUBLANES, LANES), lambda i: (0, 0))],
        out_shape=[jax.ShapeDtypeStruct((n, LANES), F32),
                   jax.ShapeDtypeStruct((SUBLANES, LANES), F32)],
        scratch_shapes=[pltpu.VMEM((SUBLANES, LANES), F32)],
        compiler_params=_params(("arbitrary",)),
        name="router",
    )(h2, wr)


def _combine_kernel(d1_ref, d2_ref, y_hbm, r_ref, xm_ref, gate_ref, n3_ref, xo_ref, buf_ref, sem, *, tm):
    base = pl.program_id(0) * tm

    def issue(r, c):
        _row_copy(y_hbm, d1_ref[base + r], buf_ref.at[0], r, sem).start()
        _row_copy(y_hbm, d2_ref[base + r], buf_ref.at[1], r, sem).start()
        return c
    lax.fori_loop(0, tm, issue, 0)

    for k in range(TOP_K):
        pltpu.make_async_copy(y_hbm.at[pl.ds(0, tm)], buf_ref.at[k], sem).wait()

    route = r_ref[...]
    w1 = route[:, 4:5]
    w2 = route[:, 5:6]
    y = w1 * buf_ref[0] + w2 * buf_ref[1]
    xo_ref[...] = xm_ref[...] + gate_ref[...] * _rms(y, n3_ref[...])


def _combine(yo, dest1, dest2, route, xm, n3, mod, row_fn, tm):
    n, d = xm.shape
    kern = functools.partial(_combine_kernel, tm=tm)
    return pl.pallas_call(
        kern,
        grid_spec=pltpu.PrefetchScalarGridSpec(
            num_scalar_prefetch=2,
            grid=(n // tm,),
            in_specs=[pl.BlockSpec(memory_space=pl.ANY),
                      pl.BlockSpec((tm, LANES), lambda i, *_: (i, 0)),
                      pl.BlockSpec((tm, d), lambda i, *_: (i, 0)),
                      _mod_spec(d, M_G2, row_fn),
                      _vec_spec(d)],
            out_specs=pl.BlockSpec((tm, d), lambda i, *_: (i, 0)),
            scratch_shapes=[pltpu.VMEM((2, tm, d), F32), pltpu.SemaphoreType.DMA(())]),
        out_shape=jax.ShapeDtypeStruct((n, d), F32),
        compiler_params=pltpu.CompilerParams(dimension_semantics=("arbitrary",),
                                             vmem_limit_bytes=VMEM_LIMIT,
                                             disable_bounds_checks=True),
        name="moe_combine",
    )(dest1, dest2, yo, route, xm, mod, n3.reshape(1, d))


def _moe(h2, xm, w_router, w_lo, w_hi, n3, mod, row_fn, tm_g, tf):
    n, d = h2.shape
    n_experts = w_router.shape[1]
    route, counts = _router(h2, w_router, min(512, n))
    e1 = route[:, 0].astype(I32)
    e2 = route[:, 1].astype(I32)
    r1 = route[:, 2].astype(I32)
    r2 = route[:, 3].astype(I32)
    cnt = counts[0, :n_experts].astype(I32)
    padded = ((cnt + tm_g - 1) // tm_g) * tm_g
    ends = jnp.cumsum(padded)
    offs = ends - padded
    dest1 = offs[e1] + r1
    dest2 = offs[e2] + r2
    p_max = ((TOP_K * n + n_experts * (tm_g - 1)) // tm_g) * tm_g
    n_rows = ends[-1:]
    tok = jnp.arange(n, dtype=I32)
    src = jnp.zeros((p_max,), I32).at[dest1].set(tok).at[dest2].set(tok)
    tile_start = jnp.arange(p_max // tm_g, dtype=I32) * tm_g
    tile_expert = jnp.minimum(jnp.sum(tile_start[:, None] >= ends[None, :], axis=1),
                              n_experts - 1).astype(I32)
    last_expert = tile_expert[jnp.maximum(n_rows[0] // tm_g - 1, 0)]
    tile_expert = jnp.where(tile_start < n_rows[0], tile_expert, last_expert)
    yo = _ffn_grouped(h2, src, w_lo, w_hi, tile_expert, n_rows // tm_g, p_max, tm_g, tf)
    return _combine(yo, dest1, dest2, route, xm, n3, mod, row_fn, min(256, n))


def _rope_tables(seq):
    inv = ROPE_BASE ** (-jnp.arange(ROPE_FREQ, dtype=F32) / ROPE_FREQ)
    t = jnp.arange(seq, dtype=I32)
    pos = jnp.stack([t // GRID_W, t % GRID_W], axis=-1).astype(F32)
    ang = pos[:, :, None] * inv
    cos, sin = jnp.cos(ang), jnp.sin(ang)
    cos = jnp.concatenate([cos, cos], axis=-1).reshape(seq, DIFF_HEAD_DIM)
    sin = jnp.concatenate([-sin, sin], axis=-1).reshape(seq, DIFF_HEAD_DIM)
    reps = LANES // DIFF_HEAD_DIM
    return jnp.tile(cos, (1, reps)), jnp.tile(sin, (1, reps))


def _pad_cols(w, mult):
    pad = (-w.shape[-1]) % mult
    return jnp.pad(w, [(0, 0)] * (w.ndim - 1) + [(0, pad)]) if pad else w


def _pad_rows(w, mult):
    pad = (-w.shape[-2]) % mult
    return jnp.pad(w, [(0, 0)] * (w.ndim - 2) + [(0, pad), (0, 0)]) if pad else w


def kernel(x, c, ctx, c_ctx, norm_g, w_ada, b_ada, w_in, diff_lambda, diff_subln_g, lru_conv_w, lru_conv_b, lru_gate_w, lru_gate_b, lru_lambda, na_rpb, w_branch, w_merge, b_merge, w_out, ffn_w_gate, ffn_w_up, ffn_w_down, moe_w_router, moe_w_gate, moe_w_up, moe_w_down):
    bsz, seq, d = x.shape
    clen = ctx.shape[1]
    depth = w_in.shape[0]
    n, nc = bsz * seq, bsz * clen
    x_lat = x.reshape(n, d)
    x_ctx = ctx.reshape(nc, d)

    mr = -(-(bsz + 1) // SUBLANES) * SUBLANES
    cvec = jnp.concatenate([c, c_ctx[None], jnp.zeros((mr - bsz - 1, d), F32)], axis=0)
    mod_all = _ada(cvec, w_ada, b_ada).reshape(depth, mr, 1, 6 * d)
    cos_t, sin_t = _rope_tables(seq)

    tm_lat, tm_ctx = min(1024, seq), min(1024, nc, seq)
    tm_mix_lat, tm_mix_ctx = min(256, seq), min(256, nc)
    tm_ffn_lat, tm_ffn_ctx = min(512, seq), min(512, nc)
    tf = min(512, d)
    tq = min(512, seq)

    def lat_row(tile_rows):
        per_batch = seq // tile_rows
        return lambda i: i // per_batch

    def ctx_row(i):
        return bsz

    h_lat = _prenorm(x_lat, norm_g[0, 0], mod_all[0], lat_row(tm_lat), tm_lat)
    h_ctx = _prenorm(x_ctx, norm_g[0, 0], mod_all[0], ctx_row, tm_ctx)

    rest_names = ("dv", "lx", "lg", "nq", "nk", "nv")
    no_bias = jnp.zeros((1, IN_WIDTH), F32)
    n_experts = moe_w_router.shape[-1]
    n_lo = n_experts // 2
    diff_steps = bsz * DIFF_HEADS * (seq // tq)

    def expert_rows(w, first, count):
        return w.reshape(-1, w.shape[-1]), first * w.shape[1], count * w.shape[1]

    def side_cast_ok(jobs):
        return all(rows % diff_steps == 0 and (rows // diff_steps) % 16 == 0
                   and row0 % (rows // diff_steps) == 0 for _, row0, rows in jobs)

    moe_bf16 = {}
    for l in range(depth):
        need_ctx = l < depth - 1
        mod = mod_all[l]
        lam_init = 0.8 - 0.6 * math.exp(-0.3 * l)
        w_qk = w_in[l][:, :2 * BRANCH_WIDTH].astype(BF16)
        w_rest = w_in[l][:, 2 * BRANCH_WIDTH:].astype(BF16)
        w_gate = w_merge[l].astype(BF16)
        b_gate = b_merge[l][None]

        def project(h, tm, rope):
            qk = _proj(h, w_qk, no_bias, cos_t, sin_t, seq, tm, "rope" if rope else "plain",
                       P_DQ, DIFF_QSCALE)
            rest = _proj(h, w_rest, no_bias, cos_t, sin_t, seq, tm, "plain", P_NQ - 2, NA_QSCALE)
            parts = {"dq": (qk, 0), "dk": (qk, 1)}
            parts.update({name: (rest, k) for k, name in enumerate(rest_names)})
            return parts

        lat = project(h_lat, tm_lat, True)
        ctxp = project(h_ctx, tm_ctx, False)
        gate_lat = _proj(h_lat, w_gate, b_gate, cos_t, sin_t, seq, tm_lat, "gate")

        mi, half = l // 2, ("lo", "hi")[l % 2]
        jobs = ()
        if mi < moe_w_gate.shape[0]:
            first, count = (0, n_lo) if half == "lo" else (n_lo, n_experts - n_lo)
            jobs = tuple(expert_rows(w[mi], first, count) for w in (moe_w_gate, moe_w_up, moe_w_down))
            if not side_cast_ok(jobs):
                moe_bf16[mi, half] = tuple(w[mi, first:first + count].astype(BF16)
                                           for w in (moe_w_gate, moe_w_up, moe_w_down))
                jobs = ()
        y_diff, cast = _diff_attn_lat(lat, ctxp, diff_lambda[l], diff_subln_g[l], lam_init,
                                      bsz, seq, clen, tq, jobs)
        if jobs:
            moe_bf16[mi, half] = tuple(c.reshape((count,) + w.shape[2:])
                                       for c, w in zip(cast, (moe_w_gate, moe_w_up, moe_w_down)))
        y_lru, y_lru_c = _lru(lat, ctxp, lru_conv_w[l], lru_conv_b[l], lru_gate_w[l],
                              lru_gate_b[l], lru_lambda[l], bsz, seq, clen, need_ctx)
        y_na = _na_lat(lat, ctxp, na_rpb[l], bsz, seq, clen)

        wb = w_branch[l].astype(BF16)
        wo = w_out[l].astype(BF16)
        is_moe = l % 2 == 1
        h2_dtype = F32 if is_moe else BF16
        xm_lat, h2_lat = _merge(y_diff, y_lru, y_na, (gate_lat, 0), wb, wo, x_lat, norm_g[l, 1],
                                norm_g[l, 2], mod, lat_row(tm_mix_lat), tm_mix_lat, h2_dtype)
        if need_ctx:
            y_diff_c = _diff_attn_ctx(ctxp, diff_lambda[l], diff_subln_g[l], lam_init, bsz, clen)
            y_na_c = _na_ctx(ctxp, bsz, clen)
            gate_ctx = _proj(h_ctx, w_gate, b_gate, cos_t, sin_t, seq, tm_ctx, "gate")
            xm_ctx, h2_ctx = _merge(y_diff_c, y_lru_c, y_na_c, (gate_ctx, 0), wb, wo, x_ctx,
                                    norm_g[l, 1], norm_g[l, 2], mod, ctx_row, tm_mix_ctx, h2_dtype)

        nxt = (norm_g[l + 1, 0], mod_all[l + 1]) if need_ctx else None
        i = l // 2
        if is_moe:
            w_lo, w_hi = moe_bf16[i, "lo"], moe_bf16[i, "hi"]
            tm_g = min(MOE_TILE, n)
            x_lat = _moe(h2_lat, xm_lat, moe_w_router[i], w_lo, w_hi, norm_g[l, 3], mod,
                         lat_row(min(256, n)), tm_g, tf)
            if need_ctx:
                x_ctx = _moe(h2_ctx, xm_ctx, moe_w_router[i], w_lo, w_hi, norm_g[l, 3], mod,
                             ctx_row, min(MOE_TILE, nc), tf)
                h_lat = _prenorm(x_lat, nxt[0], nxt[1], lat_row(tm_lat), tm_lat)
                h_ctx = _prenorm(x_ctx, nxt[0], nxt[1], ctx_row, tm_ctx)
        else:
            wg = _pad_cols(ffn_w_gate[i], tf).astype(BF16)
            wu = _pad_cols(ffn_w_up[i], tf).astype(BF16)
            wd = _pad_rows(ffn_w_down[i], tf).astype(BF16)
            x_lat, h_lat = _ffn_dense(h2_lat, wg, wu, wd, xm_lat, norm_g[l, 3], mod,
                                      lat_row(tm_ffn_lat), tm_ffn_lat, tf, nxt)
            if need_ctx:
                x_ctx, h_ctx = _ffn_dense(h2_ctx, wg, wu, wd, xm_ctx, norm_g[l, 3], mod,
                                          ctx_row, tm_ffn_ctx, tf, nxt)
    return x_lat.reshape(bsz, seq, d)
```

```python
import functools
import math

import numpy as np
import jax
import jax.numpy as jnp
from jax import lax
from jax.experimental import pallas as pl
from jax.experimental.pallas import tpu as pltpu

F32 = jnp.float32
BF16 = jnp.bfloat16
I32 = jnp.int32

EPS = 1e-6
GRID_W = 64
BRANCH_WIDTH = 1024
N_IN_PARTS = 8
IN_WIDTH = N_IN_PARTS * BRANCH_WIDTH
DIFF_HEADS = 8
DIFF_HEAD_DIM = 64
DIFF_V_DIM = 2 * DIFF_HEAD_DIM
ROPE_BASE = 10000.0
ROPE_FREQ = DIFF_HEAD_DIM // 4
LRU_BLOCKS = 8
LRU_BLOCK_DIM = BRANCH_WIDTH // LRU_BLOCKS
LRU_CONV_W = 4
LRU_C = 8.0
NA_HEADS = 8
NA_HEAD_DIM = BRANCH_WIDTH // NA_HEADS
NA_WIN_R = 8
NA_WIN_C = 16
TOP_K = 2
MOE_TILE = 560
MOE_TF = 1024
DIFF_TQ = 1024

LANES = 128
SUBLANES = 8
VMEM_LIMIT = 56 * 1024 * 1024
NEG = -1e30
LOG2E = 1.0 / math.log(2.0)
DIFF_QSCALE = DIFF_HEAD_DIM ** -0.5 * LOG2E
NA_QSCALE = NA_HEAD_DIM ** -0.5 * LOG2E

P_DQ, P_DK, P_DV, P_LX, P_LG, P_NQ, P_NK, P_NV = range(8)
M_SH1, M_SC1, M_G1, M_SH2, M_SC2, M_G2 = range(6)


def _params(sem):
    return pltpu.CompilerParams(dimension_semantics=sem, vmem_limit_bytes=VMEM_LIMIT)


def _rms(x, g):
    return x * lax.rsqrt(jnp.mean(x * x, axis=-1, keepdims=True) + EPS) * g


def _dot(a, b):
    return jnp.dot(a, b, preferred_element_type=F32)


def _dot_nt(a, b):
    return lax.dot_general(a, b, (((1,), (1,)), ((), ())), preferred_element_type=F32)


def _ada_kernel(c_ref, w_ref, b_ref, o_ref):
    c = c_ref[...]
    s = (c * jax.nn.sigmoid(c)).astype(BF16)
    o_ref[...] = _dot(s, w_ref[...].astype(BF16)) + b_ref[...]


def _ada(cvec, w_ada, b_ada):
    n_layers, d, n6 = w_ada.shape
    mr = cvec.shape[0]
    tn = min(1024, d)
    return pl.pallas_call(
        _ada_kernel,
        grid=(n_layers, n6 // tn),
        in_specs=[pl.BlockSpec((mr, d), lambda l, j: (0, 0)),
                  pl.BlockSpec((None, d, tn), lambda l, j: (l, 0, j)),
                  pl.BlockSpec((None, 1, tn), lambda l, j: (l, 0, j))],
        out_specs=pl.BlockSpec((None, mr, tn), lambda l, j: (l, 0, j)),
        out_shape=jax.ShapeDtypeStruct((n_layers, mr, n6), F32),
        compiler_params=_params(("arbitrary", "arbitrary")),
        name="ada_mod",
    )(cvec, w_ada, b_ada.reshape(n_layers, 1, n6))


def _mod_spec(d, chunk, row_fn):
    return pl.BlockSpec((None, 1, d), lambda i, *_: (row_fn(i), 0, chunk))


def _vec_spec(d):
    return pl.BlockSpec((1, d), lambda i, *_: (0, 0))


def _prenorm_kernel(x_ref, g_ref, sc_ref, sh_ref, o_ref):
    y = _rms(x_ref[...], g_ref[...])
    o_ref[...] = (y * (1.0 + sc_ref[...]) + sh_ref[...]).astype(o_ref.dtype)


def _prenorm(x, g, mod, row_fn, tm):
    n, d = x.shape
    return pl.pallas_call(
        _prenorm_kernel,
        grid=(n // tm,),
        in_specs=[pl.BlockSpec((tm, d), lambda i: (i, 0)),
                  _vec_spec(d),
                  _mod_spec(d, M_SC1, row_fn),
                  _mod_spec(d, M_SH1, row_fn)],
        out_specs=pl.BlockSpec((tm, d), lambda i: (i, 0)),
        out_shape=jax.ShapeDtypeStruct((n, d), BF16),
        compiler_params=_params(("parallel",)),
        name="prenorm",
    )(x, g.reshape(1, d), mod, mod)


def _proj_kernel(h_ref, w_ref, b_ref, cos_ref, sin_ref, o_ref, *, kind, scaled_part, scale,
                 tiles_per_part):
    acc = _dot(h_ref[...], w_ref[...])
    tn = acc.shape[1]
    if kind == "gate":
        o_ref[...] = jax.nn.sigmoid(acc + b_ref[...]).astype(o_ref.dtype)
        return
    s = jnp.where(pl.program_id(1) // tiles_per_part == scaled_part, scale, 1.0).astype(F32)
    if kind == "rope":
        lane = lax.broadcasted_iota(I32, acc.shape, 1)
        first_half = (lane % (2 * ROPE_FREQ)) < ROPE_FREQ
        partner = jnp.where(first_half, pltpu.roll(acc, tn - ROPE_FREQ, 1),
                            pltpu.roll(acc, ROPE_FREQ, 1))
        cos = jnp.tile(cos_ref[...], (1, tn // LANES)) * s
        sin = jnp.tile(sin_ref[...], (1, tn // LANES)) * s
        o_ref[...] = (acc * cos + partner * sin).astype(o_ref.dtype)
    else:
        o_ref[...] = (acc * s).astype(o_ref.dtype)


def _proj(h, w, b, cos_t, sin_t, seq, tm, kind, scaled_part=-1, scale=1.0):
    n, d = h.shape
    nw = w.shape[1]
    tn = min(1024, d)
    tpb = seq // tm
    kern = functools.partial(_proj_kernel, kind=kind, scaled_part=scaled_part, scale=scale,
                             tiles_per_part=BRANCH_WIDTH // tn)
    return pl.pallas_call(
        kern,
        grid=(n // tm, nw // tn),
        in_specs=[pl.BlockSpec((tm, d), lambda i, j: (i, 0)),
                  pl.BlockSpec((d, tn), lambda i, j: (0, j)),
                  pl.BlockSpec((1, tn), lambda i, j: (0, j)),
                  pl.BlockSpec((tm, LANES), lambda i, j: (i % tpb, 0)),
                  pl.BlockSpec((tm, LANES), lambda i, j: (i % tpb, 0))],
        out_specs=pl.BlockSpec((tm, tn), lambda i, j: (i, j)),
        out_shape=jax.ShapeDtypeStruct((n, nw), BF16),
        compiler_params=_params(("parallel", "arbitrary")),
        name="proj_" + kind,
    )(h, w, b, cos_t, sin_t)


DIFF_KEY_CHUNK = 512


def _diff_kernel(*refs, lam_init, has_lat, n_cast, n_alias=0):
    n_in = 7 if has_lat else 5
    n_extra = n_cast + n_alias
    for src_ref, dst_ref in zip(refs[n_in:n_in + n_cast], refs[n_in + n_extra + 1:]):
        dst_ref[0] = src_ref[...].astype(dst_ref.dtype)
        if dst_ref.shape[0] == 2:
            dst_ref[1] = jnp.zeros(dst_ref.shape[1:], dst_ref.dtype)
    if has_lat:
        lamv_ref, sg_ref, q_ref, k_ref, v_ref, kc_ref, vc_ref = refs[:n_in]
    else:
        lamv_ref, sg_ref, q_ref, kc_ref, vc_ref = refs[:n_in]
    o_ref = refs[n_in + n_extra]
    lv = lamv_ref[...]
    lam = (jnp.exp(jnp.sum(lv[0:1] * lv[1:2], axis=-1, keepdims=True))
           - jnp.exp(jnp.sum(lv[2:3] * lv[3:4], axis=-1, keepdims=True)) + lam_init)
    q = q_ref[...]
    first = lax.broadcasted_iota(I32, q.shape, 1) < DIFF_HEAD_DIM
    zero = jnp.zeros_like(q)
    qz = (jnp.where(first, q, zero), jnp.where(first, zero, q))
    chunks = [(kc_ref, vc_ref, 0, kc_ref.shape[0])]
    if has_lat:
        kt = min(DIFF_KEY_CHUNK, k_ref.shape[0])
        chunks += [(k_ref, v_ref, j * kt, kt) for j in range(k_ref.shape[0] // kt)]

    def qk(c, chunk):
        k, _, off, n = chunk
        return _dot_nt(qz[c], k[off:off + n, :])

    def row_max(parts):
        m = jnp.max(parts[0], axis=-1, keepdims=True)
        for p in parts[1:]:
            m = jnp.maximum(m, jnp.max(p, axis=-1, keepdims=True))
        return m

    def softmax_av(s_parts, between=None):
        m = row_max(s_parts)
        den, out = 0.0, 0.0
        for j, (_, v, off, n) in enumerate(chunks):
            if between is not None:
                between(j)
            e = jnp.exp2(s_parts[j] - m)
            den = den + jnp.sum(e, axis=-1, keepdims=True)
            out = out + _dot(e.astype(BF16), v[off:off + n, :])
        return out, den

    s1 = [qk(0, ch) for ch in chunks]
    s2 = []
    o1, den1 = softmax_av(s1, between=lambda j: s2.append(qk(1, chunks[j])))
    o2, den2 = softmax_av(s2)
    o = o1 * (1.0 / den1) - o2 * (lam / den2)
    y = _rms(o, sg_ref[...]) * (1.0 - lam_init)
    o_ref[...] = y.astype(o_ref.dtype)


def _diff_attn_lat(lat, ctxp, lamv, subln_g, lam_init, bsz, seq, clen, tq, cast_jobs=()):
    nq = seq // tq
    n_steps = bsz * DIFF_HEADS * nq
    hw = DIFF_V_DIM
    hpb = BRANCH_WIDTH // hw

    def spec(rows, src, row_fn):
        blk = src[1] * hpb
        return pl.BlockSpec((rows, hw), lambda b, h, i: (row_fn(b, i), blk + h))

    def step(b, h, i):
        return (b * DIFF_HEADS + h) * nq + i

    n_base = 7
    cast_in, cast_out, cast_shapes, cast_args, aliases = [], [], [], [], {}
    for k, (arr, half, dst) in enumerate(cast_jobs):
        _, rows, cols = arr.shape
        per_step = rows // n_steps
        assert per_step * n_steps == rows and per_step % 16 == 0
        cast_in.append(pl.BlockSpec((None, per_step, cols), lambda b, h, i, half=half: (half, step(b, h, i), 0)))
        cast_args.append(arr)
        cast_shapes.append(jax.ShapeDtypeStruct(arr.shape, BF16))
        if dst is None:
            cast_out.append(pl.BlockSpec((2, per_step, cols), lambda b, h, i: (0, step(b, h, i), 0)))
        else:
            assert half == 1
            cast_out.append(pl.BlockSpec((1, per_step, cols), lambda b, h, i: (1, step(b, h, i), 0)))
    for k, (arr, half, dst) in enumerate(cast_jobs):
        if dst is not None:
            aliases[n_base + len(cast_jobs) + len(aliases)] = 1 + k
            cast_in.append(pl.BlockSpec(memory_space=pl.ANY))
            cast_args.append(dst)
    n_alias = len(aliases)
    kern = functools.partial(_diff_kernel, lam_init=lam_init, has_lat=True, n_cast=len(cast_jobs),
                             n_alias=n_alias)
    res = pl.pallas_call(
        kern,
        grid=(bsz, DIFF_HEADS, nq),
        in_specs=[pl.BlockSpec((4, DIFF_HEAD_DIM), lambda b, h, i: (0, 0)),
                  pl.BlockSpec((1, hw), lambda b, h, i: (0, 0)),
                  spec(tq, lat["dq"], lambda b, i: b * nq + i),
                  spec(seq, lat["dk"], lambda b, i: b),
                  spec(seq, lat["dv"], lambda b, i: b),
                  spec(clen, ctxp["dk"], lambda b, i: b),
                  spec(clen, ctxp["dv"], lambda b, i: b)] + cast_in,
        out_specs=[pl.BlockSpec((tq, hw), lambda b, h, i: (b * nq + i, h))] + cast_out,
        out_shape=[jax.ShapeDtypeStruct((bsz * seq, BRANCH_WIDTH), BF16)] + cast_shapes,
        input_output_aliases=aliases,
        compiler_params=_params(("parallel", "parallel", "arbitrary")),
        name="diff_attn",
    )(lamv, subln_g.reshape(1, hw), lat["dq"][0], lat["dk"][0], lat["dv"][0],
      ctxp["dk"][0], ctxp["dv"][0], *cast_args)
    return res[0], res[1:]


def _diff_attn_ctx(ctxp, lamv, subln_g, lam_init, bsz, clen):
    hw = DIFF_V_DIM
    hpb = BRANCH_WIDTH // hw

    def spec(src):
        blk = src[1] * hpb
        return pl.BlockSpec((clen, hw), lambda b, h: (b, blk + h))

    kern = functools.partial(_diff_kernel, lam_init=lam_init, has_lat=False, n_cast=0)
    return pl.pallas_call(
        kern,
        grid=(bsz, DIFF_HEADS),
        in_specs=[pl.BlockSpec((4, DIFF_HEAD_DIM), lambda b, h: (0, 0)),
                  pl.BlockSpec((1, hw), lambda b, h: (0, 0)),
                  spec(ctxp["dq"]), spec(ctxp["dk"]), spec(ctxp["dv"])],
        out_specs=pl.BlockSpec((clen, hw), lambda b, h: (b, h)),
        out_shape=jax.ShapeDtypeStruct((bsz * clen, BRANCH_WIDTH), BF16),
        compiler_params=_params(("parallel", "parallel")),
        name="diff_attn_ctx",
    )(lamv, subln_g.reshape(1, hw), ctxp["dq"][0], ctxp["dk"][0], ctxp["dv"][0])


LRU_TW = 256
LRU_UNROLL = 4


def _gelu_tanh(x):
    return 0.5 * x * (1.0 + jnp.tanh(math.sqrt(2.0 / math.pi) * (x + 0.044715 * (x * x * x))))


def _tile_scan(a, b, reverse):
    rows = lax.broadcasted_iota(I32, a.shape, 0)
    for s in (1, 2, 4):
        if reverse:
            a_s = pltpu.roll(a, SUBLANES - s, 0)
            b_s = pltpu.roll(b, SUBLANES - s, 0)
            valid = rows < SUBLANES - s
        else:
            a_s = pltpu.roll(a, s, 0)
            b_s = pltpu.roll(b, s, 0)
            valid = rows >= s
        b = jnp.where(valid, a * b_s + b, b)
        a = jnp.where(valid, a * a_s, a)
    return a, b


def _lru_kernel(xl_ref, gl_ref, xc_ref, gc_ref, cw_ref, cb_ref, gw_ref, gb_ref, lam_ref,
                *out_and_scratch, seq, clen, need_ctx):
    if need_ctx:
        yl_ref, yc_ref, af_ref, bf_ref, ab_ref, bb_ref, hf_ref = out_and_scratch
    else:
        yl_ref, af_ref, bf_ref, ab_ref, bb_ref, hf_ref = out_and_scratch
        yc_ref = None
    tot = clen + seq
    x = jnp.concatenate([xc_ref[...], xl_ref[...]], axis=0).astype(F32)
    row = lax.broadcasted_iota(I32, x.shape, 0)
    pos = jnp.where(row < clen, row, row - clen)
    seg_len = jnp.where(row < clen, clen, seq)
    cw = cw_ref[...]
    conv = cb_ref[...] + x * cw[2:3]
    conv = conv + jnp.where(pos >= 2, pltpu.roll(x, 2, 0), 0.0) * cw[0:1]
    conv = conv + jnp.where(pos >= 1, pltpu.roll(x, 1, 0), 0.0) * cw[1:2]
    conv = conv + jnp.where(pos < seg_len - 1, pltpu.roll(x, tot - 1, 0), 0.0) * cw[3:4]
    xb = conv.astype(BF16)
    gb = gb_ref[...]
    lam = lam_ref[...]
    for d, (a_ref, b_ref) in enumerate(((af_ref, bf_ref), (ab_ref, bb_ref))):
        r = jax.nn.sigmoid(_dot(xb, gw_ref[d, 0]) + gb[2 * d:2 * d + 1])
        i = jax.nn.sigmoid(_dot(xb, gw_ref[d, 1]) + gb[2 * d + 1:2 * d + 2])
        z = -lam[d:d + 1]
        softplus = jnp.maximum(z, 0.0) + jnp.log1p(jnp.exp(-jnp.abs(z)))
        log_a = -LRU_C * r * softplus
        a = jnp.exp(log_a)
        beta = jnp.sqrt(-jnp.tanh(log_a) * (a * a + 1.0))
        a_ref[...] = a
        b_ref[...] = beta * i * conv

    n_ct = clen // SUBLANES
    n_t = tot // SUBLANES
    w = x.shape[1]
    assert n_ct % LRU_UNROLL == 0 and n_t % LRU_UNROLL == 0
    rows_per_iter = LRU_UNROLL * SUBLANES

    def fwd_step(t, carry):
        base = pl.multiple_of(t * rows_per_iter, rows_per_iter)
        scans = [_tile_scan(af_ref[pl.ds(base + k * SUBLANES, SUBLANES), :],
                            bf_ref[pl.ds(base + k * SUBLANES, SUBLANES), :], False)
                 for k in range(LRU_UNROLL)]
        for k, (a, b) in enumerate(scans):
            h = a * carry + b
            hf_ref[pl.ds(base + k * SUBLANES, SUBLANES), :] = h
            carry = h[SUBLANES - 1:SUBLANES, :]
        return carry

    lax.fori_loop(0, n_t // LRU_UNROLL, fwd_step, jnp.zeros((1, w), F32))

    def bwd_iter(first_tile, n_tiles, g_ref, y_ref, seg_off):
        def step(t, carry):
            base = pl.multiple_of((first_tile + n_tiles) * SUBLANES - (t + 1) * rows_per_iter,
                                  rows_per_iter)
            offs = [base + k * SUBLANES for k in reversed(range(LRU_UNROLL))]
            scans = [_tile_scan(ab_ref[pl.ds(off, SUBLANES), :], bb_ref[pl.ds(off, SUBLANES), :], True)
                     for off in offs]
            for off, (a, b) in zip(offs, scans):
                h = a * carry + b
                carry = h[0:1, :]
                if y_ref is not None:
                    g = g_ref[pl.ds(off - seg_off, SUBLANES), :].astype(F32)
                    y_ref[pl.ds(off - seg_off, SUBLANES), :] = (
                        (hf_ref[pl.ds(off, SUBLANES), :] + h) * _gelu_tanh(g)).astype(y_ref.dtype)
            return carry
        return step

    carry = lax.fori_loop(0, n_ct // LRU_UNROLL, bwd_iter(0, n_ct, gc_ref, yc_ref, 0),
                          jnp.zeros((1, w), F32))
    lax.fori_loop(0, (n_t - n_ct) // LRU_UNROLL, bwd_iter(n_ct, n_t - n_ct, gl_ref, yl_ref, clen), carry)


def _lru(lat, ctxp, conv_w, conv_b, gate_w, gate_b, lru_lam, bsz, seq, clen, need_ctx):
    n = bsz * seq
    tw = LRU_TW
    nwt = BRANCH_WIDTH // tw
    per = tw // LRU_BLOCK_DIM
    gw = gate_w.reshape(2, 2, nwt, per, LRU_BLOCK_DIM, LRU_BLOCK_DIM)
    eye = jnp.eye(per, dtype=gate_w.dtype)
    gw = jnp.einsum('dgnpcf,pq->dgnpcqf', gw, eye).reshape(2, 2, nwt, tw, tw).astype(BF16)
    gb = gate_b.reshape(4, BRANCH_WIDTH)
    tot = seq + clen
    kern = functools.partial(_lru_kernel, seq=seq, clen=clen, need_ctx=need_ctx)

    def part_spec(rows, src):
        blk = src[1] * nwt
        return pl.BlockSpec((rows, tw), lambda b, j: (b, blk + j))

    out_shape = [jax.ShapeDtypeStruct((n, BRANCH_WIDTH), BF16)]
    out_specs = [pl.BlockSpec((seq, tw), lambda b, j: (b, j))]
    if need_ctx:
        out_shape.append(jax.ShapeDtypeStruct((bsz * clen, BRANCH_WIDTH), BF16))
        out_specs.append(pl.BlockSpec((clen, tw), lambda b, j: (b, j)))
    res = pl.pallas_call(
        kern,
        grid=(bsz, nwt),
        in_specs=[part_spec(seq, lat["lx"]), part_spec(seq, lat["lg"]),
                  part_spec(clen, ctxp["lx"]), part_spec(clen, ctxp["lg"]),
                  pl.BlockSpec((LRU_CONV_W, tw), lambda b, j: (0, j)),
                  pl.BlockSpec((1, tw), lambda b, j: (0, j)),
                  pl.BlockSpec((2, 2, None, tw, tw), lambda b, j: (0, 0, j, 0, 0)),
                  pl.BlockSpec((4, tw), lambda b, j: (0, j)),
                  pl.BlockSpec((2, tw), lambda b, j: (0, j))],
        out_specs=out_specs,
        out_shape=out_shape,
        scratch_shapes=[pltpu.VMEM((tot, tw), F32) for _ in range(5)],
        compiler_params=_params(("parallel", "parallel")),
        name="rglru",
    )(lat["lx"][0], lat["lg"][0], ctxp["lx"][0], ctxp["lg"][0], conv_w,
      conv_b.reshape(1, BRANCH_WIDTH), gw, gb, lru_lam)
    return (res[0], res[1]) if need_ctx else (res[0], None)


NA_ROWS_PER_TILE = 4


def _na_tables(rows, rpt):
    kr = min(NA_WIN_R, rows)
    nkr = min(rows, rpt + kr - 1)
    variants, index, tile_variant, tile_kb = [], {}, [], []
    for t in range(rows // rpt):
        r0 = t * rpt
        kb = int(np.clip(r0 - kr // 2, 0, rows - nkr))
        geom = []
        for a in range(rpt):
            r = r0 + a
            rs = int(np.clip(r - kr // 2, 0, rows - kr))
            geom.append(tuple((kb + k - r + NA_WIN_R - 1) if rs <= kb + k < rs + kr else None
                              for k in range(nkr)))
        geom = tuple(geom)
        if geom not in index:
            index[geom] = len(variants)
            variants.append(geom)
        tile_variant.append(index[geom])
        tile_kb.append(kb)
    return nkr, np.array(tile_variant, np.int32), np.array(tile_kb, np.int32), variants


def _na_bias(rpb, variants):
    heads = rpb.shape[0]
    pad = GRID_W - NA_WIN_C
    padded = jnp.pad(rpb * LOG2E, ((0, 0), (0, 0), (pad, pad)))
    toep = jnp.stack([padded[..., GRID_W - 1 - qc:2 * GRID_W - 1 - qc] for qc in range(GRID_W)],
                     axis=2)
    col = np.arange(GRID_W)
    cs = np.clip(col - NA_WIN_C // 2, 0, GRID_W - NA_WIN_C)
    in_cols = (col[None, :] >= cs[:, None]) & (col[None, :] < cs[:, None] + NA_WIN_C)
    toep = jnp.where(in_cols, toep, NEG).astype(F32)
    outside = jnp.full((heads, GRID_W, GRID_W), NEG, F32)
    return jnp.stack([
        jnp.concatenate([
            jnp.concatenate([outside if dr is None else toep[:, dr] for dr in row], axis=-1)
            for row in geom], axis=-2)
        for geom in variants])


def _na_kernel(var_ref, kb_ref, q_ref, k_ref, v_ref, kc_ref, vc_ref, bias_ref, o_ref, *, nk):
    t = pl.program_id(1)
    start = pl.multiple_of(kb_ref[t] * GRID_W, GRID_W)
    for h in range(NA_HEADS):
        sl = slice(h * NA_HEAD_DIM, (h + 1) * NA_HEAD_DIM)
        qh = q_ref[:, sl]
        s_w = _dot_nt(qh, k_ref[pl.ds(start, nk), sl]) + bias_ref[h]
        s_c = _dot_nt(qh, kc_ref[:, sl])
        m = jnp.maximum(jnp.max(s_w, axis=-1, keepdims=True), jnp.max(s_c, axis=-1, keepdims=True))
        e_w = jnp.exp2(s_w - m)
        e_c = jnp.exp2(s_c - m)
        inv = 1.0 / (jnp.sum(e_w, axis=-1, keepdims=True) + jnp.sum(e_c, axis=-1, keepdims=True))
        o = (_dot(e_w.astype(BF16), v_ref[pl.ds(start, nk), sl])
             + _dot(e_c.astype(BF16), vc_ref[:, sl]))
        o_ref[:, sl] = (o * inv).astype(o_ref.dtype)


def _na_lat(lat, ctxp, rpb, bsz, seq, clen):
    n = bsz * seq
    rows = seq // GRID_W
    rpt = min(NA_ROWS_PER_TILE, rows)
    nkr, tile_variant, tile_kb, variants = _na_tables(rows, rpt)
    bias = _na_bias(rpb, variants)
    nt = rows // rpt
    rq, nk = rpt * GRID_W, nkr * GRID_W
    bw = BRANCH_WIDTH
    kern = functools.partial(_na_kernel, nk=nk)
    return pl.pallas_call(
        kern,
        grid_spec=pltpu.PrefetchScalarGridSpec(
            num_scalar_prefetch=2,
            grid=(bsz, nt),
            in_specs=[pl.BlockSpec((rq, bw), lambda b, t, var, kb: (b * nt + t, lat["nq"][1])),
                      pl.BlockSpec((seq, bw), lambda b, t, var, kb: (b, lat["nk"][1])),
                      pl.BlockSpec((seq, bw), lambda b, t, var, kb: (b, lat["nv"][1])),
                      pl.BlockSpec((clen, bw), lambda b, t, var, kb: (b, ctxp["nk"][1])),
                      pl.BlockSpec((clen, bw), lambda b, t, var, kb: (b, ctxp["nv"][1])),
                      pl.BlockSpec((None, NA_HEADS, rq, nk), lambda b, t, var, kb: (var[t], 0, 0, 0))],
            out_specs=pl.BlockSpec((rq, bw), lambda b, t, var, kb: (b * nt + t, 0))),
        out_shape=jax.ShapeDtypeStruct((n, bw), BF16),
        compiler_params=_params(("parallel", "arbitrary")),
        name="na_attn",
    )(jnp.asarray(tile_variant), jnp.asarray(tile_kb), lat["nq"][0], lat["nk"][0], lat["nv"][0],
      ctxp["nk"][0], ctxp["nv"][0], bias)


def _na_ctx_kernel(q_ref, k_ref, v_ref, o_ref):
    for h in range(NA_HEADS):
        sl = slice(h * NA_HEAD_DIM, (h + 1) * NA_HEAD_DIM)
        s = _dot_nt(q_ref[:, sl], k_ref[:, sl])
        e = jnp.exp2(s - jnp.max(s, axis=-1, keepdims=True))
        inv = 1.0 / jnp.sum(e, axis=-1, keepdims=True)
        o_ref[:, sl] = (_dot(e.astype(BF16), v_ref[:, sl]) * inv).astype(o_ref.dtype)


def _na_ctx(ctxp, bsz, clen):
    bw = BRANCH_WIDTH
    return pl.pallas_call(
        _na_ctx_kernel,
        grid=(bsz,),
        in_specs=[pl.BlockSpec((clen, bw), lambda b: (b, ctxp["nq"][1])),
                  pl.BlockSpec((clen, bw), lambda b: (b, ctxp["nk"][1])),
                  pl.BlockSpec((clen, bw), lambda b: (b, ctxp["nv"][1]))],
        out_specs=pl.BlockSpec((clen, bw), lambda b: (b, 0)),
        out_shape=jax.ShapeDtypeStruct((bsz * clen, bw), BF16),
        compiler_params=_params(("parallel",)),
        name="na_attn_ctx",
    )(ctxp["nq"][0], ctxp["nk"][0], ctxp["nv"][0])


def _merge_kernel(yd_ref, yl_ref, yn_ref, g0_ref, g1_ref, g2_ref, wb_ref, wo_ref, x_ref,
                  n1_ref, gate_ref, n2_ref, sc_ref, sh_ref, xm_ref, h2_ref):
    m = (g0_ref[...].astype(F32) * _dot(yd_ref[...], wb_ref[0])
         + g1_ref[...].astype(F32) * _dot(yl_ref[...], wb_ref[1])
         + g2_ref[...].astype(F32) * _dot(yn_ref[...], wb_ref[2]))
    mo = _dot(m.astype(BF16), wo_ref[...])
    xm = x_ref[...] + gate_ref[...] * _rms(mo, n1_ref[...])
    xm_ref[...] = xm
    h2 = _rms(xm, n2_ref[...]) * (1.0 + sc_ref[...]) + sh_ref[...]
    h2_ref[...] = h2.astype(h2_ref.dtype)


def _merge(yd, yl, yn, gates, wb, wo, x, n1, n2, mod, row_fn, tm, h2_dtype):
    n, d = x.shape
    bw = BRANCH_WIDTH
    p, gcol = gates
    resident = dict(pipeline_mode=pl.Buffered(1))
    return pl.pallas_call(
        _merge_kernel,
        grid=(n // tm,),
        in_specs=[pl.BlockSpec((tm, bw), lambda i: (i, 0)),
                  pl.BlockSpec((tm, bw), lambda i: (i, 0)),
                  pl.BlockSpec((tm, bw), lambda i: (i, 0)),
                  pl.BlockSpec((tm, d), lambda i: (i, gcol)),
                  pl.BlockSpec((tm, d), lambda i: (i, gcol + 1)),
                  pl.BlockSpec((tm, d), lambda i: (i, gcol + 2)),
                  pl.BlockSpec((3, bw, d), lambda i: (0, 0, 0), **resident),
                  pl.BlockSpec((d, d), lambda i: (0, 0), **resident),
                  pl.BlockSpec((tm, d), lambda i: (i, 0)),
                  _vec_spec(d),
                  _mod_spec(d, M_G1, row_fn),
                  _vec_spec(d),
                  _mod_spec(d, M_SC2, row_fn),
                  _mod_spec(d, M_SH2, row_fn)],
        out_specs=[pl.BlockSpec((tm, d), lambda i: (i, 0)),
                   pl.BlockSpec((tm, d), lambda i: (i, 0))],
        out_shape=[jax.ShapeDtypeStruct((n, d), F32),
                   jax.ShapeDtypeStruct((n, d), h2_dtype)],
        compiler_params=_params(("parallel",)),
        name="merge",
    )(yd, yl, yn, p, p, p, wb, wo, x, n1.reshape(1, d), mod, n2.reshape(1, d), mod, mod)


def _swiglu_step(x_ref, wg_ref, wu_ref, wd_ref, acc_ref, f):
    @pl.when(f == 0)
    def _():
        acc_ref[...] = jnp.zeros_like(acc_ref)
    xb = x_ref[...].astype(BF16)
    g = _dot(xb, wg_ref[...])
    u = _dot(xb, wu_ref[...])
    a = (g * jax.nn.sigmoid(g) * u).astype(BF16)
    acc_ref[...] += _dot(a, wd_ref[...])


def _ffn_dense_kernel(x_ref, wg_ref, wu_ref, wd_ref, xm_ref, gate_ref, n3_ref, *rest, nf, with_next):
    if with_next:
        n0_ref, sc_ref, sh_ref, xo_ref, hn_ref, acc_ref = rest
    else:
        xo_ref, acc_ref = rest
    f = pl.program_id(1)
    _swiglu_step(x_ref, wg_ref, wu_ref, wd_ref, acc_ref, f)

    @pl.when(f == nf - 1)
    def _():
        xo = xm_ref[...] + gate_ref[...] * _rms(acc_ref[...], n3_ref[...])
        xo_ref[...] = xo
        if with_next:
            hn = _rms(xo, n0_ref[...]) * (1.0 + sc_ref[...]) + sh_ref[...]
            hn_ref[...] = hn.astype(hn_ref.dtype)


def _ffn_dense(h2, wg, wu, wd, xm, n3, mod, row_fn, tm, tf, nxt):
    n, d = h2.shape
    fp = wg.shape[1]
    nf = fp // tf
    with_next = nxt is not None
    in_specs = [pl.BlockSpec((tm, d), lambda i, f: (i, 0)),
                pl.BlockSpec((d, tf), lambda i, f: (0, f)),
                pl.BlockSpec((d, tf), lambda i, f: (0, f)),
                pl.BlockSpec((tf, d), lambda i, f: (f, 0)),
                pl.BlockSpec((tm, d), lambda i, f: (i, 0)),
                _mod_spec(d, M_G2, row_fn),
                _vec_spec(d)]
    args = [h2, wg, wu, wd, xm, mod, n3.reshape(1, d)]
    out_specs = [pl.BlockSpec((tm, d), lambda i, f: (i, 0))]
    out_shape = [jax.ShapeDtypeStruct((n, d), F32)]
    if with_next:
        n0, mod_next = nxt
        in_specs += [_vec_spec(d), _mod_spec(d, M_SC1, row_fn), _mod_spec(d, M_SH1, row_fn)]
        args += [n0.reshape(1, d), mod_next, mod_next]
        out_specs.append(pl.BlockSpec((tm, d), lambda i, f: (i, 0)))
        out_shape.append(jax.ShapeDtypeStruct((n, d), BF16))
    kern = functools.partial(_ffn_dense_kernel, nf=nf, with_next=with_next)
    res = pl.pallas_call(
        kern,
        grid=(n // tm, nf),
        in_specs=in_specs,
        out_specs=out_specs,
        out_shape=out_shape,
        scratch_shapes=[pltpu.VMEM((tm, d), F32)],
        compiler_params=_params(("parallel", "arbitrary")),
        name="ffn_dense",
    )(*args)
    return (res[0], res[1]) if with_next else (res[0], None)


def _row_copy(src_hbm, src_row, dst_ref, dst_row, sem):
    return pltpu.make_async_copy(src_hbm.at[pl.ds(src_row, 1)], dst_ref.at[pl.ds(dst_row, 1)], sem)


def _ffn_grouped_kernel(te_ref, nt_ref, src_ref, x_hbm, wg_ref, wu_ref, wd_ref, o_ref,
                        rows_ref, xb_ref, sem, *, nf, tm, grid_tiles):
    i = pl.program_id(0)
    f = pl.program_id(1)
    nt = nt_ref[0]
    slot = i % 2
    rows_per_step = tm // nf

    def start_row(tile, slot_, r):
        _row_copy(x_hbm, src_ref[tile * tm + r], rows_ref.at[slot_], r, sem.at[slot_]).start()

    def wait_rows(slot_):
        pltpu.make_async_copy(x_hbm.at[pl.ds(0, tm)], rows_ref.at[slot_], sem.at[slot_]).wait()

    @pl.when((i == 0) & (f == 0))
    def _():
        def body(r, c):
            start_row(0, 0, r)
            return c
        lax.fori_loop(0, tm, body, 0)

    @pl.when((i < nt) & (f == 0))
    def _():
        wait_rows(slot)
        xb_ref[...] = rows_ref[slot].astype(BF16)
        o_ref[...] = jnp.zeros_like(o_ref)

    @pl.when(i < nt)
    def _():
        nxt = jnp.minimum(i + 1, grid_tiles - 1)
        for j in range(rows_per_step):
            start_row(nxt, 1 - slot, f * rows_per_step + j)
        xb = xb_ref[...]
        g = _dot(xb, wg_ref[...])
        u = _dot(xb, wu_ref[...])
        a = (g * jax.nn.sigmoid(g) * u).astype(BF16)
        o_ref[...] += _dot(a, wd_ref[...])

    @pl.when((i == nt - 1) & (f == nf - 1))
    def _():
        wait_rows(1 - slot)

    @pl.when((i >= nt) & (f == nf - 1))
    def _():
        o_ref[...] = jnp.zeros_like(o_ref)


def _ffn_grouped(h2, src, wg, wu, wd, tile_expert, n_tiles, p_max, tm, tf):
    d = h2.shape[1]
    fe = wg.shape[2]
    nf = fe // tf
    assert tm % nf == 0, (tm, nf)

    def fidx(i, f, nt):
        return jnp.where(i < nt[0], f, nf - 1)

    kern = functools.partial(_ffn_grouped_kernel, nf=nf, tm=tm, grid_tiles=p_max // tm)
    return pl.pallas_call(
        kern,
        grid_spec=pltpu.PrefetchScalarGridSpec(
            num_scalar_prefetch=3,
            grid=(p_max // tm, nf),
            in_specs=[pl.BlockSpec(memory_space=pl.ANY),
                      pl.BlockSpec((None, d, tf), lambda i, f, te, nt, src: (te[i], 0, fidx(i, f, nt))),
                      pl.BlockSpec((None, d, tf), lambda i, f, te, nt, src: (te[i], 0, fidx(i, f, nt))),
                      pl.BlockSpec((None, tf, d), lambda i, f, te, nt, src: (te[i], fidx(i, f, nt), 0))],
            out_specs=pl.BlockSpec((tm, d), lambda i, f, te, nt, src: (i, 0)),
            scratch_shapes=[pltpu.VMEM((2, tm, d), h2.dtype),
                            pltpu.VMEM((tm, d), BF16),
                            pltpu.SemaphoreType.DMA((2,))]),
        out_shape=jax.ShapeDtypeStruct((p_max, d), F32),
        compiler_params=pltpu.CompilerParams(dimension_semantics=("arbitrary", "arbitrary"),
                                             vmem_limit_bytes=VMEM_LIMIT,
                                             disable_bounds_checks=True),
        name="ffn_grouped",
    )(tile_expert, n_tiles, src, h2, wg, wu, wd)


def _router_kernel(h_ref, wr_ref, o_ref, cnt_ref, carry_ref, *, n_experts):
    i = pl.program_id(0)

    @pl.when(i == 0)
    def _():
        carry_ref[...] = jnp.zeros_like(carry_ref)

    logits = _dot(h_ref[...].astype(BF16), wr_ref[...])
    tm = logits.shape[0]
    lane_i = lax.broadcasted_iota(I32, logits.shape, 1)
    lane = lane_i.astype(F32)
    logits = jnp.where(lane_i < n_experts, logits, -jnp.inf)
    m1 = jnp.max(logits, axis=-1, keepdims=True)
    i1 = jnp.min(jnp.where(logits == m1, lane, float(LANES)), axis=-1, keepdims=True)
    rest = jnp.where(lane == i1, -jnp.inf, logits)
    m2 = jnp.max(rest, axis=-1, keepdims=True)
    i2 = jnp.min(jnp.where(rest == m2, lane, float(LANES)), axis=-1, keepdims=True)
    e21 = jnp.exp(m2 - m1)
    w1 = 1.0 / (1.0 + e21)
    w2 = e21 * w1

    sel1 = lane == i1
    sel2 = lane == i2
    onehot = jnp.where(sel1 | sel2, 1.0, 0.0).astype(BF16)
    r = lax.broadcasted_iota(I32, (tm, tm), 0)
    c = lax.broadcasted_iota(I32, (tm, tm), 1)
    strict_lower = jnp.where(c < r, 1.0, 0.0).astype(BF16)
    before = _dot(strict_lower, onehot) + carry_ref[0:1, :]
    r1 = jnp.sum(jnp.where(sel1, before, 0.0), axis=-1, keepdims=True)
    r2 = jnp.sum(jnp.where(sel2, before, 0.0), axis=-1, keepdims=True)
    total = carry_ref[0:1, :] + jnp.sum(onehot.astype(F32), axis=0, keepdims=True)
    carry_ref[...] = jnp.broadcast_to(total, carry_ref.shape)
    cnt_ref[...] = jnp.broadcast_to(total, cnt_ref.shape)

    packed = jnp.where(lane_i == 0, i1, 0.0)
    packed = jnp.where(lane_i == 1, i2, packed)
    packed = jnp.where(lane_i == 2, r1, packed)
    packed = jnp.where(lane_i == 3, r2, packed)
    packed = jnp.where(lane_i == 4, w1, packed)
    packed = jnp.where(lane_i == 5, w2, packed)
    o_ref[...] = packed


def _router(h2, w_router, tm):
    n, d = h2.shape
    n_experts = w_router.shape[1]
    wr = jnp.pad(w_router, ((0, 0), (0, LANES - n_experts))).astype(BF16)
    kern = functools.partial(_router_kernel, n_experts=n_experts)
    return pl.pallas_call(
        kern,
        grid=(n // tm,),
        in_specs=[pl.BlockSpec((tm, d), lambda i: (i, 0)),
                  pl.BlockSpec((d, LANES), lambda i: (0, 0))],
        out_specs=[pl.BlockSpec((tm, LANES), lambda i: (i, 0)),
                   pl.BlockSpec((SUBLANES, LANES), lambda i: (0, 0))],
        out_shape=[jax.ShapeDtypeStruct((n, LANES), F32),
                   jax.ShapeDtypeStruct((SUBLANES, LANES), F32)],
        scratch_shapes=[pltpu.VMEM((SUBLANES, LANES), F32)],
        compiler_params=_params(("arbitrary",)),
        name="router",
    )(h2, wr)


def _combine_kernel(d1_ref, d2_ref, y_hbm, r_ref, xm_ref, gate_ref, n3_ref, xo_ref, buf_ref, sem, *, tm):
    base = pl.program_id(0) * tm

    def issue(r, c):
        _row_copy(y_hbm, d1_ref[base + r], buf_ref.at[0], r, sem).start()
        _row_copy(y_hbm, d2_ref[base + r], buf_ref.at[1], r, sem).start()
        return c
    lax.fori_loop(0, tm, issue, 0)

    for k in range(TOP_K):
        pltpu.make_async_copy(y_hbm.at[pl.ds(0, tm)], buf_ref.at[k], sem).wait()

    route = r_ref[...]
    w1 = route[:, 4:5]
    w2 = route[:, 5:6]
    y = w1 * buf_ref[0] + w2 * buf_ref[1]
    xo_ref[...] = xm_ref[...] + gate_ref[...] * _rms(y, n3_ref[...])


def _combine(yo, dest1, dest2, route, xm, n3, mod, row_fn, tm):
    n, d = xm.shape
    kern = functools.partial(_combine_kernel, tm=tm)
    return pl.pallas_call(
        kern,
        grid_spec=pltpu.PrefetchScalarGridSpec(
            num_scalar_prefetch=2,
            grid=(n // tm,),
            in_specs=[pl.BlockSpec(memory_space=pl.ANY),
                      pl.BlockSpec((tm, LANES), lambda i, *_: (i, 0)),
                      pl.BlockSpec((tm, d), lambda i, *_: (i, 0)),
                      _mod_spec(d, M_G2, row_fn),
                      _vec_spec(d)],
            out_specs=pl.BlockSpec((tm, d), lambda i, *_: (i, 0)),
            scratch_shapes=[pltpu.VMEM((2, tm, d), F32), pltpu.SemaphoreType.DMA(())]),
        out_shape=jax.ShapeDtypeStruct((n, d), F32),
        compiler_params=pltpu.CompilerParams(dimension_semantics=("arbitrary",),
                                             vmem_limit_bytes=VMEM_LIMIT,
                                             disable_bounds_checks=True),
        name="moe_combine",
    )(dest1, dest2, yo, route, xm, mod, n3.reshape(1, d))


def _moe(h2, xm, w_router, wg, wu, wd, n3, mod, row_fn, tm_g, tf):
    n, d = h2.shape
    n_experts = w_router.shape[1]
    route, counts = _router(h2, w_router, min(512, n))
    e1 = route[:, 0].astype(I32)
    e2 = route[:, 1].astype(I32)
    r1 = route[:, 2].astype(I32)
    r2 = route[:, 3].astype(I32)
    cnt = counts[0, :n_experts].astype(I32)
    padded = ((cnt + tm_g - 1) // tm_g) * tm_g
    ends = jnp.cumsum(padded)
    offs = ends - padded
    dest1 = offs[e1] + r1
    dest2 = offs[e2] + r2
    p_max = ((TOP_K * n + n_experts * (tm_g - 1)) // tm_g) * tm_g
    n_rows = ends[-1:]
    tok = jnp.arange(n, dtype=I32)
    src = jnp.zeros((p_max,), I32).at[dest1].set(tok).at[dest2].set(tok)
    tile_start = jnp.arange(p_max // tm_g, dtype=I32) * tm_g
    tile_expert = jnp.minimum(jnp.sum(tile_start[:, None] >= ends[None, :], axis=1),
                              n_experts - 1).astype(I32)
    last_expert = tile_expert[jnp.maximum(n_rows[0] // tm_g - 1, 0)]
    tile_expert = jnp.where(tile_start < n_rows[0], tile_expert, last_expert)
    yo = _ffn_grouped(h2, src, wg, wu, wd, tile_expert, n_rows // tm_g, p_max, tm_g, tf)
    return _combine(yo, dest1, dest2, route, xm, n3, mod, row_fn, min(256, n))


def _rope_tables(seq):
    inv = ROPE_BASE ** (-jnp.arange(ROPE_FREQ, dtype=F32) / ROPE_FREQ)
    t = jnp.arange(seq, dtype=I32)
    pos = jnp.stack([t // GRID_W, t % GRID_W], axis=-1).astype(F32)
    ang = pos[:, :, None] * inv
    cos, sin = jnp.cos(ang), jnp.sin(ang)
    cos = jnp.concatenate([cos, cos], axis=-1).reshape(seq, DIFF_HEAD_DIM)
    sin = jnp.concatenate([-sin, sin], axis=-1).reshape(seq, DIFF_HEAD_DIM)
    reps = LANES // DIFF_HEAD_DIM
    return jnp.tile(cos, (1, reps)), jnp.tile(sin, (1, reps))


def _pad_cols(w, mult):
    pad = (-w.shape[-1]) % mult
    return jnp.pad(w, [(0, 0)] * (w.ndim - 1) + [(0, pad)]) if pad else w


def _pad_rows(w, mult):
    pad = (-w.shape[-2]) % mult
    return jnp.pad(w, [(0, 0)] * (w.ndim - 2) + [(0, pad), (0, 0)]) if pad else w


def kernel(x, c, ctx, c_ctx, norm_g, w_ada, b_ada, w_in, diff_lambda, diff_subln_g, lru_conv_w, lru_conv_b, lru_gate_w, lru_gate_b, lru_lambda, na_rpb, w_branch, w_merge, b_merge, w_out, ffn_w_gate, ffn_w_up, ffn_w_down, moe_w_router, moe_w_gate, moe_w_up, moe_w_down):
    bsz, seq, d = x.shape
    clen = ctx.shape[1]
    depth = w_in.shape[0]
    n, nc = bsz * seq, bsz * clen
    x_lat = x.reshape(n, d)
    x_ctx = ctx.reshape(nc, d)

    mr = -(-(bsz + 1) // SUBLANES) * SUBLANES
    cvec = jnp.concatenate([c, c_ctx[None], jnp.zeros((mr - bsz - 1, d), F32)], axis=0)
    mod_all = _ada(cvec, w_ada, b_ada).reshape(depth, mr, 1, 6 * d)
    cos_t, sin_t = _rope_tables(seq)

    tm_lat, tm_ctx = min(1024, seq), min(1024, nc, seq)
    tm_mix_lat, tm_mix_ctx = min(256, seq), min(256, nc)
    tm_ffn_lat, tm_ffn_ctx = min(512, seq), min(512, nc)
    tf = min(512, d)
    tf_moe = min(MOE_TF, d)
    tq = min(DIFF_TQ, seq)

    def lat_row(tile_rows):
        per_batch = seq // tile_rows
        return lambda i: i // per_batch

    def ctx_row(i):
        return bsz

    h_lat = _prenorm(x_lat, norm_g[0, 0], mod_all[0], lat_row(tm_lat), tm_lat)
    h_ctx = _prenorm(x_ctx, norm_g[0, 0], mod_all[0], ctx_row, tm_ctx)

    rest_names = ("dv", "lx", "lg", "nq", "nk", "nv")
    no_bias = jnp.zeros((1, IN_WIDTH), F32)
    diff_steps = bsz * DIFF_HEADS * (seq // tq)
    moe_src = (moe_w_gate, moe_w_up, moe_w_down)

    def halves(w):
        return w.reshape(2, -1, w.shape[-1])

    def side_cast_ok(mi):
        return moe_w_router.shape[-1] % 2 == 0 and all(
            halves(w[mi]).shape[1] % (16 * diff_steps) == 0 for w in moe_src)

    moe_bf16 = {}
    for l in range(depth):
        need_ctx = l < depth - 1
        mod = mod_all[l]
        lam_init = 0.8 - 0.6 * math.exp(-0.3 * l)
        w_qk = w_in[l][:, :2 * BRANCH_WIDTH].astype(BF16)
        w_rest = w_in[l][:, 2 * BRANCH_WIDTH:].astype(BF16)
        w_gate = w_merge[l].astype(BF16)
        b_gate = b_merge[l][None]

        def project(h, tm, rope):
            qk = _proj(h, w_qk, no_bias, cos_t, sin_t, seq, tm, "rope" if rope else "plain",
                       P_DQ, DIFF_QSCALE)
            rest = _proj(h, w_rest, no_bias, cos_t, sin_t, seq, tm, "plain", P_NQ - 2, NA_QSCALE)
            parts = {"dq": (qk, 0), "dk": (qk, 1)}
            parts.update({name: (rest, k) for k, name in enumerate(rest_names)})
            return parts

        lat = project(h_lat, tm_lat, True)
        ctxp = project(h_ctx, tm_ctx, False)
        gate_lat = _proj(h_lat, w_gate, b_gate, cos_t, sin_t, seq, tm_lat, "gate")

        mi, half = l // 2, l % 2
        jobs = ()
        if mi < moe_w_gate.shape[0] and side_cast_ok(mi):
            jobs = tuple((halves(w[mi]), half, moe_bf16[mi][k] if half else None)
                         for k, w in enumerate(moe_src))
        y_diff, cast = _diff_attn_lat(lat, ctxp, diff_lambda[l], diff_subln_g[l], lam_init,
                                      bsz, seq, clen, tq, jobs)
        if jobs:
            moe_bf16[mi] = cast
        y_lru, y_lru_c = _lru(lat, ctxp, lru_conv_w[l], lru_conv_b[l], lru_gate_w[l],
                              lru_gate_b[l], lru_lambda[l], bsz, seq, clen, need_ctx)
        y_na = _na_lat(lat, ctxp, na_rpb[l], bsz, seq, clen)

        wb = w_branch[l].astype(BF16)
        wo = w_out[l].astype(BF16)
        is_moe = l % 2 == 1
        h2_dtype = F32 if is_moe else BF16
        xm_lat, h2_lat = _merge(y_diff, y_lru, y_na, (gate_lat, 0), wb, wo, x_lat, norm_g[l, 1],
                                norm_g[l, 2], mod, lat_row(tm_mix_lat), tm_mix_lat, h2_dtype)
        if need_ctx:
            y_diff_c = _diff_attn_ctx(ctxp, diff_lambda[l], diff_subln_g[l], lam_init, bsz, clen)
            y_na_c = _na_ctx(ctxp, bsz, clen)
            gate_ctx = _proj(h_ctx, w_gate, b_gate, cos_t, sin_t, seq, tm_ctx, "gate")
            xm_ctx, h2_ctx = _merge(y_diff_c, y_lru_c, y_na_c, (gate_ctx, 0), wb, wo, x_ctx,
                                    norm_g[l, 1], norm_g[l, 2], mod, ctx_row, tm_mix_ctx, h2_dtype)

        nxt = (norm_g[l + 1, 0], mod_all[l + 1]) if need_ctx else None
        i = l // 2
        if is_moe:
            if i in moe_bf16:
                wg, wu, wd = (c.reshape(w.shape[1:]) for c, w in zip(moe_bf16[i], moe_src))
            else:
                wg, wu, wd = (w[i].astype(BF16) for w in moe_src)
            tm_g = min(MOE_TILE, n)
            x_lat = _moe(h2_lat, xm_lat, moe_w_router[i], wg, wu, wd, norm_g[l, 3], mod,
                         lat_row(min(256, n)), tm_g, tf_moe)
            if need_ctx:
                x_ctx = _moe(h2_ctx, xm_ctx, moe_w_router[i], wg, wu, wd, norm_g[l, 3], mod,
                             ctx_row, min(MOE_TILE, nc), tf_moe)
                h_lat = _prenorm(x_lat, nxt[0], nxt[1], lat_row(tm_lat), tm_lat)
                h_ctx = _prenorm(x_ctx, nxt[0], nxt[1], ctx_row, tm_ctx)
        else:
            wg = _pad_cols(ffn_w_gate[i], tf).astype(BF16)
            wu = _pad_cols(ffn_w_up[i], tf).astype(BF16)
            wd = _pad_rows(ffn_w_down[i], tf).astype(BF16)
            x_lat, h_lat = _ffn_dense(h2_lat, wg, wu, wd, xm_lat, norm_g[l, 3], mod,
                                      lat_row(tm_ffn_lat), tm_ffn_lat, tf, nxt)
            if need_ctx:
                x_ctx, h_ctx = _ffn_dense(h2_ctx, wg, wu, wd, xm_ctx, norm_g[l, 3], mod,
                                          ctx_row, tm_ffn_ctx, tf, nxt)
    return x_lat.reshape(bsz, seq, d)
```

```python
import functools
import math

import numpy as np
import jax
import jax.numpy as jnp
from jax import lax
from jax.experimental import pallas as pl
from jax.experimental.pallas import tpu as pltpu

F32 = jnp.float32
BF16 = jnp.bfloat16
I32 = jnp.int32

EPS = 1e-6
GRID_W = 64
BRANCH_WIDTH = 1024
N_IN_PARTS = 8
IN_WIDTH = N_IN_PARTS * BRANCH_WIDTH
DIFF_HEADS = 8
DIFF_HEAD_DIM = 64
DIFF_V_DIM = 2 * DIFF_HEAD_DIM
ROPE_BASE = 10000.0
ROPE_FREQ = DIFF_HEAD_DIM // 4
LRU_BLOCKS = 8
LRU_BLOCK_DIM = BRANCH_WIDTH // LRU_BLOCKS
LRU_CONV_W = 4
LRU_C = 8.0
NA_HEADS = 8
NA_HEAD_DIM = BRANCH_WIDTH // NA_HEADS
NA_WIN_R = 8
NA_WIN_C = 16
TOP_K = 2
MOE_TILE = 560
MOE_TF = 1024
DIFF_TQ = 1024
PROJ_TN = 2048
PROJ_TN_ROPE = 1024

LANES = 128
SUBLANES = 8
VMEM_LIMIT = 56 * 1024 * 1024
NEG = -1e30
LOG2E = 1.0 / math.log(2.0)
DIFF_QSCALE = DIFF_HEAD_DIM ** -0.5 * LOG2E
NA_QSCALE = NA_HEAD_DIM ** -0.5 * LOG2E

P_DQ, P_DK, P_DV, P_LX, P_LG, P_NQ, P_NK, P_NV = range(8)
M_SH1, M_SC1, M_G1, M_SH2, M_SC2, M_G2 = range(6)


def _params(sem):
    return pltpu.CompilerParams(dimension_semantics=sem, vmem_limit_bytes=VMEM_LIMIT)


def _rms(x, g):
    return x * lax.rsqrt(jnp.mean(x * x, axis=-1, keepdims=True) + EPS) * g


def _dot(a, b):
    return jnp.dot(a, b, preferred_element_type=F32)


def _dot_nt(a, b):
    return lax.dot_general(a, b, (((1,), (1,)), ((), ())), preferred_element_type=F32)


def _ada_kernel(c_ref, w_ref, b_ref, o_ref):
    c = c_ref[...]
    s = (c * jax.nn.sigmoid(c)).astype(BF16)
    o_ref[...] = _dot(s, w_ref[...].astype(BF16)) + b_ref[...]


def _ada(cvec, w_ada, b_ada):
    n_layers, d, n6 = w_ada.shape
    mr = cvec.shape[0]
    tn = min(1024, d)
    return pl.pallas_call(
        _ada_kernel,
        grid=(n_layers, n6 // tn),
        in_specs=[pl.BlockSpec((mr, d), lambda l, j: (0, 0)),
                  pl.BlockSpec((None, d, tn), lambda l, j: (l, 0, j)),
                  pl.BlockSpec((None, 1, tn), lambda l, j: (l, 0, j))],
        out_specs=pl.BlockSpec((None, mr, tn), lambda l, j: (l, 0, j)),
        out_shape=jax.ShapeDtypeStruct((n_layers, mr, n6), F32),
        compiler_params=_params(("arbitrary", "arbitrary")),
        name="ada_mod",
    )(cvec, w_ada, b_ada.reshape(n_layers, 1, n6))


def _mod_spec(d, chunk, row_fn):
    return pl.BlockSpec((None, 1, d), lambda i, *_: (row_fn(i), 0, chunk))


def _vec_spec(d):
    return pl.BlockSpec((1, d), lambda i, *_: (0, 0))


def _prenorm_kernel(x_ref, g_ref, sc_ref, sh_ref, o_ref):
    y = _rms(x_ref[...], g_ref[...])
    o_ref[...] = (y * (1.0 + sc_ref[...]) + sh_ref[...]).astype(o_ref.dtype)


def _prenorm(x, g, mod, row_fn, tm):
    n, d = x.shape
    return pl.pallas_call(
        _prenorm_kernel,
        grid=(n // tm,),
        in_specs=[pl.BlockSpec((tm, d), lambda i: (i, 0)),
                  _vec_spec(d),
                  _mod_spec(d, M_SC1, row_fn),
                  _mod_spec(d, M_SH1, row_fn)],
        out_specs=pl.BlockSpec((tm, d), lambda i: (i, 0)),
        out_shape=jax.ShapeDtypeStruct((n, d), BF16),
        compiler_params=_params(("parallel",)),
        name="prenorm",
    )(x, g.reshape(1, d), mod, mod)


def _proj_kernel(h_ref, w_ref, v_ref, cos_ref, sin_ref, o_ref, *, kind):
    acc = _dot(h_ref[...], w_ref[...])
    tn = acc.shape[1]
    if kind == "gate":
        o_ref[...] = jax.nn.sigmoid(acc + v_ref[...]).astype(o_ref.dtype)
    elif kind == "rope":
        lane = lax.broadcasted_iota(I32, acc.shape, 1)
        first_half = (lane % (2 * ROPE_FREQ)) < ROPE_FREQ
        partner = jnp.where(first_half, pltpu.roll(acc, tn - ROPE_FREQ, 1),
                            pltpu.roll(acc, ROPE_FREQ, 1))
        cos = jnp.tile(cos_ref[...], (1, tn // LANES)) * v_ref[...]
        sin = jnp.tile(sin_ref[...], (1, tn // LANES)) * v_ref[...]
        o_ref[...] = (acc * cos + partner * sin).astype(o_ref.dtype)
    else:
        o_ref[...] = (acc * v_ref[...]).astype(o_ref.dtype)


def _proj(h, w, v, cos_t, sin_t, seq, tm, tn, kind):
    n, d = h.shape
    nw = w.shape[1]
    tn = min(tn, nw)
    tpb = seq // tm
    kern = functools.partial(_proj_kernel, kind=kind)
    return pl.pallas_call(
        kern,
        grid=(n // tm, nw // tn),
        in_specs=[pl.BlockSpec((tm, d), lambda i, j: (i, 0)),
                  pl.BlockSpec((d, tn), lambda i, j: (0, j)),
                  pl.BlockSpec((1, tn), lambda i, j: (0, j)),
                  pl.BlockSpec((tm, LANES), lambda i, j: (i % tpb, 0)),
                  pl.BlockSpec((tm, LANES), lambda i, j: (i % tpb, 0))],
        out_specs=pl.BlockSpec((tm, tn), lambda i, j: (i, j)),
        out_shape=jax.ShapeDtypeStruct((n, nw), BF16),
        compiler_params=_params(("parallel", "arbitrary")),
        name="proj_" + kind,
    )(h, w, v, cos_t, sin_t)


DIFF_KEY_CHUNK = 512


def _diff_kernel(*refs, lam_init, has_lat, n_cast, n_alias=0):
    n_in = 7 if has_lat else 5
    n_extra = n_cast + n_alias
    for src_ref, dst_ref in zip(refs[n_in:n_in + n_cast], refs[n_in + n_extra + 1:]):
        dst_ref[0] = src_ref[...].astype(dst_ref.dtype)
        if dst_ref.shape[0] == 2:
            dst_ref[1] = jnp.zeros(dst_ref.shape[1:], dst_ref.dtype)
    if has_lat:
        lamv_ref, sg_ref, q_ref, k_ref, v_ref, kc_ref, vc_ref = refs[:n_in]
    else:
        lamv_ref, sg_ref, q_ref, kc_ref, vc_ref = refs[:n_in]
    o_ref = refs[n_in + n_extra]
    lv = lamv_ref[...]
    lam = (jnp.exp(jnp.sum(lv[0:1] * lv[1:2], axis=-1, keepdims=True))
           - jnp.exp(jnp.sum(lv[2:3] * lv[3:4], axis=-1, keepdims=True)) + lam_init)
    q = q_ref[...]
    first = lax.broadcasted_iota(I32, q.shape, 1) < DIFF_HEAD_DIM
    zero = jnp.zeros_like(q)
    qz = (jnp.where(first, q, zero), jnp.where(first, zero, q))
    chunks = [(kc_ref, vc_ref, 0, kc_ref.shape[0])]
    if has_lat:
        kt = min(DIFF_KEY_CHUNK, k_ref.shape[0])
        chunks += [(k_ref, v_ref, j * kt, kt) for j in range(k_ref.shape[0] // kt)]

    def qk(c, chunk):
        k, _, off, n = chunk
        return _dot_nt(qz[c], k[off:off + n, :])

    def row_max(parts):
        m = jnp.max(parts[0], axis=-1, keepdims=True)
        for p in parts[1:]:
            m = jnp.maximum(m, jnp.max(p, axis=-1, keepdims=True))
        return m

    def softmax_av(s_parts, between=None):
        m = row_max(s_parts)
        den, out = 0.0, 0.0
        for j, (_, v, off, n) in enumerate(chunks):
            if between is not None:
                between(j)
            e = jnp.exp2(s_parts[j] - m)
            den = den + jnp.sum(e, axis=-1, keepdims=True)
            out = out + _dot(e.astype(BF16), v[off:off + n, :])
        return out, den

    s1 = [qk(0, ch) for ch in chunks]
    s2 = []
    o1, den1 = softmax_av(s1, between=lambda j: s2.append(qk(1, chunks[j])))
    o2, den2 = softmax_av(s2)
    o = o1 * (1.0 / den1) - o2 * (lam / den2)
    y = _rms(o, sg_ref[...]) * (1.0 - lam_init)
    o_ref[...] = y.astype(o_ref.dtype)


def _diff_attn_lat(lat, ctxp, lamv, subln_g, lam_init, bsz, seq, clen, tq, cast_jobs=()):
    nq = seq // tq
    n_steps = bsz * DIFF_HEADS * nq
    hw = DIFF_V_DIM
    hpb = BRANCH_WIDTH // hw

    def spec(rows, src, row_fn):
        blk = src[1] * hpb
        return pl.BlockSpec((rows, hw), lambda b, h, i: (row_fn(b, i), blk + h))

    def step(b, h, i):
        return (b * DIFF_HEADS + h) * nq + i

    n_base = 7
    cast_in, cast_out, cast_shapes, cast_args, aliases = [], [], [], [], {}
    for k, (arr, half, dst) in enumerate(cast_jobs):
        _, rows, cols = arr.shape
        per_step = rows // n_steps
        assert per_step * n_steps == rows and per_step % 16 == 0
        cast_in.append(pl.BlockSpec((None, per_step, cols), lambda b, h, i, half=half: (half, step(b, h, i), 0)))
        cast_args.append(arr)
        cast_shapes.append(jax.ShapeDtypeStruct(arr.shape, BF16))
        if dst is None:
            cast_out.append(pl.BlockSpec((2, per_step, cols), lambda b, h, i: (0, step(b, h, i), 0)))
        else:
            assert half == 1
            cast_out.append(pl.BlockSpec((1, per_step, cols), lambda b, h, i: (1, step(b, h, i), 0)))
    for k, (arr, half, dst) in enumerate(cast_jobs):
        if dst is not None:
            aliases[n_base + len(cast_jobs) + len(aliases)] = 1 + k
            cast_in.append(pl.BlockSpec(memory_space=pl.ANY))
            cast_args.append(dst)
    n_alias = len(aliases)
    kern = functools.partial(_diff_kernel, lam_init=lam_init, has_lat=True, n_cast=len(cast_jobs),
                             n_alias=n_alias)
    res = pl.pallas_call(
        kern,
        grid=(bsz, DIFF_HEADS, nq),
        in_specs=[pl.BlockSpec((4, DIFF_HEAD_DIM), lambda b, h, i: (0, 0)),
                  pl.BlockSpec((1, hw), lambda b, h, i: (0, 0)),
                  spec(tq, lat["dq"], lambda b, i: b * nq + i),
                  spec(seq, lat["dk"], lambda b, i: b),
                  spec(seq, lat["dv"], lambda b, i: b),
                  spec(clen, ctxp["dk"], lambda b, i: b),
                  spec(clen, ctxp["dv"], lambda b, i: b)] + cast_in,
        out_specs=[pl.BlockSpec((tq, hw), lambda b, h, i: (b * nq + i, h))] + cast_out,
        out_shape=[jax.ShapeDtypeStruct((bsz * seq, BRANCH_WIDTH), BF16)] + cast_shapes,
        input_output_aliases=aliases,
        compiler_params=_params(("parallel", "parallel", "arbitrary")),
        name="diff_attn",
    )(lamv, subln_g.reshape(1, hw), lat["dq"][0], lat["dk"][0], lat["dv"][0],
      ctxp["dk"][0], ctxp["dv"][0], *cast_args)
    return res[0], res[1:]


def _diff_attn_ctx(ctxp, lamv, subln_g, lam_init, bsz, clen):
    hw = DIFF_V_DIM
    hpb = BRANCH_WIDTH // hw

    def spec(src):
        blk = src[1] * hpb
        return pl.BlockSpec((clen, hw), lambda b, h: (b, blk + h))

    kern = functools.partial(_diff_kernel, lam_init=lam_init, has_lat=False, n_cast=0)
    return pl.pallas_call(
        kern,
        grid=(bsz, DIFF_HEADS),
        in_specs=[pl.BlockSpec((4, DIFF_HEAD_DIM), lambda b, h: (0, 0)),
                  pl.BlockSpec((1, hw), lambda b, h: (0, 0)),
                  spec(ctxp["dq"]), spec(ctxp["dk"]), spec(ctxp["dv"])],
        out_specs=pl.BlockSpec((clen, hw), lambda b, h: (b, h)),
        out_shape=jax.ShapeDtypeStruct((bsz * clen, BRANCH_WIDTH), BF16),
        compiler_params=_params(("parallel", "parallel")),
        name="diff_attn_ctx",
    )(lamv, subln_g.reshape(1, hw), ctxp["dq"][0], ctxp["dk"][0], ctxp["dv"][0])


LRU_TW = 256
LRU_UNROLL = 4


def _gelu_tanh(x):
    return 0.5 * x * (1.0 + jnp.tanh(math.sqrt(2.0 / math.pi) * (x + 0.044715 * (x * x * x))))


def _tile_scan(a, b, reverse):
    rows = lax.broadcasted_iota(I32, a.shape, 0)
    for s in (1, 2, 4):
        if reverse:
            a_s = pltpu.roll(a, SUBLANES - s, 0)
            b_s = pltpu.roll(b, SUBLANES - s, 0)
            valid = rows < SUBLANES - s
        else:
            a_s = pltpu.roll(a, s, 0)
            b_s = pltpu.roll(b, s, 0)
            valid = rows >= s
        b = jnp.where(valid, a * b_s + b, b)
        a = jnp.where(valid, a * a_s, a)
    return a, b


def _lru_kernel(xl_ref, gl_ref, xc_ref, gc_ref, cw_ref, cb_ref, gw_ref, gb_ref, lam_ref,
                *out_and_scratch, seq, clen, need_ctx):
    if need_ctx:
        yl_ref, yc_ref, af_ref, bf_ref, ab_ref, bb_ref, hf_ref = out_and_scratch
    else:
        yl_ref, af_ref, bf_ref, ab_ref, bb_ref, hf_ref = out_and_scratch
        yc_ref = None
    tot = clen + seq
    x = jnp.concatenate([xc_ref[...], xl_ref[...]], axis=0).astype(F32)
    row = lax.broadcasted_iota(I32, x.shape, 0)
    pos = jnp.where(row < clen, row, row - clen)
    seg_len = jnp.where(row < clen, clen, seq)
    cw = cw_ref[...]
    conv = cb_ref[...] + x * cw[2:3]
    conv = conv + jnp.where(pos >= 2, pltpu.roll(x, 2, 0), 0.0) * cw[0:1]
    conv = conv + jnp.where(pos >= 1, pltpu.roll(x, 1, 0), 0.0) * cw[1:2]
    conv = conv + jnp.where(pos < seg_len - 1, pltpu.roll(x, tot - 1, 0), 0.0) * cw[3:4]
    xb = conv.astype(BF16)
    gb = gb_ref[...]
    lam = lam_ref[...]
    for d, (a_ref, b_ref) in enumerate(((af_ref, bf_ref), (ab_ref, bb_ref))):
        r = jax.nn.sigmoid(_dot(xb, gw_ref[d, 0]) + gb[2 * d:2 * d + 1])
        i = jax.nn.sigmoid(_dot(xb, gw_ref[d, 1]) + gb[2 * d + 1:2 * d + 2])
        z = -lam[d:d + 1]
        softplus = jnp.maximum(z, 0.0) + jnp.log1p(jnp.exp(-jnp.abs(z)))
        log_a = -LRU_C * r * softplus
        a = jnp.exp(log_a)
        beta = jnp.sqrt(-jnp.tanh(log_a) * (a * a + 1.0))
        a_ref[...] = a
        b_ref[...] = beta * i * conv

    n_ct = clen // SUBLANES
    n_t = tot // SUBLANES
    w = x.shape[1]
    assert n_ct % LRU_UNROLL == 0 and n_t % LRU_UNROLL == 0
    rows_per_iter = LRU_UNROLL * SUBLANES

    def fwd_step(t, carry):
        base = pl.multiple_of(t * rows_per_iter, rows_per_iter)
        scans = [_tile_scan(af_ref[pl.ds(base + k * SUBLANES, SUBLANES), :],
                            bf_ref[pl.ds(base + k * SUBLANES, SUBLANES), :], False)
                 for k in range(LRU_UNROLL)]
        for k, (a, b) in enumerate(scans):
            h = a * carry + b
            hf_ref[pl.ds(base + k * SUBLANES, SUBLANES), :] = h
            carry = h[SUBLANES - 1:SUBLANES, :]
        return carry

    lax.fori_loop(0, n_t // LRU_UNROLL, fwd_step, jnp.zeros((1, w), F32))

    def bwd_iter(first_tile, n_tiles, g_ref, y_ref, seg_off):
        def step(t, carry):
            base = pl.multiple_of((first_tile + n_tiles) * SUBLANES - (t + 1) * rows_per_iter,
                                  rows_per_iter)
            offs = [base + k * SUBLANES for k in reversed(range(LRU_UNROLL))]
            scans = [_tile_scan(ab_ref[pl.ds(off, SUBLANES), :], bb_ref[pl.ds(off, SUBLANES), :], True)
                     for off in offs]
            for off, (a, b) in zip(offs, scans):
                h = a * carry + b
                carry = h[0:1, :]
                if y_ref is not None:
                    g = g_ref[pl.ds(off - seg_off, SUBLANES), :].astype(F32)
                    y_ref[pl.ds(off - seg_off, SUBLANES), :] = (
                        (hf_ref[pl.ds(off, SUBLANES), :] + h) * _gelu_tanh(g)).astype(y_ref.dtype)
            return carry
        return step

    carry = lax.fori_loop(0, n_ct // LRU_UNROLL, bwd_iter(0, n_ct, gc_ref, yc_ref, 0),
                          jnp.zeros((1, w), F32))
    lax.fori_loop(0, (n_t - n_ct) // LRU_UNROLL, bwd_iter(n_ct, n_t - n_ct, gl_ref, yl_ref, clen), carry)


def _lru(lat, ctxp, conv_w, conv_b, gate_w, gate_b, lru_lam, bsz, seq, clen, need_ctx):
    n = bsz * seq
    tw = LRU_TW
    nwt = BRANCH_WIDTH // tw
    per = tw // LRU_BLOCK_DIM
    gw = gate_w.reshape(2, 2, nwt, per, LRU_BLOCK_DIM, LRU_BLOCK_DIM)
    eye = jnp.eye(per, dtype=gate_w.dtype)
    gw = jnp.einsum('dgnpcf,pq->dgnpcqf', gw, eye).reshape(2, 2, nwt, tw, tw).astype(BF16)
    gb = gate_b.reshape(4, BRANCH_WIDTH)
    tot = seq + clen
    kern = functools.partial(_lru_kernel, seq=seq, clen=clen, need_ctx=need_ctx)

    def part_spec(rows, src):
        blk = src[1] * nwt
        return pl.BlockSpec((rows, tw), lambda b, j: (b, blk + j))

    out_shape = [jax.ShapeDtypeStruct((n, BRANCH_WIDTH), BF16)]
    out_specs = [pl.BlockSpec((seq, tw), lambda b, j: (b, j))]
    if need_ctx:
        out_shape.append(jax.ShapeDtypeStruct((bsz * clen, BRANCH_WIDTH), BF16))
        out_specs.append(pl.BlockSpec((clen, tw), lambda b, j: (b, j)))
    res = pl.pallas_call(
        kern,
        grid=(bsz, nwt),
        in_specs=[part_spec(seq, lat["lx"]), part_spec(seq, lat["lg"]),
                  part_spec(clen, ctxp["lx"]), part_spec(clen, ctxp["lg"]),
                  pl.BlockSpec((LRU_CONV_W, tw), lambda b, j: (0, j)),
                  pl.BlockSpec((1, tw), lambda b, j: (0, j)),
                  pl.BlockSpec((2, 2, None, tw, tw), lambda b, j: (0, 0, j, 0, 0)),
                  pl.BlockSpec((4, tw), lambda b, j: (0, j)),
                  pl.BlockSpec((2, tw), lambda b, j: (0, j))],
        out_specs=out_specs,
        out_shape=out_shape,
        scratch_shapes=[pltpu.VMEM((tot, tw), F32) for _ in range(5)],
        compiler_params=_params(("parallel", "parallel")),
        name="rglru",
    )(lat["lx"][0], lat["lg"][0], ctxp["lx"][0], ctxp["lg"][0], conv_w,
      conv_b.reshape(1, BRANCH_WIDTH), gw, gb, lru_lam)
    return (res[0], res[1]) if need_ctx else (res[0], None)


NA_ROWS_PER_TILE = 4


def _na_tables(rows, rpt):
    kr = min(NA_WIN_R, rows)
    nkr = min(rows, rpt + kr - 1)
    variants, index, tile_variant, tile_kb = [], {}, [], []
    for t in range(rows // rpt):
        r0 = t * rpt
        kb = int(np.clip(r0 - kr // 2, 0, rows - nkr))
        geom = []
        for a in range(rpt):
            r = r0 + a
            rs = int(np.clip(r - kr // 2, 0, rows - kr))
            geom.append(tuple((kb + k - r + NA_WIN_R - 1) if rs <= kb + k < rs + kr else None
                              for k in range(nkr)))
        geom = tuple(geom)
        if geom not in index:
            index[geom] = len(variants)
            variants.append(geom)
        tile_variant.append(index[geom])
        tile_kb.append(kb)
    return nkr, np.array(tile_variant, np.int32), np.array(tile_kb, np.int32), variants


def _na_bias(rpb, variants):
    heads = rpb.shape[0]
    pad = GRID_W - NA_WIN_C
    padded = jnp.pad(rpb * LOG2E, ((0, 0), (0, 0), (pad, pad)))
    toep = jnp.stack([padded[..., GRID_W - 1 - qc:2 * GRID_W - 1 - qc] for qc in range(GRID_W)],
                     axis=2)
    col = np.arange(GRID_W)
    cs = np.clip(col - NA_WIN_C // 2, 0, GRID_W - NA_WIN_C)
    in_cols = (col[None, :] >= cs[:, None]) & (col[None, :] < cs[:, None] + NA_WIN_C)
    toep = jnp.where(in_cols, toep, NEG).astype(F32)
    outside = jnp.full((heads, GRID_W, GRID_W), NEG, F32)
    return jnp.stack([
        jnp.concatenate([
            jnp.concatenate([outside if dr is None else toep[:, dr] for dr in row], axis=-1)
            for row in geom], axis=-2)
        for geom in variants])


def _na_kernel(var_ref, kb_ref, q_ref, k_ref, v_ref, kc_ref, vc_ref, bias_ref, o_ref, *, nk):
    t = pl.program_id(1)
    start = pl.multiple_of(kb_ref[t] * GRID_W, GRID_W)
    for h in range(NA_HEADS):
        sl = slice(h * NA_HEAD_DIM, (h + 1) * NA_HEAD_DIM)
        qh = q_ref[:, sl]
        s_w = _dot_nt(qh, k_ref[pl.ds(start, nk), sl]) + bias_ref[h]
        s_c = _dot_nt(qh, kc_ref[:, sl])
        m = jnp.maximum(jnp.max(s_w, axis=-1, keepdims=True), jnp.max(s_c, axis=-1, keepdims=True))
        e_w = jnp.exp2(s_w - m)
        e_c = jnp.exp2(s_c - m)
        inv = 1.0 / (jnp.sum(e_w, axis=-1, keepdims=True) + jnp.sum(e_c, axis=-1, keepdims=True))
        o = (_dot(e_w.astype(BF16), v_ref[pl.ds(start, nk), sl])
             + _dot(e_c.astype(BF16), vc_ref[:, sl]))
        o_ref[:, sl] = (o * inv).astype(o_ref.dtype)


def _na_lat(lat, ctxp, rpb, bsz, seq, clen):
    n = bsz * seq
    rows = seq // GRID_W
    rpt = min(NA_ROWS_PER_TILE, rows)
    nkr, tile_variant, tile_kb, variants = _na_tables(rows, rpt)
    bias = _na_bias(rpb, variants)
    nt = rows // rpt
    rq, nk = rpt * GRID_W, nkr * GRID_W
    bw = BRANCH_WIDTH
    kern = functools.partial(_na_kernel, nk=nk)
    return pl.pallas_call(
        kern,
        grid_spec=pltpu.PrefetchScalarGridSpec(
            num_scalar_prefetch=2,
            grid=(bsz, nt),
            in_specs=[pl.BlockSpec((rq, bw), lambda b, t, var, kb: (b * nt + t, lat["nq"][1])),
                      pl.BlockSpec((seq, bw), lambda b, t, var, kb: (b, lat["nk"][1])),
                      pl.BlockSpec((seq, bw), lambda b, t, var, kb: (b, lat["nv"][1])),
                      pl.BlockSpec((clen, bw), lambda b, t, var, kb: (b, ctxp["nk"][1])),
                      pl.BlockSpec((clen, bw), lambda b, t, var, kb: (b, ctxp["nv"][1])),
                      pl.BlockSpec((None, NA_HEADS, rq, nk), lambda b, t, var, kb: (var[t], 0, 0, 0))],
            out_specs=pl.BlockSpec((rq, bw), lambda b, t, var, kb: (b * nt + t, 0))),
        out_shape=jax.ShapeDtypeStruct((n, bw), BF16),
        compiler_params=_params(("parallel", "arbitrary")),
        name="na_attn",
    )(jnp.asarray(tile_variant), jnp.asarray(tile_kb), lat["nq"][0], lat["nk"][0], lat["nv"][0],
      ctxp["nk"][0], ctxp["nv"][0], bias)


def _na_ctx_kernel(q_ref, k_ref, v_ref, o_ref):
    for h in range(NA_HEADS):
        sl = slice(h * NA_HEAD_DIM, (h + 1) * NA_HEAD_DIM)
        s = _dot_nt(q_ref[:, sl], k_ref[:, sl])
        e = jnp.exp2(s - jnp.max(s, axis=-1, keepdims=True))
        inv = 1.0 / jnp.sum(e, axis=-1, keepdims=True)
        o_ref[:, sl] = (_dot(e.astype(BF16), v_ref[:, sl]) * inv).astype(o_ref.dtype)


def _na_ctx(ctxp, bsz, clen):
    bw = BRANCH_WIDTH
    return pl.pallas_call(
        _na_ctx_kernel,
        grid=(bsz,),
        in_specs=[pl.BlockSpec((clen, bw), lambda b: (b, ctxp["nq"][1])),
                  pl.BlockSpec((clen, bw), lambda b: (b, ctxp["nk"][1])),
                  pl.BlockSpec((clen, bw), lambda b: (b, ctxp["nv"][1]))],
        out_specs=pl.BlockSpec((clen, bw), lambda b: (b, 0)),
        out_shape=jax.ShapeDtypeStruct((bsz * clen, bw), BF16),
        compiler_params=_params(("parallel",)),
        name="na_attn_ctx",
    )(ctxp["nq"][0], ctxp["nk"][0], ctxp["nv"][0])


def _merge_kernel(yd_ref, yl_ref, yn_ref, g0_ref, g1_ref, g2_ref, wb_ref, wo_ref, x_ref,
                  n1_ref, gate_ref, n2_ref, sc_ref, sh_ref, xm_ref, h2_ref):
    m = (g0_ref[...].astype(F32) * _dot(yd_ref[...], wb_ref[0])
         + g1_ref[...].astype(F32) * _dot(yl_ref[...], wb_ref[1])
         + g2_ref[...].astype(F32) * _dot(yn_ref[...], wb_ref[2]))
    mo = _dot(m.astype(BF16), wo_ref[...])
    xm = x_ref[...] + gate_ref[...] * _rms(mo, n1_ref[...])
    xm_ref[...] = xm
    h2 = _rms(xm, n2_ref[...]) * (1.0 + sc_ref[...]) + sh_ref[...]
    h2_ref[...] = h2.astype(h2_ref.dtype)


def _merge(yd, yl, yn, gates, wb, wo, x, n1, n2, mod, row_fn, tm, h2_dtype):
    n, d = x.shape
    bw = BRANCH_WIDTH
    p, gcol = gates
    resident = dict(pipeline_mode=pl.Buffered(1))
    return pl.pallas_call(
        _merge_kernel,
        grid=(n // tm,),
        in_specs=[pl.BlockSpec((tm, bw), lambda i: (i, 0)),
                  pl.BlockSpec((tm, bw), lambda i: (i, 0)),
                  pl.BlockSpec((tm, bw), lambda i: (i, 0)),
                  pl.BlockSpec((tm, d), lambda i: (i, gcol)),
                  pl.BlockSpec((tm, d), lambda i: (i, gcol + 1)),
                  pl.BlockSpec((tm, d), lambda i: (i, gcol + 2)),
                  pl.BlockSpec((3, bw, d), lambda i: (0, 0, 0), **resident),
                  pl.BlockSpec((d, d), lambda i: (0, 0), **resident),
                  pl.BlockSpec((tm, d), lambda i: (i, 0)),
                  _vec_spec(d),
                  _mod_spec(d, M_G1, row_fn),
                  _vec_spec(d),
                  _mod_spec(d, M_SC2, row_fn),
                  _mod_spec(d, M_SH2, row_fn)],
        out_specs=[pl.BlockSpec((tm, d), lambda i: (i, 0)),
                   pl.BlockSpec((tm, d), lambda i: (i, 0))],
        out_shape=[jax.ShapeDtypeStruct((n, d), F32),
                   jax.ShapeDtypeStruct((n, d), h2_dtype)],
        compiler_params=_params(("parallel",)),
        name="merge",
    )(yd, yl, yn, p, p, p, wb, wo, x, n1.reshape(1, d), mod, n2.reshape(1, d), mod, mod)


def _swiglu_step(x_ref, wg_ref, wu_ref, wd_ref, acc_ref, f):
    @pl.when(f == 0)
    def _():
        acc_ref[...] = jnp.zeros_like(acc_ref)
    xb = x_ref[...].astype(BF16)
    g = _dot(xb, wg_ref[...])
    u = _dot(xb, wu_ref[...])
    a = (g * jax.nn.sigmoid(g) * u).astype(BF16)
    acc_ref[...] += _dot(a, wd_ref[...])


def _ffn_dense_kernel(x_ref, wg_ref, wu_ref, wd_ref, xm_ref, gate_ref, n3_ref, *rest, nf, with_next):
    if with_next:
        n0_ref, sc_ref, sh_ref, xo_ref, hn_ref, acc_ref = rest
    else:
        xo_ref, acc_ref = rest
    f = pl.program_id(1)
    _swiglu_step(x_ref, wg_ref, wu_ref, wd_ref, acc_ref, f)

    @pl.when(f == nf - 1)
    def _():
        xo = xm_ref[...] + gate_ref[...] * _rms(acc_ref[...], n3_ref[...])
        xo_ref[...] = xo
        if with_next:
            hn = _rms(xo, n0_ref[...]) * (1.0 + sc_ref[...]) + sh_ref[...]
            hn_ref[...] = hn.astype(hn_ref.dtype)


def _ffn_dense(h2, wg, wu, wd, xm, n3, mod, row_fn, tm, tf, nxt):
    n, d = h2.shape
    fp = wg.shape[1]
    nf = fp // tf
    with_next = nxt is not None
    in_specs = [pl.BlockSpec((tm, d), lambda i, f: (i, 0)),
                pl.BlockSpec((d, tf), lambda i, f: (0, f)),
                pl.BlockSpec((d, tf), lambda i, f: (0, f)),
                pl.BlockSpec((tf, d), lambda i, f: (f, 0)),
                pl.BlockSpec((tm, d), lambda i, f: (i, 0)),
                _mod_spec(d, M_G2, row_fn),
                _vec_spec(d)]
    args = [h2, wg, wu, wd, xm, mod, n3.reshape(1, d)]
    out_specs = [pl.BlockSpec((tm, d), lambda i, f: (i, 0))]
    out_shape = [jax.ShapeDtypeStruct((n, d), F32)]
    if with_next:
        n0, mod_next = nxt
        in_specs += [_vec_spec(d), _mod_spec(d, M_SC1, row_fn), _mod_spec(d, M_SH1, row_fn)]
        args += [n0.reshape(1, d), mod_next, mod_next]
        out_specs.append(pl.BlockSpec((tm, d), lambda i, f: (i, 0)))
        out_shape.append(jax.ShapeDtypeStruct((n, d), BF16))
    kern = functools.partial(_ffn_dense_kernel, nf=nf, with_next=with_next)
    res = pl.pallas_call(
        kern,
        grid=(n // tm, nf),
        in_specs=in_specs,
        out_specs=out_specs,
        out_shape=out_shape,
        scratch_shapes=[pltpu.VMEM((tm, d), F32)],
        compiler_params=_params(("parallel", "arbitrary")),
        name="ffn_dense",
    )(*args)
    return (res[0], res[1]) if with_next else (res[0], None)


def _row_copy(src_hbm, src_row, dst_ref, dst_row, sem):
    return pltpu.make_async_copy(src_hbm.at[pl.ds(src_row, 1)], dst_ref.at[pl.ds(dst_row, 1)], sem)


def _ffn_grouped_kernel(te_ref, nt_ref, src_ref, x_hbm, wg_ref, wu_ref, wd_ref, o_ref,
                        rows_ref, xb_ref, sem, *, nf, tm, grid_tiles):
    i = pl.program_id(0)
    f = pl.program_id(1)
    nt = nt_ref[0]
    slot = i % 2
    issue_steps = max(1, nf - 2)
    rows_per_step = tm // issue_steps

    def start_row(tile, slot_, r):
        _row_copy(x_hbm, src_ref[tile * tm + r], rows_ref.at[slot_], r, sem.at[slot_]).start()

    def wait_rows(slot_):
        pltpu.make_async_copy(x_hbm.at[pl.ds(0, tm)], rows_ref.at[slot_], sem.at[slot_]).wait()

    @pl.when((i == 0) & (f == 0))
    def _():
        def body(r, c):
            start_row(0, 0, r)
            return c
        lax.fori_loop(0, tm, body, 0)

    @pl.when((i < nt) & (f == 0))
    def _():
        wait_rows(slot)
        xb_ref[...] = rows_ref[slot].astype(BF16)
        o_ref[...] = jnp.zeros_like(o_ref)

    def step(request_rows):
        if request_rows:
            nxt = jnp.minimum(i + 1, grid_tiles - 1)
            for j in range(rows_per_step):
                start_row(nxt, 1 - slot, f * rows_per_step + j)
        xb = xb_ref[...]
        g = _dot(xb, wg_ref[...])
        u = _dot(xb, wu_ref[...])
        a = (g * jax.nn.sigmoid(g) * u).astype(BF16)
        o_ref[...] += _dot(a, wd_ref[...])

    pl.when((i < nt) & (f < issue_steps))(lambda: step(True))
    if issue_steps < nf:
        pl.when((i < nt) & (f >= issue_steps))(lambda: step(False))

    @pl.when((i == nt - 1) & (f == nf - 1))
    def _():
        wait_rows(1 - slot)

    @pl.when((i >= nt) & (f == nf - 1))
    def _():
        o_ref[...] = jnp.zeros_like(o_ref)


def _ffn_grouped(h2, src, wg, wu, wd, tile_expert, n_tiles, p_max, tm, tf):
    d = h2.shape[1]
    fe = wg.shape[2]
    nf = fe // tf
    assert tm % max(1, nf - 2) == 0, (tm, nf)

    def fidx(i, f, nt):
        return jnp.where(i < nt[0], f, nf - 1)

    kern = functools.partial(_ffn_grouped_kernel, nf=nf, tm=tm, grid_tiles=p_max // tm)
    return pl.pallas_call(
        kern,
        grid_spec=pltpu.PrefetchScalarGridSpec(
            num_scalar_prefetch=3,
            grid=(p_max // tm, nf),
            in_specs=[pl.BlockSpec(memory_space=pl.ANY),
                      pl.BlockSpec((None, d, tf), lambda i, f, te, nt, src: (te[i], 0, fidx(i, f, nt))),
                      pl.BlockSpec((None, d, tf), lambda i, f, te, nt, src: (te[i], 0, fidx(i, f, nt))),
                      pl.BlockSpec((None, tf, d), lambda i, f, te, nt, src: (te[i], fidx(i, f, nt), 0))],
            out_specs=pl.BlockSpec((tm, d), lambda i, f, te, nt, src: (i, 0)),
            scratch_shapes=[pltpu.VMEM((2, tm, d), h2.dtype),
                            pltpu.VMEM((tm, d), BF16),
                            pltpu.SemaphoreType.DMA((2,))]),
        out_shape=jax.ShapeDtypeStruct((p_max, d), F32),
        compiler_params=pltpu.CompilerParams(dimension_semantics=("arbitrary", "arbitrary"),
                                             vmem_limit_bytes=VMEM_LIMIT,
                                             disable_bounds_checks=True),
        name="ffn_grouped",
    )(tile_expert, n_tiles, src, h2, wg, wu, wd)


def _router_kernel(h_ref, wr_ref, o_ref, cnt_ref, carry_ref, *, n_experts):
    i = pl.program_id(0)

    @pl.when(i == 0)
    def _():
        carry_ref[...] = jnp.zeros_like(carry_ref)

    logits = _dot(h_ref[...].astype(BF16), wr_ref[...])
    tm = logits.shape[0]
    lane_i = lax.broadcasted_iota(I32, logits.shape, 1)
    lane = lane_i.astype(F32)
    logits = jnp.where(lane_i < n_experts, logits, -jnp.inf)
    m1 = jnp.max(logits, axis=-1, keepdims=True)
    i1 = jnp.min(jnp.where(logits == m1, lane, float(LANES)), axis=-1, keepdims=True)
    rest = jnp.where(lane == i1, -jnp.inf, logits)
    m2 = jnp.max(rest, axis=-1, keepdims=True)
    i2 = jnp.min(jnp.where(rest == m2, lane, float(LANES)), axis=-1, keepdims=True)
    e21 = jnp.exp(m2 - m1)
    w1 = 1.0 / (1.0 + e21)
    w2 = e21 * w1

    sel1 = lane == i1
    sel2 = lane == i2
    onehot = jnp.where(sel1 | sel2, 1.0, 0.0).astype(BF16)
    r = lax.broadcasted_iota(I32, (tm, tm), 0)
    c = lax.broadcasted_iota(I32, (tm, tm), 1)
    strict_lower = jnp.where(c < r, 1.0, 0.0).astype(BF16)
    before = _dot(strict_lower, onehot) + carry_ref[0:1, :]
    r1 = jnp.sum(jnp.where(sel1, before, 0.0), axis=-1, keepdims=True)
    r2 = jnp.sum(jnp.where(sel2, before, 0.0), axis=-1, keepdims=True)
    total = carry_ref[0:1, :] + jnp.sum(onehot.astype(F32), axis=0, keepdims=True)
    carry_ref[...] = jnp.broadcast_to(total, carry_ref.shape)
    cnt_ref[...] = jnp.broadcast_to(total, cnt_ref.shape)

    packed = jnp.where(lane_i == 0, i1, 0.0)
    packed = jnp.where(lane_i == 1, i2, packed)
    packed = jnp.where(lane_i == 2, r1, packed)
    packed = jnp.where(lane_i == 3, r2, packed)
    packed = jnp.where(lane_i == 4, w1, packed)
    packed = jnp.where(lane_i == 5, w2, packed)
    o_ref[...] = packed


def _router(h2, w_router, tm):
    n, d = h2.shape
    n_experts = w_router.shape[1]
    wr = jnp.pad(w_router, ((0, 0), (0, LANES - n_experts))).astype(BF16)
    kern = functools.partial(_router_kernel, n_experts=n_experts)
    return pl.pallas_call(
        kern,
        grid=(n // tm,),
        in_specs=[pl.BlockSpec((tm, d), lambda i: (i, 0)),
                  pl.BlockSpec((d, LANES), lambda i: (0, 0))],
        out_specs=[pl.BlockSpec((tm, LANES), lambda i: (i, 0)),
                   pl.BlockSpec((SUBLANES, LANES), lambda i: (0, 0))],
        out_shape=[jax.ShapeDtypeStruct((n, LANES), F32),
                   jax.ShapeDtypeStruct((SUBLANES, LANES), F32)],
        scratch_shapes=[pltpu.VMEM((SUBLANES, LANES), F32)],
        compiler_params=_params(("arbitrary",)),
        name="router",
    )(h2, wr)


def _combine_kernel(d1_ref, d2_ref, y_hbm, r_ref, xm_ref, gate_ref, n3_ref, xo_ref, buf_ref, sem, *,
                    tm, n_tiles):
    i = pl.program_id(0)
    slot = i % 2
    dest = (d1_ref, d2_ref)

    def start_rows(tile, slot_, r):
        for k in range(TOP_K):
            _row_copy(y_hbm, dest[k][tile * tm + r], buf_ref.at[slot_, k], r, sem.at[slot_]).start()

    def wait_rows(slot_):
        for k in range(TOP_K):
            pltpu.make_async_copy(y_hbm.at[pl.ds(0, tm)], buf_ref.at[slot_, k], sem.at[slot_]).wait()

    @pl.when(i == 0)
    def _():
        def body(r, c):
            start_rows(0, 0, r)
            return c
        lax.fori_loop(0, tm, body, 0)

    wait_rows(slot)
    nxt = jnp.minimum(i + 1, n_tiles - 1)
    for r in range(tm):
        start_rows(nxt, 1 - slot, r)
    route = r_ref[...]
    y = route[:, 4:5] * buf_ref[slot, 0] + route[:, 5:6] * buf_ref[slot, 1]
    xo_ref[...] = xm_ref[...] + gate_ref[...] * _rms(y, n3_ref[...])

    @pl.when(i == n_tiles - 1)
    def _():
        wait_rows(1 - slot)


def _combine(yo, dest1, dest2, route, xm, n3, mod, row_fn, tm):
    n, d = xm.shape
    kern = functools.partial(_combine_kernel, tm=tm, n_tiles=n // tm)
    return pl.pallas_call(
        kern,
        grid_spec=pltpu.PrefetchScalarGridSpec(
            num_scalar_prefetch=2,
            grid=(n // tm,),
            in_specs=[pl.BlockSpec(memory_space=pl.ANY),
                      pl.BlockSpec((tm, LANES), lambda i, *_: (i, 0)),
                      pl.BlockSpec((tm, d), lambda i, *_: (i, 0)),
                      _mod_spec(d, M_G2, row_fn),
                      _vec_spec(d)],
            out_specs=pl.BlockSpec((tm, d), lambda i, *_: (i, 0)),
            scratch_shapes=[pltpu.VMEM((2, TOP_K, tm, d), F32), pltpu.SemaphoreType.DMA((2,))]),
        out_shape=jax.ShapeDtypeStruct((n, d), F32),
        compiler_params=pltpu.CompilerParams(dimension_semantics=("arbitrary",),
                                             vmem_limit_bytes=VMEM_LIMIT,
                                             disable_bounds_checks=True),
        name="moe_combine",
    )(dest1, dest2, yo, route, xm, mod, n3.reshape(1, d))


def _moe(h2, xm, w_router, wg, wu, wd, n3, mod, row_fn, tm_g, tf):
    n, d = h2.shape
    n_experts = w_router.shape[1]
    route, counts = _router(h2, w_router, min(512, n))
    e1 = route[:, 0].astype(I32)
    e2 = route[:, 1].astype(I32)
    r1 = route[:, 2].astype(I32)
    r2 = route[:, 3].astype(I32)
    cnt = counts[0, :n_experts].astype(I32)
    padded = ((cnt + tm_g - 1) // tm_g) * tm_g
    ends = jnp.cumsum(padded)
    offs = ends - padded
    dest1 = offs[e1] + r1
    dest2 = offs[e2] + r2
    p_max = ((TOP_K * n + n_experts * (tm_g - 1)) // tm_g) * tm_g
    n_rows = ends[-1:]
    tok = jnp.arange(n, dtype=I32)
    src = jnp.zeros((p_max,), I32).at[dest1].set(tok).at[dest2].set(tok)
    tile_start = jnp.arange(p_max // tm_g, dtype=I32) * tm_g
    tile_expert = jnp.minimum(jnp.sum(tile_start[:, None] >= ends[None, :], axis=1),
                              n_experts - 1).astype(I32)
    last_expert = tile_expert[jnp.maximum(n_rows[0] // tm_g - 1, 0)]
    tile_expert = jnp.where(tile_start < n_rows[0], tile_expert, last_expert)
    yo = _ffn_grouped(h2, src, wg, wu, wd, tile_expert, n_rows // tm_g, p_max, tm_g, tf)
    return _combine(yo, dest1, dest2, route, xm, n3, mod, row_fn, min(256, n))


def _rope_tables(seq):
    inv = ROPE_BASE ** (-jnp.arange(ROPE_FREQ, dtype=F32) / ROPE_FREQ)
    t = jnp.arange(seq, dtype=I32)
    pos = jnp.stack([t // GRID_W, t % GRID_W], axis=-1).astype(F32)
    ang = pos[:, :, None] * inv
    cos, sin = jnp.cos(ang), jnp.sin(ang)
    cos = jnp.concatenate([cos, cos], axis=-1).reshape(seq, DIFF_HEAD_DIM)
    sin = jnp.concatenate([-sin, sin], axis=-1).reshape(seq, DIFF_HEAD_DIM)
    reps = LANES // DIFF_HEAD_DIM
    return jnp.tile(cos, (1, reps)), jnp.tile(sin, (1, reps))


def _pad_cols(w, mult):
    pad = (-w.shape[-1]) % mult
    return jnp.pad(w, [(0, 0)] * (w.ndim - 1) + [(0, pad)]) if pad else w


def _pad_rows(w, mult):
    pad = (-w.shape[-2]) % mult
    return jnp.pad(w, [(0, 0)] * (w.ndim - 2) + [(0, pad), (0, 0)]) if pad else w


def kernel(x, c, ctx, c_ctx, norm_g, w_ada, b_ada, w_in, diff_lambda, diff_subln_g, lru_conv_w, lru_conv_b, lru_gate_w, lru_gate_b, lru_lambda, na_rpb, w_branch, w_merge, b_merge, w_out, ffn_w_gate, ffn_w_up, ffn_w_down, moe_w_router, moe_w_gate, moe_w_up, moe_w_down):
    bsz, seq, d = x.shape
    clen = ctx.shape[1]
    depth = w_in.shape[0]
    n, nc = bsz * seq, bsz * clen
    x_lat = x.reshape(n, d)
    x_ctx = ctx.reshape(nc, d)

    mr = -(-(bsz + 1) // SUBLANES) * SUBLANES
    cvec = jnp.concatenate([c, c_ctx[None], jnp.zeros((mr - bsz - 1, d), F32)], axis=0)
    mod_all = _ada(cvec, w_ada, b_ada).reshape(depth, mr, 1, 6 * d)
    cos_t, sin_t = _rope_tables(seq)

    tm_lat, tm_ctx = min(1024, seq), min(1024, nc, seq)
    tm_mix_lat, tm_mix_ctx = min(256, seq), min(256, nc)
    tm_ffn_lat, tm_ffn_ctx = min(512, seq), min(512, nc)
    tf = min(512, d)
    tf_moe = min(MOE_TF, d)
    tq = min(DIFF_TQ, seq)

    def lat_row(tile_rows):
        per_batch = seq // tile_rows
        return lambda i: i // per_batch

    def ctx_row(i):
        return bsz

    h_lat = _prenorm(x_lat, norm_g[0, 0], mod_all[0], lat_row(tm_lat), tm_lat)
    h_ctx = _prenorm(x_ctx, norm_g[0, 0], mod_all[0], ctx_row, tm_ctx)

    rest_names = ("dv", "lx", "lg", "nq", "nk", "nv")
    part_scale = [1.0] * N_IN_PARTS
    part_scale[P_DQ], part_scale[P_NQ] = DIFF_QSCALE, NA_QSCALE
    col_scale = jnp.repeat(jnp.asarray(part_scale, F32), BRANCH_WIDTH)[None]
    scale_qk, scale_rest = col_scale[:, :2 * BRANCH_WIDTH], col_scale[:, 2 * BRANCH_WIDTH:]
    diff_steps = bsz * DIFF_HEADS * (seq // tq)
    moe_src = (moe_w_gate, moe_w_up, moe_w_down)

    def halves(w):
        return w.reshape(2, -1, w.shape[-1])

    def side_cast_ok(mi):
        return moe_w_router.shape[-1] % 2 == 0 and all(
            halves(w[mi]).shape[1] % (16 * diff_steps) == 0 for w in moe_src)

    moe_bf16 = {}
    for l in range(depth):
        need_ctx = l < depth - 1
        mod = mod_all[l]
        lam_init = 0.8 - 0.6 * math.exp(-0.3 * l)
        w_qk = w_in[l][:, :2 * BRANCH_WIDTH].astype(BF16)
        w_rest = w_in[l][:, 2 * BRANCH_WIDTH:].astype(BF16)
        w_gate = w_merge[l].astype(BF16)
        b_gate = b_merge[l][None]

        def project(h, tm, rope):
            qk = _proj(h, w_qk, scale_qk, cos_t, sin_t, seq, tm, PROJ_TN_ROPE if rope else PROJ_TN,
                       "rope" if rope else "plain")
            rest = _proj(h, w_rest, scale_rest, cos_t, sin_t, seq, tm, PROJ_TN, "plain")
            parts = {"dq": (qk, 0), "dk": (qk, 1)}
            parts.update({name: (rest, k) for k, name in enumerate(rest_names)})
            return parts

        lat = project(h_lat, tm_lat, True)
        ctxp = project(h_ctx, tm_ctx, False)
        gate_lat = _proj(h_lat, w_gate, b_gate, cos_t, sin_t, seq, tm_lat, PROJ_TN, "gate")

        mi, half = l // 2, l % 2
        jobs = ()
        if mi < moe_w_gate.shape[0] and side_cast_ok(mi):
            jobs = tuple((halves(w[mi]), half, moe_bf16[mi][k] if half else None)
                         for k, w in enumerate(moe_src))
        y_diff, cast = _diff_attn_lat(lat, ctxp, diff_lambda[l], diff_subln_g[l], lam_init,
                                      bsz, seq, clen, tq, jobs)
        if jobs:
            moe_bf16[mi] = cast
        y_lru, y_lru_c = _lru(lat, ctxp, lru_conv_w[l], lru_conv_b[l], lru_gate_w[l],
                              lru_gate_b[l], lru_lambda[l], bsz, seq, clen, need_ctx)
        y_na = _na_lat(lat, ctxp, na_rpb[l], bsz, seq, clen)

        wb = w_branch[l].astype(BF16)
        wo = w_out[l].astype(BF16)
        is_moe = l % 2 == 1
        h2_dtype = F32 if is_moe else BF16
        xm_lat, h2_lat = _merge(y_diff, y_lru, y_na, (gate_lat, 0), wb, wo, x_lat, norm_g[l, 1],
                                norm_g[l, 2], mod, lat_row(tm_mix_lat), tm_mix_lat, h2_dtype)
        if need_ctx:
            y_diff_c = _diff_attn_ctx(ctxp, diff_lambda[l], diff_subln_g[l], lam_init, bsz, clen)
            y_na_c = _na_ctx(ctxp, bsz, clen)
            gate_ctx = _proj(h_ctx, w_gate, b_gate, cos_t, sin_t, seq, tm_ctx, PROJ_TN, "gate")
            xm_ctx, h2_ctx = _merge(y_diff_c, y_lru_c, y_na_c, (gate_ctx, 0), wb, wo, x_ctx,
                                    norm_g[l, 1], norm_g[l, 2], mod, ctx_row, tm_mix_ctx, h2_dtype)

        nxt = (norm_g[l + 1, 0], mod_all[l + 1]) if need_ctx else None
        i = l // 2
        if is_moe:
            if i in moe_bf16:
                wg, wu, wd = (c.reshape(w.shape[1:]) for c, w in zip(moe_bf16[i], moe_src))
            else:
                wg, wu, wd = (w[i].astype(BF16) for w in moe_src)
            tm_g = min(MOE_TILE, n)
            x_lat = _moe(h2_lat, xm_lat, moe_w_router[i], wg, wu, wd, norm_g[l, 3], mod,
                         lat_row(min(256, n)), tm_g, tf_moe)
            if need_ctx:
                x_ctx = _moe(h2_ctx, xm_ctx, moe_w_router[i], wg, wu, wd, norm_g[l, 3], mod,
                             ctx_row, min(MOE_TILE, nc), tf_moe)
                h_lat = _prenorm(x_lat, nxt[0], nxt[1], lat_row(tm_lat), tm_lat)
                h_ctx = _prenorm(x_ctx, nxt[0], nxt[1], ctx_row, tm_ctx)
        else:
            wg = _pad_cols(ffn_w_gate[i], tf).astype(BF16)
            wu = _pad_cols(ffn_w_up[i], tf).astype(BF16)
            wd = _pad_rows(ffn_w_down[i], tf).astype(BF16)
            x_lat, h_lat = _ffn_dense(h2_lat, wg, wu, wd, xm_lat, norm_g[l, 3], mod,
                                      lat_row(tm_ffn_lat), tm_ffn_lat, tf, nxt)
            if need_ctx:
                x_ctx, h_ctx = _ffn_dense(h2_ctx, wg, wu, wd, xm_ctx, norm_g[l, 3], mod,
                                          ctx_row, tm_ffn_ctx, tf, nxt)
    return x_lat.reshape(bsz, seq, d)
```

```python
import functools
import math

import numpy as np
import jax
import jax.numpy as jnp
from jax import lax
from jax.experimental import pallas as pl
from jax.experimental.pallas import tpu as pltpu

F32 = jnp.float32
BF16 = jnp.bfloat16
I32 = jnp.int32

EPS = 1e-6
GRID_W = 64
BRANCH_WIDTH = 1024
N_IN_PARTS = 8
IN_WIDTH = N_IN_PARTS * BRANCH_WIDTH
DIFF_HEADS = 8
DIFF_HEAD_DIM = 64
DIFF_V_DIM = 2 * DIFF_HEAD_DIM
ROPE_BASE = 10000.0
ROPE_FREQ = DIFF_HEAD_DIM // 4
LRU_BLOCKS = 8
LRU_BLOCK_DIM = BRANCH_WIDTH // LRU_BLOCKS
LRU_CONV_W = 4
LRU_C = 8.0
NA_HEADS = 8
NA_HEAD_DIM = BRANCH_WIDTH // NA_HEADS
NA_WIN_R = 8
NA_WIN_C = 16
TOP_K = 2
MOE_TILE = 560
MOE_TF = 1024
DIFF_TQ = 1024
PROJ_TN = 2048
PROJ_TN_ROPE = 1024

LANES = 128
SUBLANES = 8
VMEM_LIMIT = 56 * 1024 * 1024
NEG = -1e30
LOG2E = 1.0 / math.log(2.0)
DIFF_QSCALE = DIFF_HEAD_DIM ** -0.5 * LOG2E
NA_QSCALE = NA_HEAD_DIM ** -0.5 * LOG2E

P_DQ, P_DK, P_DV, P_LX, P_LG, P_NQ, P_NK, P_NV = range(8)
M_SH1, M_SC1, M_G1, M_SH2, M_SC2, M_G2 = range(6)


def _params(sem):
    return pltpu.CompilerParams(dimension_semantics=sem, vmem_limit_bytes=VMEM_LIMIT)


def _rms(x, g):
    return x * lax.rsqrt(jnp.mean(x * x, axis=-1, keepdims=True) + EPS) * g


def _dot(a, b):
    return jnp.dot(a, b, preferred_element_type=F32)


def _dot_nt(a, b):
    return lax.dot_general(a, b, (((1,), (1,)), ((), ())), preferred_element_type=F32)


def _ada_kernel(c_ref, w_ref, b_ref, o_ref):
    c = c_ref[...]
    s = (c * jax.nn.sigmoid(c)).astype(BF16)
    o_ref[...] = _dot(s, w_ref[...].astype(BF16)) + b_ref[...]


def _ada(cvec, w_ada, b_ada):
    n_layers, d, n6 = w_ada.shape
    mr = cvec.shape[0]
    tn = min(1024, d)
    return pl.pallas_call(
        _ada_kernel,
        grid=(n_layers, n6 // tn),
        in_specs=[pl.BlockSpec((mr, d), lambda l, j: (0, 0)),
                  pl.BlockSpec((None, d, tn), lambda l, j: (l, 0, j)),
                  pl.BlockSpec((None, 1, tn), lambda l, j: (l, 0, j))],
        out_specs=pl.BlockSpec((None, mr, tn), lambda l, j: (l, 0, j)),
        out_shape=jax.ShapeDtypeStruct((n_layers, mr, n6), F32),
        compiler_params=_params(("arbitrary", "arbitrary")),
        name="ada_mod",
    )(cvec, w_ada, b_ada.reshape(n_layers, 1, n6))


def _mod_spec(d, chunk, row_fn):
    return pl.BlockSpec((None, 1, d), lambda i, *_: (row_fn(i), 0, chunk))


def _vec_spec(d):
    return pl.BlockSpec((1, d), lambda i, *_: (0, 0))


def _prenorm_kernel(x_ref, g_ref, sc_ref, sh_ref, o_ref):
    y = _rms(x_ref[...], g_ref[...])
    o_ref[...] = (y * (1.0 + sc_ref[...]) + sh_ref[...]).astype(o_ref.dtype)


def _prenorm(x, g, mod, row_fn, tm):
    n, d = x.shape
    return pl.pallas_call(
        _prenorm_kernel,
        grid=(n // tm,),
        in_specs=[pl.BlockSpec((tm, d), lambda i: (i, 0)),
                  _vec_spec(d),
                  _mod_spec(d, M_SC1, row_fn),
                  _mod_spec(d, M_SH1, row_fn)],
        out_specs=pl.BlockSpec((tm, d), lambda i: (i, 0)),
        out_shape=jax.ShapeDtypeStruct((n, d), BF16),
        compiler_params=_params(("parallel",)),
        name="prenorm",
    )(x, g.reshape(1, d), mod, mod)


def _proj_kernel(h_ref, w_ref, v_ref, cos_ref, sin_ref, o_ref, *, kind):
    acc = _dot(h_ref[...], w_ref[...])
    tn = acc.shape[1]
    if kind == "gate":
        o_ref[...] = jax.nn.sigmoid(acc + v_ref[...]).astype(o_ref.dtype)
    elif kind == "rope":
        lane = lax.broadcasted_iota(I32, acc.shape, 1)
        first_half = (lane % (2 * ROPE_FREQ)) < ROPE_FREQ
        partner = jnp.where(first_half, pltpu.roll(acc, tn - ROPE_FREQ, 1),
                            pltpu.roll(acc, ROPE_FREQ, 1))
        cos = jnp.tile(cos_ref[...], (1, tn // LANES)) * v_ref[...]
        sin = jnp.tile(sin_ref[...], (1, tn // LANES)) * v_ref[...]
        o_ref[...] = (acc * cos + partner * sin).astype(o_ref.dtype)
    else:
        o_ref[...] = (acc * v_ref[...]).astype(o_ref.dtype)


def _proj(h, w, v, cos_t, sin_t, seq, tm, tn, kind):
    n, d = h.shape
    nw = w.shape[1]
    tn = min(tn, nw)
    tpb = seq // tm
    kern = functools.partial(_proj_kernel, kind=kind)
    return pl.pallas_call(
        kern,
        grid=(n // tm, nw // tn),
        in_specs=[pl.BlockSpec((tm, d), lambda i, j: (i, 0)),
                  pl.BlockSpec((d, tn), lambda i, j: (0, j)),
                  pl.BlockSpec((1, tn), lambda i, j: (0, j)),
                  pl.BlockSpec((tm, LANES), lambda i, j: (i % tpb, 0)),
                  pl.BlockSpec((tm, LANES), lambda i, j: (i % tpb, 0))],
        out_specs=pl.BlockSpec((tm, tn), lambda i, j: (i, j)),
        out_shape=jax.ShapeDtypeStruct((n, nw), BF16),
        compiler_params=_params(("parallel", "arbitrary")),
        name="proj_" + kind,
    )(h, w, v, cos_t, sin_t)


DIFF_KEY_CHUNK = 512


def _diff_kernel(*refs, lam_init, has_lat, n_cast, n_alias=0):
    n_in = 7 if has_lat else 5
    n_extra = n_cast + n_alias
    for src_ref, dst_ref in zip(refs[n_in:n_in + n_cast], refs[n_in + n_extra + 1:]):
        dst_ref[0] = src_ref[...].astype(dst_ref.dtype)
        if dst_ref.shape[0] == 2:
            dst_ref[1] = jnp.zeros(dst_ref.shape[1:], dst_ref.dtype)
    if has_lat:
        lamv_ref, sg_ref, q_ref, k_ref, v_ref, kc_ref, vc_ref = refs[:n_in]
    else:
        lamv_ref, sg_ref, q_ref, kc_ref, vc_ref = refs[:n_in]
    o_ref = refs[n_in + n_extra]
    lv = lamv_ref[...]
    lam = (jnp.exp(jnp.sum(lv[0:1] * lv[1:2], axis=-1, keepdims=True))
           - jnp.exp(jnp.sum(lv[2:3] * lv[3:4], axis=-1, keepdims=True)) + lam_init)
    q = q_ref[...]
    first = lax.broadcasted_iota(I32, q.shape, 1) < DIFF_HEAD_DIM
    zero = jnp.zeros_like(q)
    qz = (jnp.where(first, q, zero), jnp.where(first, zero, q))
    chunks = [(kc_ref, vc_ref, 0, kc_ref.shape[0])]
    if has_lat:
        kt = min(DIFF_KEY_CHUNK, k_ref.shape[0])
        chunks += [(k_ref, v_ref, j * kt, kt) for j in range(k_ref.shape[0] // kt)]

    def qk(c, chunk):
        k, _, off, n = chunk
        return _dot_nt(qz[c], k[off:off + n, :])

    def row_max(parts):
        m = jnp.max(parts[0], axis=-1, keepdims=True)
        for p in parts[1:]:
            m = jnp.maximum(m, jnp.max(p, axis=-1, keepdims=True))
        return m

    def softmax_av(s_parts, between=None):
        m = row_max(s_parts)
        den, out = 0.0, 0.0
        for j, (_, v, off, n) in enumerate(chunks):
            if between is not None:
                between(j)
            e = jnp.exp2(s_parts[j] - m)
            den = den + jnp.sum(e, axis=-1, keepdims=True)
            out = out + _dot(e.astype(BF16), v[off:off + n, :])
        return out, den

    s1 = [qk(0, ch) for ch in chunks]
    s2 = []
    o1, den1 = softmax_av(s1, between=lambda j: s2.append(qk(1, chunks[j])))
    o2, den2 = softmax_av(s2)
    o = o1 * (1.0 / den1) - o2 * (lam / den2)
    y = _rms(o, sg_ref[...]) * (1.0 - lam_init)
    o_ref[...] = y.astype(o_ref.dtype)


def _diff_attn_lat(lat, ctxp, lamv, subln_g, lam_init, bsz, seq, clen, tq, cast_jobs=()):
    nq = seq // tq
    n_steps = bsz * DIFF_HEADS * nq
    hw = DIFF_V_DIM
    hpb = BRANCH_WIDTH // hw

    def spec(rows, src, row_fn):
        blk = src[1] * hpb
        return pl.BlockSpec((rows, hw), lambda b, h, i: (row_fn(b, i), blk + h))

    def step(b, h, i):
        return (b * DIFF_HEADS + h) * nq + i

    n_base = 7
    cast_in, cast_out, cast_shapes, cast_args, aliases = [], [], [], [], {}
    for k, (arr, half, dst) in enumerate(cast_jobs):
        _, rows, cols = arr.shape
        per_step = rows // n_steps
        assert per_step * n_steps == rows and per_step % 16 == 0
        cast_in.append(pl.BlockSpec((None, per_step, cols), lambda b, h, i, half=half: (half, step(b, h, i), 0)))
        cast_args.append(arr)
        cast_shapes.append(jax.ShapeDtypeStruct(arr.shape, BF16))
        if dst is None:
            cast_out.append(pl.BlockSpec((2, per_step, cols), lambda b, h, i: (0, step(b, h, i), 0)))
        else:
            assert half == 1
            cast_out.append(pl.BlockSpec((1, per_step, cols), lambda b, h, i: (1, step(b, h, i), 0)))
    for k, (arr, half, dst) in enumerate(cast_jobs):
        if dst is not None:
            aliases[n_base + len(cast_jobs) + len(aliases)] = 1 + k
            cast_in.append(pl.BlockSpec(memory_space=pl.ANY))
            cast_args.append(dst)
    n_alias = len(aliases)
    kern = functools.partial(_diff_kernel, lam_init=lam_init, has_lat=True, n_cast=len(cast_jobs),
                             n_alias=n_alias)
    res = pl.pallas_call(
        kern,
        grid=(bsz, DIFF_HEADS, nq),
        in_specs=[pl.BlockSpec((4, DIFF_HEAD_DIM), lambda b, h, i: (0, 0)),
                  pl.BlockSpec((1, hw), lambda b, h, i: (0, 0)),
                  spec(tq, lat["dq"], lambda b, i: b * nq + i),
                  spec(seq, lat["dk"], lambda b, i: b),
                  spec(seq, lat["dv"], lambda b, i: b),
                  spec(clen, ctxp["dk"], lambda b, i: b),
                  spec(clen, ctxp["dv"], lambda b, i: b)] + cast_in,
        out_specs=[pl.BlockSpec((tq, hw), lambda b, h, i: (b * nq + i, h))] + cast_out,
        out_shape=[jax.ShapeDtypeStruct((bsz * seq, BRANCH_WIDTH), BF16)] + cast_shapes,
        input_output_aliases=aliases,
        compiler_params=_params(("parallel", "parallel", "arbitrary")),
        name="diff_attn",
    )(lamv, subln_g.reshape(1, hw), lat["dq"][0], lat["dk"][0], lat["dv"][0],
      ctxp["dk"][0], ctxp["dv"][0], *cast_args)
    return res[0], res[1:]


def _diff_attn_ctx(ctxp, lamv, subln_g, lam_init, bsz, clen):
    hw = DIFF_V_DIM
    hpb = BRANCH_WIDTH // hw

    def spec(src):
        blk = src[1] * hpb
        return pl.BlockSpec((clen, hw), lambda b, h: (b, blk + h))

    kern = functools.partial(_diff_kernel, lam_init=lam_init, has_lat=False, n_cast=0)
    return pl.pallas_call(
        kern,
        grid=(bsz, DIFF_HEADS),
        in_specs=[pl.BlockSpec((4, DIFF_HEAD_DIM), lambda b, h: (0, 0)),
                  pl.BlockSpec((1, hw), lambda b, h: (0, 0)),
                  spec(ctxp["dq"]), spec(ctxp["dk"]), spec(ctxp["dv"])],
        out_specs=pl.BlockSpec((clen, hw), lambda b, h: (b, h)),
        out_shape=jax.ShapeDtypeStruct((bsz * clen, BRANCH_WIDTH), BF16),
        compiler_params=_params(("parallel", "parallel")),
        name="diff_attn_ctx",
    )(lamv, subln_g.reshape(1, hw), ctxp["dq"][0], ctxp["dk"][0], ctxp["dv"][0])


LRU_TW = 256
LRU_UNROLL = 4


def _gelu_tanh(x):
    return 0.5 * x * (1.0 + jnp.tanh(math.sqrt(2.0 / math.pi) * (x + 0.044715 * (x * x * x))))


def _tile_scan(a, b, reverse):
    rows = lax.broadcasted_iota(I32, a.shape, 0)
    for s in (1, 2, 4):
        if reverse:
            a_s = pltpu.roll(a, SUBLANES - s, 0)
            b_s = pltpu.roll(b, SUBLANES - s, 0)
            valid = rows < SUBLANES - s
        else:
            a_s = pltpu.roll(a, s, 0)
            b_s = pltpu.roll(b, s, 0)
            valid = rows >= s
        b = jnp.where(valid, a * b_s + b, b)
        a = jnp.where(valid, a * a_s, a)
    return a, b


def _lru_kernel(xl_ref, gl_ref, xc_ref, gc_ref, cw_ref, cb_ref, gw_ref, gb_ref, lam_ref,
                *out_and_scratch, seq, clen, need_ctx):
    if need_ctx:
        yl_ref, yc_ref, af_ref, bf_ref, ab_ref, bb_ref, hf_ref = out_and_scratch
    else:
        yl_ref, af_ref, bf_ref, ab_ref, bb_ref, hf_ref = out_and_scratch
        yc_ref = None
    tot = clen + seq
    x = jnp.concatenate([xc_ref[...], xl_ref[...]], axis=0).astype(F32)
    row = lax.broadcasted_iota(I32, x.shape, 0)
    pos = jnp.where(row < clen, row, row - clen)
    seg_len = jnp.where(row < clen, clen, seq)
    cw = cw_ref[...]
    conv = cb_ref[...] + x * cw[2:3]
    conv = conv + jnp.where(pos >= 2, pltpu.roll(x, 2, 0), 0.0) * cw[0:1]
    conv = conv + jnp.where(pos >= 1, pltpu.roll(x, 1, 0), 0.0) * cw[1:2]
    conv = conv + jnp.where(pos < seg_len - 1, pltpu.roll(x, tot - 1, 0), 0.0) * cw[3:4]
    xb = conv.astype(BF16)
    gb = gb_ref[...]
    lam = lam_ref[...]
    for d, (a_ref, b_ref) in enumerate(((af_ref, bf_ref), (ab_ref, bb_ref))):
        r = jax.nn.sigmoid(_dot(xb, gw_ref[d, 0]) + gb[2 * d:2 * d + 1])
        i = jax.nn.sigmoid(_dot(xb, gw_ref[d, 1]) + gb[2 * d + 1:2 * d + 2])
        z = -lam[d:d + 1]
        softplus = jnp.maximum(z, 0.0) + jnp.log1p(jnp.exp(-jnp.abs(z)))
        log_a = -LRU_C * r * softplus
        a = jnp.exp(log_a)
        beta = jnp.sqrt(-jnp.tanh(log_a) * (a * a + 1.0))
        a_ref[...] = a
        b_ref[...] = beta * i * conv

    n_ct = clen // SUBLANES
    n_t = tot // SUBLANES
    w = x.shape[1]
    assert n_ct % LRU_UNROLL == 0 and n_t % LRU_UNROLL == 0
    rows_per_iter = LRU_UNROLL * SUBLANES

    def fwd_step(t, carry):
        base = pl.multiple_of(t * rows_per_iter, rows_per_iter)
        scans = [_tile_scan(af_ref[pl.ds(base + k * SUBLANES, SUBLANES), :],
                            bf_ref[pl.ds(base + k * SUBLANES, SUBLANES), :], False)
                 for k in range(LRU_UNROLL)]
        for k, (a, b) in enumerate(scans):
            h = a * carry + b
            hf_ref[pl.ds(base + k * SUBLANES, SUBLANES), :] = h
            carry = h[SUBLANES - 1:SUBLANES, :]
        return carry

    lax.fori_loop(0, n_t // LRU_UNROLL, fwd_step, jnp.zeros((1, w), F32))

    def bwd_iter(first_tile, n_tiles, g_ref, y_ref, seg_off):
        def step(t, carry):
            base = pl.multiple_of((first_tile + n_tiles) * SUBLANES - (t + 1) * rows_per_iter,
                                  rows_per_iter)
            offs = [base + k * SUBLANES for k in reversed(range(LRU_UNROLL))]
            scans = [_tile_scan(ab_ref[pl.ds(off, SUBLANES), :], bb_ref[pl.ds(off, SUBLANES), :], True)
                     for off in offs]
            for off, (a, b) in zip(offs, scans):
                h = a * carry + b
                carry = h[0:1, :]
                if y_ref is not None:
                    g = g_ref[pl.ds(off - seg_off, SUBLANES), :].astype(F32)
                    y_ref[pl.ds(off - seg_off, SUBLANES), :] = (
                        (hf_ref[pl.ds(off, SUBLANES), :] + h) * _gelu_tanh(g)).astype(y_ref.dtype)
            return carry
        return step

    carry = lax.fori_loop(0, n_ct // LRU_UNROLL, bwd_iter(0, n_ct, gc_ref, yc_ref, 0),
                          jnp.zeros((1, w), F32))
    lax.fori_loop(0, (n_t - n_ct) // LRU_UNROLL, bwd_iter(n_ct, n_t - n_ct, gl_ref, yl_ref, clen), carry)


def _lru(lat, ctxp, conv_w, conv_b, gate_w, gate_b, lru_lam, bsz, seq, clen, need_ctx):
    n = bsz * seq
    tw = LRU_TW
    nwt = BRANCH_WIDTH // tw
    per = tw // LRU_BLOCK_DIM
    gw = gate_w.reshape(2, 2, nwt, per, LRU_BLOCK_DIM, LRU_BLOCK_DIM)
    eye = jnp.eye(per, dtype=gate_w.dtype)
    gw = jnp.einsum('dgnpcf,pq->dgnpcqf', gw, eye).reshape(2, 2, nwt, tw, tw).astype(BF16)
    gb = gate_b.reshape(4, BRANCH_WIDTH)
    tot = seq + clen
    kern = functools.partial(_lru_kernel, seq=seq, clen=clen, need_ctx=need_ctx)

    def part_spec(rows, src):
        blk = src[1] * nwt
        return pl.BlockSpec((rows, tw), lambda b, j: (b, blk + j))

    out_shape = [jax.ShapeDtypeStruct((n, BRANCH_WIDTH), BF16)]
    out_specs = [pl.BlockSpec((seq, tw), lambda b, j: (b, j))]
    if need_ctx:
        out_shape.append(jax.ShapeDtypeStruct((bsz * clen, BRANCH_WIDTH), BF16))
        out_specs.append(pl.BlockSpec((clen, tw), lambda b, j: (b, j)))
    res = pl.pallas_call(
        kern,
        grid=(bsz, nwt),
        in_specs=[part_spec(seq, lat["lx"]), part_spec(seq, lat["lg"]),
                  part_spec(clen, ctxp["lx"]), part_spec(clen, ctxp["lg"]),
                  pl.BlockSpec((LRU_CONV_W, tw), lambda b, j: (0, j)),
                  pl.BlockSpec((1, tw), lambda b, j: (0, j)),
                  pl.BlockSpec((2, 2, None, tw, tw), lambda b, j: (0, 0, j, 0, 0)),
                  pl.BlockSpec((4, tw), lambda b, j: (0, j)),
                  pl.BlockSpec((2, tw), lambda b, j: (0, j))],
        out_specs=out_specs,
        out_shape=out_shape,
        scratch_shapes=[pltpu.VMEM((tot, tw), F32) for _ in range(5)],
        compiler_params=_params(("parallel", "parallel")),
        name="rglru",
    )(lat["lx"][0], lat["lg"][0], ctxp["lx"][0], ctxp["lg"][0], conv_w,
      conv_b.reshape(1, BRANCH_WIDTH), gw, gb, lru_lam)
    return (res[0], res[1]) if need_ctx else (res[0], None)


NA_ROWS_PER_TILE = 4


def _na_tables(rows, rpt):
    kr = min(NA_WIN_R, rows)
    nkr = min(rows, rpt + kr - 1)
    variants, index, tile_variant, tile_kb = [], {}, [], []
    for t in range(rows // rpt):
        r0 = t * rpt
        kb = int(np.clip(r0 - kr // 2, 0, rows - nkr))
        geom = []
        for a in range(rpt):
            r = r0 + a
            rs = int(np.clip(r - kr // 2, 0, rows - kr))
            geom.append(tuple((kb + k - r + NA_WIN_R - 1) if rs <= kb + k < rs + kr else None
                              for k in range(nkr)))
        geom = tuple(geom)
        if geom not in index:
            index[geom] = len(variants)
            variants.append(geom)
        tile_variant.append(index[geom])
        tile_kb.append(kb)
    return nkr, np.array(tile_variant, np.int32), np.array(tile_kb, np.int32), variants


def _na_bias(rpb, variants):
    heads = rpb.shape[0]
    pad = GRID_W - NA_WIN_C
    padded = jnp.pad(rpb * LOG2E, ((0, 0), (0, 0), (pad, pad)))
    toep = jnp.stack([padded[..., GRID_W - 1 - qc:2 * GRID_W - 1 - qc] for qc in range(GRID_W)],
                     axis=2)
    col = np.arange(GRID_W)
    cs = np.clip(col - NA_WIN_C // 2, 0, GRID_W - NA_WIN_C)
    in_cols = (col[None, :] >= cs[:, None]) & (col[None, :] < cs[:, None] + NA_WIN_C)
    toep = jnp.where(in_cols, toep, NEG).astype(F32)
    outside = jnp.full((heads, GRID_W, GRID_W), NEG, F32)
    return jnp.stack([
        jnp.concatenate([
            jnp.concatenate([outside if dr is None else toep[:, dr] for dr in row], axis=-1)
            for row in geom], axis=-2)
        for geom in variants])


def _na_kernel(var_ref, kb_ref, q_ref, k_ref, v_ref, kc_ref, vc_ref, bias_ref, o_ref, *, nk):
    t = pl.program_id(1)
    start = pl.multiple_of(kb_ref[t] * GRID_W, GRID_W)
    for h in range(NA_HEADS):
        sl = slice(h * NA_HEAD_DIM, (h + 1) * NA_HEAD_DIM)
        qh = q_ref[:, sl]
        s_w = _dot_nt(qh, k_ref[pl.ds(start, nk), sl]) + bias_ref[h]
        s_c = _dot_nt(qh, kc_ref[:, sl])
        m = jnp.maximum(jnp.max(s_w, axis=-1, keepdims=True), jnp.max(s_c, axis=-1, keepdims=True))
        e_w = jnp.exp2(s_w - m)
        e_c = jnp.exp2(s_c - m)
        inv = 1.0 / (jnp.sum(e_w, axis=-1, keepdims=True) + jnp.sum(e_c, axis=-1, keepdims=True))
        o = (_dot(e_w.astype(BF16), v_ref[pl.ds(start, nk), sl])
             + _dot(e_c.astype(BF16), vc_ref[:, sl]))
        o_ref[:, sl] = (o * inv).astype(o_ref.dtype)


def _na_lat(lat, ctxp, rpb, bsz, seq, clen):
    n = bsz * seq
    rows = seq // GRID_W
    rpt = min(NA_ROWS_PER_TILE, rows)
    nkr, tile_variant, tile_kb, variants = _na_tables(rows, rpt)
    bias = _na_bias(rpb, variants)
    nt = rows // rpt
    rq, nk = rpt * GRID_W, nkr * GRID_W
    bw = BRANCH_WIDTH
    kern = functools.partial(_na_kernel, nk=nk)
    return pl.pallas_call(
        kern,
        grid_spec=pltpu.PrefetchScalarGridSpec(
            num_scalar_prefetch=2,
            grid=(bsz, nt),
            in_specs=[pl.BlockSpec((rq, bw), lambda b, t, var, kb: (b * nt + t, lat["nq"][1])),
                      pl.BlockSpec((seq, bw), lambda b, t, var, kb: (b, lat["nk"][1])),
                      pl.BlockSpec((seq, bw), lambda b, t, var, kb: (b, lat["nv"][1])),
                      pl.BlockSpec((clen, bw), lambda b, t, var, kb: (b, ctxp["nk"][1])),
                      pl.BlockSpec((clen, bw), lambda b, t, var, kb: (b, ctxp["nv"][1])),
                      pl.BlockSpec((None, NA_HEADS, rq, nk), lambda b, t, var, kb: (var[t], 0, 0, 0))],
            out_specs=pl.BlockSpec((rq, bw), lambda b, t, var, kb: (b * nt + t, 0))),
        out_shape=jax.ShapeDtypeStruct((n, bw), BF16),
        compiler_params=_params(("parallel", "arbitrary")),
        name="na_attn",
    )(jnp.asarray(tile_variant), jnp.asarray(tile_kb), lat["nq"][0], lat["nk"][0], lat["nv"][0],
      ctxp["nk"][0], ctxp["nv"][0], bias)


def _na_ctx_kernel(q_ref, k_ref, v_ref, o_ref):
    for h in range(NA_HEADS):
        sl = slice(h * NA_HEAD_DIM, (h + 1) * NA_HEAD_DIM)
        s = _dot_nt(q_ref[:, sl], k_ref[:, sl])
        e = jnp.exp2(s - jnp.max(s, axis=-1, keepdims=True))
        inv = 1.0 / jnp.sum(e, axis=-1, keepdims=True)
        o_ref[:, sl] = (_dot(e.astype(BF16), v_ref[:, sl]) * inv).astype(o_ref.dtype)


def _na_ctx(ctxp, bsz, clen):
    bw = BRANCH_WIDTH
    return pl.pallas_call(
        _na_ctx_kernel,
        grid=(bsz,),
        in_specs=[pl.BlockSpec((clen, bw), lambda b: (b, ctxp["nq"][1])),
                  pl.BlockSpec((clen, bw), lambda b: (b, ctxp["nk"][1])),
                  pl.BlockSpec((clen, bw), lambda b: (b, ctxp["nv"][1]))],
        out_specs=pl.BlockSpec((clen, bw), lambda b: (b, 0)),
        out_shape=jax.ShapeDtypeStruct((bsz * clen, bw), BF16),
        compiler_params=_params(("parallel",)),
        name="na_attn_ctx",
    )(ctxp["nq"][0], ctxp["nk"][0], ctxp["nv"][0])


def _merge_kernel(yd_ref, yl_ref, yn_ref, g0_ref, g1_ref, g2_ref, wb_ref, wo_ref, x_ref,
                  n1_ref, gate_ref, n2_ref, sc_ref, sh_ref, xm_ref, h2_ref):
    m = (g0_ref[...].astype(F32) * _dot(yd_ref[...], wb_ref[0])
         + g1_ref[...].astype(F32) * _dot(yl_ref[...], wb_ref[1])
         + g2_ref[...].astype(F32) * _dot(yn_ref[...], wb_ref[2]))
    mo = _dot(m.astype(BF16), wo_ref[...])
    xm = x_ref[...] + gate_ref[...] * _rms(mo, n1_ref[...])
    xm_ref[...] = xm
    h2 = _rms(xm, n2_ref[...]) * (1.0 + sc_ref[...]) + sh_ref[...]
    h2_ref[...] = h2.astype(h2_ref.dtype)


def _merge(yd, yl, yn, gates, wb, wo, x, n1, n2, mod, row_fn, tm, h2_dtype):
    n, d = x.shape
    bw = BRANCH_WIDTH
    p, gcol = gates
    resident = dict(pipeline_mode=pl.Buffered(1))
    return pl.pallas_call(
        _merge_kernel,
        grid=(n // tm,),
        in_specs=[pl.BlockSpec((tm, bw), lambda i: (i, 0)),
                  pl.BlockSpec((tm, bw), lambda i: (i, 0)),
                  pl.BlockSpec((tm, bw), lambda i: (i, 0)),
                  pl.BlockSpec((tm, d), lambda i: (i, gcol)),
                  pl.BlockSpec((tm, d), lambda i: (i, gcol + 1)),
                  pl.BlockSpec((tm, d), lambda i: (i, gcol + 2)),
                  pl.BlockSpec((3, bw, d), lambda i: (0, 0, 0), **resident),
                  pl.BlockSpec((d, d), lambda i: (0, 0), **resident),
                  pl.BlockSpec((tm, d), lambda i: (i, 0)),
                  _vec_spec(d),
                  _mod_spec(d, M_G1, row_fn),
                  _vec_spec(d),
                  _mod_spec(d, M_SC2, row_fn),
                  _mod_spec(d, M_SH2, row_fn)],
        out_specs=[pl.BlockSpec((tm, d), lambda i: (i, 0)),
                   pl.BlockSpec((tm, d), lambda i: (i, 0))],
        out_shape=[jax.ShapeDtypeStruct((n, d), F32),
                   jax.ShapeDtypeStruct((n, d), h2_dtype)],
        compiler_params=_params(("parallel",)),
        name="merge",
    )(yd, yl, yn, p, p, p, wb, wo, x, n1.reshape(1, d), mod, n2.reshape(1, d), mod, mod)


def _swiglu_step(x_ref, wg_ref, wu_ref, wd_ref, acc_ref, f):
    @pl.when(f == 0)
    def _():
        acc_ref[...] = jnp.zeros_like(acc_ref)
    xb = x_ref[...].astype(BF16)
    g = _dot(xb, wg_ref[...])
    u = _dot(xb, wu_ref[...])
    a = (g * jax.nn.sigmoid(g) * u).astype(BF16)
    acc_ref[...] += _dot(a, wd_ref[...])


def _ffn_dense_kernel(x_ref, wg_ref, wu_ref, wd_ref, xm_ref, gate_ref, n3_ref, *rest, nf, with_next):
    if with_next:
        n0_ref, sc_ref, sh_ref, xo_ref, hn_ref, acc_ref = rest
    else:
        xo_ref, acc_ref = rest
    f = pl.program_id(1)
    _swiglu_step(x_ref, wg_ref, wu_ref, wd_ref, acc_ref, f)

    @pl.when(f == nf - 1)
    def _():
        xo = xm_ref[...] + gate_ref[...] * _rms(acc_ref[...], n3_ref[...])
        xo_ref[...] = xo
        if with_next:
            hn = _rms(xo, n0_ref[...]) * (1.0 + sc_ref[...]) + sh_ref[...]
            hn_ref[...] = hn.astype(hn_ref.dtype)


def _ffn_dense(h2, wg, wu, wd, xm, n3, mod, row_fn, tm, tf, nxt):
    n, d = h2.shape
    fp = wg.shape[1]
    nf = fp // tf
    with_next = nxt is not None
    in_specs = [pl.BlockSpec((tm, d), lambda i, f: (i, 0)),
                pl.BlockSpec((d, tf), lambda i, f: (0, f)),
                pl.BlockSpec((d, tf), lambda i, f: (0, f)),
                pl.BlockSpec((tf, d), lambda i, f: (f, 0)),
                pl.BlockSpec((tm, d), lambda i, f: (i, 0)),
                _mod_spec(d, M_G2, row_fn),
                _vec_spec(d)]
    args = [h2, wg, wu, wd, xm, mod, n3.reshape(1, d)]
    out_specs = [pl.BlockSpec((tm, d), lambda i, f: (i, 0))]
    out_shape = [jax.ShapeDtypeStruct((n, d), F32)]
    if with_next:
        n0, mod_next = nxt
        in_specs += [_vec_spec(d), _mod_spec(d, M_SC1, row_fn), _mod_spec(d, M_SH1, row_fn)]
        args += [n0.reshape(1, d), mod_next, mod_next]
        out_specs.append(pl.BlockSpec((tm, d), lambda i, f: (i, 0)))
        out_shape.append(jax.ShapeDtypeStruct((n, d), BF16))
    kern = functools.partial(_ffn_dense_kernel, nf=nf, with_next=with_next)
    res = pl.pallas_call(
        kern,
        grid=(n // tm, nf),
        in_specs=in_specs,
        out_specs=out_specs,
        out_shape=out_shape,
        scratch_shapes=[pltpu.VMEM((tm, d), F32)],
        compiler_params=_params(("parallel", "arbitrary")),
        name="ffn_dense",
    )(*args)
    return (res[0], res[1]) if with_next else (res[0], None)


def _row_copy(src_hbm, src_row, dst_ref, dst_row, sem):
    return pltpu.make_async_copy(src_hbm.at[pl.ds(src_row, 1)], dst_ref.at[pl.ds(dst_row, 1)], sem)


def _ffn_grouped_kernel(te_ref, nt_ref, src_ref, x_hbm, wg_ref, wu_ref, wd_ref, o_ref,
                        rows_ref, xb_ref, sem, *, nf, tm, grid_tiles):
    i = pl.program_id(0)
    f = pl.program_id(1)
    nt = nt_ref[0]
    slot = i % 2
    issue_steps = max(1, nf - 2)
    rows_per_step = tm // issue_steps

    def start_row(tile, slot_, r):
        _row_copy(x_hbm, src_ref[tile * tm + r], rows_ref.at[slot_], r, sem.at[slot_]).start()

    def wait_rows(slot_):
        pltpu.make_async_copy(x_hbm.at[pl.ds(0, tm)], rows_ref.at[slot_], sem.at[slot_]).wait()

    @pl.when((i == 0) & (f == 0))
    def _():
        def body(r, c):
            start_row(0, 0, r)
            return c
        lax.fori_loop(0, tm, body, 0)

    @pl.when((i < nt) & (f == 0))
    def _():
        wait_rows(slot)
        xb_ref[...] = rows_ref[slot].astype(BF16)
        o_ref[...] = jnp.zeros_like(o_ref)

    def step(request_rows):
        if request_rows:
            nxt = jnp.minimum(i + 1, grid_tiles - 1)
            for j in range(rows_per_step):
                start_row(nxt, 1 - slot, f * rows_per_step + j)
        xb = xb_ref[...]
        g = _dot(xb, wg_ref[...])
        u = _dot(xb, wu_ref[...])
        a = (g * jax.nn.sigmoid(g) * u).astype(BF16)
        o_ref[...] += _dot(a, wd_ref[...])

    pl.when((i < nt) & (f < issue_steps))(lambda: step(True))
    if issue_steps < nf:
        pl.when((i < nt) & (f >= issue_steps))(lambda: step(False))

    @pl.when((i == nt - 1) & (f == nf - 1))
    def _():
        wait_rows(1 - slot)

    @pl.when((i >= nt) & (f == nf - 1))
    def _():
        o_ref[...] = jnp.zeros_like(o_ref)


def _ffn_grouped(h2, src, wg, wu, wd, tile_expert, n_tiles, p_max, tm, tf):
    d = h2.shape[1]
    fe = wg.shape[2]
    nf = fe // tf
    assert tm % max(1, nf - 2) == 0, (tm, nf)

    def fidx(i, f, nt):
        return jnp.where(i < nt[0], f, nf - 1)

    kern = functools.partial(_ffn_grouped_kernel, nf=nf, tm=tm, grid_tiles=p_max // tm)
    return pl.pallas_call(
        kern,
        grid_spec=pltpu.PrefetchScalarGridSpec(
            num_scalar_prefetch=3,
            grid=(p_max // tm, nf),
            in_specs=[pl.BlockSpec(memory_space=pl.ANY),
                      pl.BlockSpec((None, d, tf), lambda i, f, te, nt, src: (te[i], 0, fidx(i, f, nt))),
                      pl.BlockSpec((None, d, tf), lambda i, f, te, nt, src: (te[i], 0, fidx(i, f, nt))),
                      pl.BlockSpec((None, tf, d), lambda i, f, te, nt, src: (te[i], fidx(i, f, nt), 0))],
            out_specs=pl.BlockSpec((tm, d), lambda i, f, te, nt, src: (i, 0)),
            scratch_shapes=[pltpu.VMEM((2, tm, d), h2.dtype),
                            pltpu.VMEM((tm, d), BF16),
                            pltpu.SemaphoreType.DMA((2,))]),
        out_shape=jax.ShapeDtypeStruct((p_max, d), F32),
        compiler_params=pltpu.CompilerParams(dimension_semantics=("arbitrary", "arbitrary"),
                                             vmem_limit_bytes=VMEM_LIMIT,
                                             disable_bounds_checks=True),
        name="ffn_grouped",
    )(tile_expert, n_tiles, src, h2, wg, wu, wd)


def _router_kernel(h_ref, wr_ref, o_ref, cnt_ref, carry_ref, *, n_experts):
    i = pl.program_id(0)

    @pl.when(i == 0)
    def _():
        carry_ref[...] = jnp.zeros_like(carry_ref)

    logits = _dot(h_ref[...].astype(BF16), wr_ref[...])
    tm = logits.shape[0]
    lane_i = lax.broadcasted_iota(I32, logits.shape, 1)
    lane = lane_i.astype(F32)
    logits = jnp.where(lane_i < n_experts, logits, -jnp.inf)
    m1 = jnp.max(logits, axis=-1, keepdims=True)
    i1 = jnp.min(jnp.where(logits == m1, lane, float(LANES)), axis=-1, keepdims=True)
    rest = jnp.where(lane == i1, -jnp.inf, logits)
    m2 = jnp.max(rest, axis=-1, keepdims=True)
    i2 = jnp.min(jnp.where(rest == m2, lane, float(LANES)), axis=-1, keepdims=True)
    e21 = jnp.exp(m2 - m1)
    w1 = 1.0 / (1.0 + e21)
    w2 = e21 * w1

    sel1 = lane == i1
    sel2 = lane == i2
    onehot = jnp.where(sel1 | sel2, 1.0, 0.0).astype(BF16)
    r = lax.broadcasted_iota(I32, (tm, tm), 0)
    c = lax.broadcasted_iota(I32, (tm, tm), 1)
    strict_lower = jnp.where(c < r, 1.0, 0.0).astype(BF16)
    before = _dot(strict_lower, onehot) + carry_ref[0:1, :]
    r1 = jnp.sum(jnp.where(sel1, before, 0.0), axis=-1, keepdims=True)
    r2 = jnp.sum(jnp.where(sel2, before, 0.0), axis=-1, keepdims=True)
    total = carry_ref[0:1, :] + jnp.sum(onehot.astype(F32), axis=0, keepdims=True)
    carry_ref[...] = jnp.broadcast_to(total, carry_ref.shape)
    cnt_ref[...] = jnp.broadcast_to(total, cnt_ref.shape)

    packed = jnp.where(lane_i == 0, i1, 0.0)
    packed = jnp.where(lane_i == 1, i2, packed)
    packed = jnp.where(lane_i == 2, r1, packed)
    packed = jnp.where(lane_i == 3, r2, packed)
    packed = jnp.where(lane_i == 4, w1, packed)
    packed = jnp.where(lane_i == 5, w2, packed)
    o_ref[...] = packed


def _router(h2, w_router, tm):
    n, d = h2.shape
    n_experts = w_router.shape[1]
    wr = jnp.pad(w_router, ((0, 0), (0, LANES - n_experts))).astype(BF16)
    kern = functools.partial(_router_kernel, n_experts=n_experts)
    return pl.pallas_call(
        kern,
        grid=(n // tm,),
        in_specs=[pl.BlockSpec((tm, d), lambda i: (i, 0)),
                  pl.BlockSpec((d, LANES), lambda i: (0, 0))],
        out_specs=[pl.BlockSpec((tm, LANES), lambda i: (i, 0)),
                   pl.BlockSpec((SUBLANES, LANES), lambda i: (0, 0))],
        out_shape=[jax.ShapeDtypeStruct((n, LANES), F32),
                   jax.ShapeDtypeStruct((SUBLANES, LANES), F32)],
        scratch_shapes=[pltpu.VMEM((SUBLANES, LANES), F32)],
        compiler_params=_params(("arbitrary",)),
        name="router",
    )(h2, wr)


def _combine_kernel(d1_ref, d2_ref, y_hbm, r_ref, xm_ref, gate_ref, n3_ref, xo_ref, buf_ref, sem, *,
                    tm, n_tiles):
    i = pl.program_id(0)
    slot = i % 2
    dest = (d1_ref, d2_ref)

    def start_rows(tile, slot_, r):
        for k in range(TOP_K):
            _row_copy(y_hbm, dest[k][tile * tm + r], buf_ref.at[slot_, k], r, sem.at[slot_]).start()

    def wait_rows(slot_):
        for k in range(TOP_K):
            pltpu.make_async_copy(y_hbm.at[pl.ds(0, tm)], buf_ref.at[slot_, k], sem.at[slot_]).wait()

    @pl.when(i == 0)
    def _():
        def body(r, c):
            start_rows(0, 0, r)
            return c
        lax.fori_loop(0, tm, body, 0)

    wait_rows(slot)
    nxt = jnp.minimum(i + 1, n_tiles - 1)
    for r in range(tm):
        start_rows(nxt, 1 - slot, r)
    route = r_ref[...]
    y = route[:, 4:5] * buf_ref[slot, 0] + route[:, 5:6] * buf_ref[slot, 1]
    xo_ref[...] = xm_ref[...] + gate_ref[...] * _rms(y, n3_ref[...])

    @pl.when(i == n_tiles - 1)
    def _():
        wait_rows(1 - slot)


def _combine(yo, dest1, dest2, route, xm, n3, mod, row_fn, tm):
    n, d = xm.shape
    kern = functools.partial(_combine_kernel, tm=tm, n_tiles=n // tm)
    return pl.pallas_call(
        kern,
        grid_spec=pltpu.PrefetchScalarGridSpec(
            num_scalar_prefetch=2,
            grid=(n // tm,),
            in_specs=[pl.BlockSpec(memory_space=pl.ANY),
                      pl.BlockSpec((tm, LANES), lambda i, *_: (i, 0)),
                      pl.BlockSpec((tm, d), lambda i, *_: (i, 0)),
                      _mod_spec(d, M_G2, row_fn),
                      _vec_spec(d)],
            out_specs=pl.BlockSpec((tm, d), lambda i, *_: (i, 0)),
            scratch_shapes=[pltpu.VMEM((2, TOP_K, tm, d), F32), pltpu.SemaphoreType.DMA((2,))]),
        out_shape=jax.ShapeDtypeStruct((n, d), F32),
        compiler_params=pltpu.CompilerParams(dimension_semantics=("arbitrary",),
                                             vmem_limit_bytes=VMEM_LIMIT,
                                             disable_bounds_checks=True),
        name="moe_combine",
    )(dest1, dest2, yo, route, xm, mod, n3.reshape(1, d))


def _moe(h2, xm, w_router, wg, wu, wd, n3, mod, row_fn, tm_g, tf):
    n, d = h2.shape
    n_experts = w_router.shape[1]
    route, counts = _router(h2, w_router, min(512, n))
    e1 = route[:, 0].astype(I32)
    e2 = route[:, 1].astype(I32)
    r1 = route[:, 2].astype(I32)
    r2 = route[:, 3].astype(I32)
    cnt = counts[0, :n_experts].astype(I32)
    padded = ((cnt + tm_g - 1) // tm_g) * tm_g
    ends = jnp.cumsum(padded)
    offs = ends - padded
    dest1 = offs[e1] + r1
    dest2 = offs[e2] + r2
    p_max = ((TOP_K * n + n_experts * (tm_g - 1)) // tm_g) * tm_g
    n_rows = ends[-1:]
    tok = jnp.arange(n, dtype=I32)
    src = jnp.zeros((p_max,), I32).at[jnp.concatenate([dest1, dest2])].set(
        jnp.concatenate([tok, tok]), unique_indices=True)
    tile_start = jnp.arange(p_max // tm_g, dtype=I32) * tm_g
    tile_expert = jnp.minimum(jnp.sum(tile_start[:, None] >= ends[None, :], axis=1),
                              n_experts - 1).astype(I32)
    last_expert = tile_expert[jnp.maximum(n_rows[0] // tm_g - 1, 0)]
    tile_expert = jnp.where(tile_start < n_rows[0], tile_expert, last_expert)
    yo = _ffn_grouped(h2, src, wg, wu, wd, tile_expert, n_rows // tm_g, p_max, tm_g, tf)
    return _combine(yo, dest1, dest2, route, xm, n3, mod, row_fn, min(256, n))


def _rope_tables(seq):
    inv = ROPE_BASE ** (-jnp.arange(ROPE_FREQ, dtype=F32) / ROPE_FREQ)
    t = jnp.arange(seq, dtype=I32)
    pos = jnp.stack([t // GRID_W, t % GRID_W], axis=-1).astype(F32)
    ang = pos[:, :, None] * inv
    cos, sin = jnp.cos(ang), jnp.sin(ang)
    cos = jnp.concatenate([cos, cos], axis=-1).reshape(seq, DIFF_HEAD_DIM)
    sin = jnp.concatenate([-sin, sin], axis=-1).reshape(seq, DIFF_HEAD_DIM)
    reps = LANES // DIFF_HEAD_DIM
    return jnp.tile(cos, (1, reps)), jnp.tile(sin, (1, reps))


def _pad_cols(w, mult):
    pad = (-w.shape[-1]) % mult
    return jnp.pad(w, [(0, 0)] * (w.ndim - 1) + [(0, pad)]) if pad else w


def _pad_rows(w, mult):
    pad = (-w.shape[-2]) % mult
    return jnp.pad(w, [(0, 0)] * (w.ndim - 2) + [(0, pad), (0, 0)]) if pad else w


def kernel(x, c, ctx, c_ctx, norm_g, w_ada, b_ada, w_in, diff_lambda, diff_subln_g, lru_conv_w, lru_conv_b, lru_gate_w, lru_gate_b, lru_lambda, na_rpb, w_branch, w_merge, b_merge, w_out, ffn_w_gate, ffn_w_up, ffn_w_down, moe_w_router, moe_w_gate, moe_w_up, moe_w_down):
    bsz, seq, d = x.shape
    clen = ctx.shape[1]
    depth = w_in.shape[0]
    n, nc = bsz * seq, bsz * clen
    x_lat = x.reshape(n, d)
    x_ctx = ctx.reshape(nc, d)

    mr = -(-(bsz + 1) // SUBLANES) * SUBLANES
    cvec = jnp.concatenate([c, c_ctx[None], jnp.zeros((mr - bsz - 1, d), F32)], axis=0)
    mod_all = _ada(cvec, w_ada, b_ada).reshape(depth, mr, 1, 6 * d)
    cos_t, sin_t = _rope_tables(seq)

    tm_lat, tm_ctx = min(1024, seq), min(1024, nc, seq)
    tm_mix_lat, tm_mix_ctx = min(256, seq), min(256, nc)
    tm_ffn_lat, tm_ffn_ctx = min(512, seq), min(512, nc)
    tf = min(512, d)
    tf_moe = min(MOE_TF, d)
    tq = min(DIFF_TQ, seq)

    def lat_row(tile_rows):
        per_batch = seq // tile_rows
        return lambda i: i // per_batch

    def ctx_row(i):
        return bsz

    h_lat = _prenorm(x_lat, norm_g[0, 0], mod_all[0], lat_row(tm_lat), tm_lat)
    h_ctx = _prenorm(x_ctx, norm_g[0, 0], mod_all[0], ctx_row, tm_ctx)

    rest_names = ("dv", "lx", "lg", "nq", "nk", "nv")
    part_scale = [1.0] * N_IN_PARTS
    part_scale[P_DQ], part_scale[P_NQ] = DIFF_QSCALE, NA_QSCALE
    col_scale = jnp.repeat(jnp.asarray(part_scale, F32), BRANCH_WIDTH)[None]
    scale_qk, scale_rest = col_scale[:, :2 * BRANCH_WIDTH], col_scale[:, 2 * BRANCH_WIDTH:]
    diff_steps = bsz * DIFF_HEADS * (seq // tq)
    moe_src = (moe_w_gate, moe_w_up, moe_w_down)

    def halves(w):
        return w.reshape(2, -1, w.shape[-1])

    def side_cast_ok(mi):
        return moe_w_router.shape[-1] % 2 == 0 and all(
            halves(w[mi]).shape[1] % (16 * diff_steps) == 0 for w in moe_src)

    moe_bf16 = {}
    for l in range(depth):
        need_ctx = l < depth - 1
        mod = mod_all[l]
        lam_init = 0.8 - 0.6 * math.exp(-0.3 * l)
        w_qk = w_in[l][:, :2 * BRANCH_WIDTH].astype(BF16)
        w_rest = w_in[l][:, 2 * BRANCH_WIDTH:].astype(BF16)
        w_gate = w_merge[l].astype(BF16)
        b_gate = b_merge[l][None]

        def project(h, tm, rope):
            qk = _proj(h, w_qk, scale_qk, cos_t, sin_t, seq, tm, PROJ_TN_ROPE if rope else PROJ_TN,
                       "rope" if rope else "plain")
            rest = _proj(h, w_rest, scale_rest, cos_t, sin_t, seq, tm, PROJ_TN, "plain")
            parts = {"dq": (qk, 0), "dk": (qk, 1)}
            parts.update({name: (rest, k) for k, name in enumerate(rest_names)})
            return parts

        lat = project(h_lat, tm_lat, True)
        ctxp = project(h_ctx, tm_ctx, False)
        gate_lat = _proj(h_lat, w_gate, b_gate, cos_t, sin_t, seq, tm_lat, PROJ_TN, "gate")

        mi, half = l // 2, l % 2
        jobs = ()
        if mi < moe_w_gate.shape[0] and side_cast_ok(mi):
            jobs = tuple((halves(w[mi]), half, moe_bf16[mi][k] if half else None)
                         for k, w in enumerate(moe_src))
        y_diff, cast = _diff_attn_lat(lat, ctxp, diff_lambda[l], diff_subln_g[l], lam_init,
                                      bsz, seq, clen, tq, jobs)
        if jobs:
            moe_bf16[mi] = cast
        y_lru, y_lru_c = _lru(lat, ctxp, lru_conv_w[l], lru_conv_b[l], lru_gate_w[l],
                              lru_gate_b[l], lru_lambda[l], bsz, seq, clen, need_ctx)
        y_na = _na_lat(lat, ctxp, na_rpb[l], bsz, seq, clen)

        wb = w_branch[l].astype(BF16)
        wo = w_out[l].astype(BF16)
        is_moe = l % 2 == 1
        h2_dtype = F32 if is_moe else BF16
        xm_lat, h2_lat = _merge(y_diff, y_lru, y_na, (gate_lat, 0), wb, wo, x_lat, norm_g[l, 1],
                                norm_g[l, 2], mod, lat_row(tm_mix_lat), tm_mix_lat, h2_dtype)
        if need_ctx:
            y_diff_c = _diff_attn_ctx(ctxp, diff_lambda[l], diff_subln_g[l], lam_init, bsz, clen)
            y_na_c = _na_ctx(ctxp, bsz, clen)
            gate_ctx = _proj(h_ctx, w_gate, b_gate, cos_t, sin_t, seq, tm_ctx, PROJ_TN, "gate")
            xm_ctx, h2_ctx = _merge(y_diff_c, y_lru_c, y_na_c, (gate_ctx, 0), wb, wo, x_ctx,
                                    norm_g[l, 1], norm_g[l, 2], mod, ctx_row, tm_mix_ctx, h2_dtype)

        nxt = (norm_g[l + 1, 0], mod_all[l + 1]) if need_ctx else None
        i = l // 2
        if is_moe:
            if i in moe_bf16:
                wg, wu, wd = (c.reshape(w.shape[1:]) for c, w in zip(moe_bf16[i], moe_src))
            else:
                wg, wu, wd = (w[i].astype(BF16) for w in moe_src)
            tm_g = min(MOE_TILE, n)
            x_lat = _moe(h2_lat, xm_lat, moe_w_router[i], wg, wu, wd, norm_g[l, 3], mod,
                         lat_row(min(256, n)), tm_g, tf_moe)
            if need_ctx:
                x_ctx = _moe(h2_ctx, xm_ctx, moe_w_router[i], wg, wu, wd, norm_g[l, 3], mod,
                             ctx_row, min(MOE_TILE, nc), tf_moe)
                h_lat = _prenorm(x_lat, nxt[0], nxt[1], lat_row(tm_lat), tm_lat)
                h_ctx = _prenorm(x_ctx, nxt[0], nxt[1], ctx_row, tm_ctx)
        else:
            wg = _pad_cols(ffn_w_gate[i].astype(BF16), tf)
            wu = _pad_cols(ffn_w_up[i].astype(BF16), tf)
            wd = _pad_rows(ffn_w_down[i].astype(BF16), tf)
            x_lat, h_lat = _ffn_dense(h2_lat, wg, wu, wd, xm_lat, norm_g[l, 3], mod,
                                      lat_row(tm_ffn_lat), tm_ffn_lat, tf, nxt)
            if need_ctx:
                x_ctx, h_ctx = _ffn_dense(h2_ctx, wg, wu, wd, xm_ctx, norm_g[l, 3], mod,
                                          ctx_row, tm_ffn_ctx, tf, nxt)
    return x_lat.reshape(bsz, seq, d)
```

```python
import functools
import math

import numpy as np
import jax
import jax.numpy as jnp
from jax import lax
from jax.experimental import pallas as pl
from jax.experimental.pallas import tpu as pltpu

F32 = jnp.float32
BF16 = jnp.bfloat16
I32 = jnp.int32

EPS = 1e-6
GRID_W = 64
BRANCH_WIDTH = 1024
N_IN_PARTS = 8
IN_WIDTH = N_IN_PARTS * BRANCH_WIDTH
DIFF_HEADS = 8
DIFF_HEAD_DIM = 64
DIFF_V_DIM = 2 * DIFF_HEAD_DIM
ROPE_BASE = 10000.0
ROPE_FREQ = DIFF_HEAD_DIM // 4
LRU_BLOCKS = 8
LRU_BLOCK_DIM = BRANCH_WIDTH // LRU_BLOCKS
LRU_CONV_W = 4
LRU_C = 8.0
NA_HEADS = 8
NA_HEAD_DIM = BRANCH_WIDTH // NA_HEADS
NA_WIN_R = 8
NA_WIN_C = 16
TOP_K = 2
MOE_TILE = 560
MOE_TF = 1024
DIFF_TQ = 1024
PROJ_TN = 2048
PROJ_TN_ROPE = 1024

LANES = 128
SUBLANES = 8
VMEM_LIMIT = 56 * 1024 * 1024
NEG = -1e30
LOG2E = 1.0 / math.log(2.0)
DIFF_QSCALE = DIFF_HEAD_DIM ** -0.5 * LOG2E
NA_QSCALE = NA_HEAD_DIM ** -0.5 * LOG2E

P_DQ, P_DK, P_DV, P_LX, P_LG, P_NQ, P_NK, P_NV = range(8)
M_SH1, M_SC1, M_G1, M_SH2, M_SC2, M_G2 = range(6)


def _params(sem):
    return pltpu.CompilerParams(dimension_semantics=sem, vmem_limit_bytes=VMEM_LIMIT)


def _rms(x, g):
    return x * lax.rsqrt(jnp.mean(x * x, axis=-1, keepdims=True) + EPS) * g


def _dot(a, b):
    return jnp.dot(a, b, preferred_element_type=F32)


def _dot_nt(a, b):
    return lax.dot_general(a, b, (((1,), (1,)), ((), ())), preferred_element_type=F32)


def _ada_kernel(c_ref, w_ref, b_ref, o_ref):
    c = c_ref[...]
    s = (c * jax.nn.sigmoid(c)).astype(BF16)
    o_ref[...] = _dot(s, w_ref[...].astype(BF16)) + b_ref[...]


def _ada(cvec, w_ada, b_ada):
    n_layers, d, n6 = w_ada.shape
    mr = cvec.shape[0]
    tn = min(1024, d)
    return pl.pallas_call(
        _ada_kernel,
        grid=(n_layers, n6 // tn),
        in_specs=[pl.BlockSpec((mr, d), lambda l, j: (0, 0)),
                  pl.BlockSpec((None, d, tn), lambda l, j: (l, 0, j)),
                  pl.BlockSpec((None, 1, tn), lambda l, j: (l, 0, j))],
        out_specs=pl.BlockSpec((None, mr, tn), lambda l, j: (l, 0, j)),
        out_shape=jax.ShapeDtypeStruct((n_layers, mr, n6), F32),
        compiler_params=_params(("arbitrary", "arbitrary")),
        name="ada_mod",
    )(cvec, w_ada, b_ada.reshape(n_layers, 1, n6))


def _mod_spec(d, chunk, row_fn):
    return pl.BlockSpec((None, 1, d), lambda i, *_: (row_fn(i), 0, chunk))


def _vec_spec(d):
    return pl.BlockSpec((1, d), lambda i, *_: (0, 0))


def _prenorm_kernel(x_ref, g_ref, sc_ref, sh_ref, o_ref):
    y = _rms(x_ref[...], g_ref[...])
    o_ref[...] = (y * (1.0 + sc_ref[...]) + sh_ref[...]).astype(o_ref.dtype)


def _prenorm(x, g, mod, row_fn, tm):
    n, d = x.shape
    return pl.pallas_call(
        _prenorm_kernel,
        grid=(n // tm,),
        in_specs=[pl.BlockSpec((tm, d), lambda i: (i, 0)),
                  _vec_spec(d),
                  _mod_spec(d, M_SC1, row_fn),
                  _mod_spec(d, M_SH1, row_fn)],
        out_specs=pl.BlockSpec((tm, d), lambda i: (i, 0)),
        out_shape=jax.ShapeDtypeStruct((n, d), BF16),
        compiler_params=_params(("parallel",)),
        name="prenorm",
    )(x, g.reshape(1, d), mod, mod)


def _proj_kernel(h_ref, w_ref, v_ref, cos_ref, sin_ref, o_ref, *, kind):
    acc = _dot(h_ref[...], w_ref[...])
    tn = acc.shape[1]
    if kind == "gate":
        o_ref[...] = jax.nn.sigmoid(acc + v_ref[...]).astype(o_ref.dtype)
    elif kind == "rope":
        lane = lax.broadcasted_iota(I32, acc.shape, 1)
        first_half = (lane % (2 * ROPE_FREQ)) < ROPE_FREQ
        partner = jnp.where(first_half, pltpu.roll(acc, tn - ROPE_FREQ, 1),
                            pltpu.roll(acc, ROPE_FREQ, 1))
        cos = jnp.tile(cos_ref[...], (1, tn // LANES)) * v_ref[...]
        sin = jnp.tile(sin_ref[...], (1, tn // LANES)) * v_ref[...]
        o_ref[...] = (acc * cos + partner * sin).astype(o_ref.dtype)
    else:
        o_ref[...] = (acc * v_ref[...]).astype(o_ref.dtype)


def _proj(h, w, v, cos_t, sin_t, seq, tm, tn, kind):
    n, d = h.shape
    nw = w.shape[1]
    tn = min(tn, nw)
    tpb = seq // tm
    kern = functools.partial(_proj_kernel, kind=kind)
    return pl.pallas_call(
        kern,
        grid=(n // tm, nw // tn),
        in_specs=[pl.BlockSpec((tm, d), lambda i, j: (i, 0)),
                  pl.BlockSpec((d, tn), lambda i, j: (0, j)),
                  pl.BlockSpec((1, tn), lambda i, j: (0, j)),
                  pl.BlockSpec((tm, LANES), lambda i, j: (i % tpb, 0)),
                  pl.BlockSpec((tm, LANES), lambda i, j: (i % tpb, 0))],
        out_specs=pl.BlockSpec((tm, tn), lambda i, j: (i, j)),
        out_shape=jax.ShapeDtypeStruct((n, nw), BF16),
        compiler_params=_params(("parallel", "arbitrary")),
        name="proj_" + kind,
    )(h, w, v, cos_t, sin_t)


DIFF_KEY_CHUNK = 512


def _diff_kernel(*refs, lam_init, has_lat, n_cast, n_alias=0):
    n_in = 7 if has_lat else 5
    n_extra = n_cast + n_alias
    for src_ref, dst_ref in zip(refs[n_in:n_in + n_cast], refs[n_in + n_extra + 1:]):
        dst_ref[0] = src_ref[...].astype(dst_ref.dtype)
        if dst_ref.shape[0] == 2:
            dst_ref[1] = jnp.zeros(dst_ref.shape[1:], dst_ref.dtype)
    if has_lat:
        lamv_ref, sg_ref, q_ref, k_ref, v_ref, kc_ref, vc_ref = refs[:n_in]
    else:
        lamv_ref, sg_ref, q_ref, kc_ref, vc_ref = refs[:n_in]
    o_ref = refs[n_in + n_extra]
    lv = lamv_ref[...]
    lam = (jnp.exp(jnp.sum(lv[0:1] * lv[1:2], axis=-1, keepdims=True))
           - jnp.exp(jnp.sum(lv[2:3] * lv[3:4], axis=-1, keepdims=True)) + lam_init)
    q = q_ref[...]
    first = lax.broadcasted_iota(I32, q.shape, 1) < DIFF_HEAD_DIM
    zero = jnp.zeros_like(q)
    qz = (jnp.where(first, q, zero), jnp.where(first, zero, q))
    chunks = [(kc_ref, vc_ref, 0, kc_ref.shape[0])]
    if has_lat:
        kt = min(DIFF_KEY_CHUNK, k_ref.shape[0])
        chunks += [(k_ref, v_ref, j * kt, kt) for j in range(k_ref.shape[0] // kt)]

    def qk(c, chunk):
        k, _, off, n = chunk
        return _dot_nt(qz[c], k[off:off + n, :])

    def row_max(parts):
        m = jnp.max(parts[0], axis=-1, keepdims=True)
        for p in parts[1:]:
            m = jnp.maximum(m, jnp.max(p, axis=-1, keepdims=True))
        return m

    def softmax_av(s_parts, between=None):
        m = row_max(s_parts)
        den, out = 0.0, 0.0
        for j, (_, v, off, n) in enumerate(chunks):
            if between is not None:
                between(j)
            e = jnp.exp2(s_parts[j] - m)
            den = den + jnp.sum(e, axis=-1, keepdims=True)
            out = out + _dot(e.astype(BF16), v[off:off + n, :])
        return out, den

    s1 = [qk(0, ch) for ch in chunks]
    s2 = []
    o1, den1 = softmax_av(s1, between=lambda j: s2.append(qk(1, chunks[j])))
    o2, den2 = softmax_av(s2)
    o = o1 * (1.0 / den1) - o2 * (lam / den2)
    y = _rms(o, sg_ref[...]) * (1.0 - lam_init)
    o_ref[...] = y.astype(o_ref.dtype)


def _diff_attn_lat(lat, ctxp, lamv, subln_g, lam_init, bsz, seq, clen, tq, cast_jobs=()):
    nq = seq // tq
    n_steps = bsz * DIFF_HEADS * nq
    hw = DIFF_V_DIM
    hpb = BRANCH_WIDTH // hw

    def spec(rows, src, row_fn):
        blk = src[1] * hpb
        return pl.BlockSpec((rows, hw), lambda b, h, i: (row_fn(b, i), blk + h))

    def step(b, h, i):
        return (b * DIFF_HEADS + h) * nq + i

    n_base = 7
    cast_in, cast_out, cast_shapes, cast_args, aliases = [], [], [], [], {}
    for k, (arr, half, dst) in enumerate(cast_jobs):
        _, rows, cols = arr.shape
        per_step = rows // n_steps
        assert per_step * n_steps == rows and per_step % 16 == 0
        cast_in.append(pl.BlockSpec((None, per_step, cols), lambda b, h, i, half=half: (half, step(b, h, i), 0)))
        cast_args.append(arr)
        cast_shapes.append(jax.ShapeDtypeStruct(arr.shape, BF16))
        if dst is None:
            cast_out.append(pl.BlockSpec((2, per_step, cols), lambda b, h, i: (0, step(b, h, i), 0)))
        else:
            assert half == 1
            cast_out.append(pl.BlockSpec((1, per_step, cols), lambda b, h, i: (1, step(b, h, i), 0)))
    for k, (arr, half, dst) in enumerate(cast_jobs):
        if dst is not None:
            aliases[n_base + len(cast_jobs) + len(aliases)] = 1 + k
            cast_in.append(pl.BlockSpec(memory_space=pl.ANY))
            cast_args.append(dst)
    n_alias = len(aliases)
    kern = functools.partial(_diff_kernel, lam_init=lam_init, has_lat=True, n_cast=len(cast_jobs),
                             n_alias=n_alias)
    res = pl.pallas_call(
        kern,
        grid=(bsz, DIFF_HEADS, nq),
        in_specs=[pl.BlockSpec((4, DIFF_HEAD_DIM), lambda b, h, i: (0, 0)),
                  pl.BlockSpec((1, hw), lambda b, h, i: (0, 0)),
                  spec(tq, lat["dq"], lambda b, i: b * nq + i),
                  spec(seq, lat["dk"], lambda b, i: b),
                  spec(seq, lat["dv"], lambda b, i: b),
                  spec(clen, ctxp["dk"], lambda b, i: b),
                  spec(clen, ctxp["dv"], lambda b, i: b)] + cast_in,
        out_specs=[pl.BlockSpec((tq, hw), lambda b, h, i: (b * nq + i, h))] + cast_out,
        out_shape=[jax.ShapeDtypeStruct((bsz * seq, BRANCH_WIDTH), BF16)] + cast_shapes,
        input_output_aliases=aliases,
        compiler_params=_params(("parallel", "parallel", "arbitrary")),
        name="diff_attn",
    )(lamv, subln_g.reshape(1, hw), lat["dq"][0], lat["dk"][0], lat["dv"][0],
      ctxp["dk"][0], ctxp["dv"][0], *cast_args)
    return res[0], res[1:]


def _diff_attn_ctx(ctxp, lamv, subln_g, lam_init, bsz, clen):
    hw = DIFF_V_DIM
    hpb = BRANCH_WIDTH // hw

    def spec(src):
        blk = src[1] * hpb
        return pl.BlockSpec((clen, hw), lambda b, h: (b, blk + h))

    kern = functools.partial(_diff_kernel, lam_init=lam_init, has_lat=False, n_cast=0)
    return pl.pallas_call(
        kern,
        grid=(bsz, DIFF_HEADS),
        in_specs=[pl.BlockSpec((4, DIFF_HEAD_DIM), lambda b, h: (0, 0)),
                  pl.BlockSpec((1, hw), lambda b, h: (0, 0)),
                  spec(ctxp["dq"]), spec(ctxp["dk"]), spec(ctxp["dv"])],
        out_specs=pl.BlockSpec((clen, hw), lambda b, h: (b, h)),
        out_shape=jax.ShapeDtypeStruct((bsz * clen, BRANCH_WIDTH), BF16),
        compiler_params=_params(("parallel", "parallel")),
        name="diff_attn_ctx",
    )(lamv, subln_g.reshape(1, hw), ctxp["dq"][0], ctxp["dk"][0], ctxp["dv"][0])


LRU_TW = 256
LRU_UNROLL = 4


def _gelu_tanh(x):
    return 0.5 * x * (1.0 + jnp.tanh(math.sqrt(2.0 / math.pi) * (x + 0.044715 * (x * x * x))))


def _tile_scan(a, b, reverse):
    rows = lax.broadcasted_iota(I32, a.shape, 0)
    for s in (1, 2, 4):
        if reverse:
            a_s = pltpu.roll(a, SUBLANES - s, 0)
            b_s = pltpu.roll(b, SUBLANES - s, 0)
            valid = rows < SUBLANES - s
        else:
            a_s = pltpu.roll(a, s, 0)
            b_s = pltpu.roll(b, s, 0)
            valid = rows >= s
        b = jnp.where(valid, a * b_s + b, b)
        a = jnp.where(valid, a * a_s, a)
    return a, b


def _lru_kernel(xl_ref, gl_ref, xc_ref, gc_ref, cw_ref, cb_ref, gw_ref, gb_ref, lam_ref,
                *out_and_scratch, seq, clen, need_ctx):
    if need_ctx:
        yl_ref, yc_ref, af_ref, bf_ref, ab_ref, bb_ref, hf_ref = out_and_scratch
    else:
        yl_ref, af_ref, bf_ref, ab_ref, bb_ref, hf_ref = out_and_scratch
        yc_ref = None
    tot = clen + seq
    x = jnp.concatenate([xc_ref[...], xl_ref[...]], axis=0).astype(F32)
    row = lax.broadcasted_iota(I32, x.shape, 0)
    pos = jnp.where(row < clen, row, row - clen)
    seg_len = jnp.where(row < clen, clen, seq)
    cw = cw_ref[...]
    conv = cb_ref[...] + x * cw[2:3]
    conv = conv + jnp.where(pos >= 2, pltpu.roll(x, 2, 0), 0.0) * cw[0:1]
    conv = conv + jnp.where(pos >= 1, pltpu.roll(x, 1, 0), 0.0) * cw[1:2]
    conv = conv + jnp.where(pos < seg_len - 1, pltpu.roll(x, tot - 1, 0), 0.0) * cw[3:4]
    xb = conv.astype(BF16)
    gb = gb_ref[...]
    lam = lam_ref[...]
    for d, (a_ref, b_ref) in enumerate(((af_ref, bf_ref), (ab_ref, bb_ref))):
        r = jax.nn.sigmoid(_dot(xb, gw_ref[d, 0]) + gb[2 * d:2 * d + 1])
        i = jax.nn.sigmoid(_dot(xb, gw_ref[d, 1]) + gb[2 * d + 1:2 * d + 2])
        z = -lam[d:d + 1]
        softplus = jnp.maximum(z, 0.0) + jnp.log1p(jnp.exp(-jnp.abs(z)))
        log_a = -LRU_C * r * softplus
        a = jnp.exp(log_a)
        beta = jnp.sqrt(-jnp.tanh(log_a) * (a * a + 1.0))
        a_ref[...] = a
        b_ref[...] = beta * i * conv

    n_ct = clen // SUBLANES
    n_t = tot // SUBLANES
    w = x.shape[1]
    assert n_ct % LRU_UNROLL == 0 and n_t % LRU_UNROLL == 0
    rows_per_iter = LRU_UNROLL * SUBLANES

    def fwd_step(t, carry):
        base = pl.multiple_of(t * rows_per_iter, rows_per_iter)
        scans = [_tile_scan(af_ref[pl.ds(base + k * SUBLANES, SUBLANES), :],
                            bf_ref[pl.ds(base + k * SUBLANES, SUBLANES), :], False)
                 for k in range(LRU_UNROLL)]
        for k, (a, b) in enumerate(scans):
            h = a * carry + b
            hf_ref[pl.ds(base + k * SUBLANES, SUBLANES), :] = h
            carry = h[SUBLANES - 1:SUBLANES, :]
        return carry

    lax.fori_loop(0, n_t // LRU_UNROLL, fwd_step, jnp.zeros((1, w), F32))

    def bwd_iter(first_tile, n_tiles, g_ref, y_ref, seg_off):
        def step(t, carry):
            base = pl.multiple_of((first_tile + n_tiles) * SUBLANES - (t + 1) * rows_per_iter,
                                  rows_per_iter)
            offs = [base + k * SUBLANES for k in reversed(range(LRU_UNROLL))]
            scans = [_tile_scan(ab_ref[pl.ds(off, SUBLANES), :], bb_ref[pl.ds(off, SUBLANES), :], True)
                     for off in offs]
            for off, (a, b) in zip(offs, scans):
                h = a * carry + b
                carry = h[0:1, :]
                if y_ref is not None:
                    g = g_ref[pl.ds(off - seg_off, SUBLANES), :].astype(F32)
                    y_ref[pl.ds(off - seg_off, SUBLANES), :] = (
                        (hf_ref[pl.ds(off, SUBLANES), :] + h) * _gelu_tanh(g)).astype(y_ref.dtype)
            return carry
        return step

    carry = lax.fori_loop(0, n_ct // LRU_UNROLL, bwd_iter(0, n_ct, gc_ref, yc_ref, 0),
                          jnp.zeros((1, w), F32))
    lax.fori_loop(0, (n_t - n_ct) // LRU_UNROLL, bwd_iter(n_ct, n_t - n_ct, gl_ref, yl_ref, clen), carry)


def _lru(lat, ctxp, conv_w, conv_b, gate_w, gate_b, lru_lam, bsz, seq, clen, need_ctx):
    n = bsz * seq
    tw = LRU_TW
    nwt = BRANCH_WIDTH // tw
    per = tw // LRU_BLOCK_DIM
    gw = gate_w.reshape(2, 2, nwt, per, LRU_BLOCK_DIM, LRU_BLOCK_DIM)
    eye = jnp.eye(per, dtype=gate_w.dtype)
    gw = jnp.einsum('dgnpcf,pq->dgnpcqf', gw, eye).reshape(2, 2, nwt, tw, tw).astype(BF16)
    gb = gate_b.reshape(4, BRANCH_WIDTH)
    tot = seq + clen
    kern = functools.partial(_lru_kernel, seq=seq, clen=clen, need_ctx=need_ctx)

    def part_spec(rows, src):
        blk = src[1] * nwt
        return pl.BlockSpec((rows, tw), lambda b, j: (b, blk + j))

    out_shape = [jax.ShapeDtypeStruct((n, BRANCH_WIDTH), BF16)]
    out_specs = [pl.BlockSpec((seq, tw), lambda b, j: (b, j))]
    if need_ctx:
        out_shape.append(jax.ShapeDtypeStruct((bsz * clen, BRANCH_WIDTH), BF16))
        out_specs.append(pl.BlockSpec((clen, tw), lambda b, j: (b, j)))
    res = pl.pallas_call(
        kern,
        grid=(bsz, nwt),
        in_specs=[part_spec(seq, lat["lx"]), part_spec(seq, lat["lg"]),
                  part_spec(clen, ctxp["lx"]), part_spec(clen, ctxp["lg"]),
                  pl.BlockSpec((LRU_CONV_W, tw), lambda b, j: (0, j)),
                  pl.BlockSpec((1, tw), lambda b, j: (0, j)),
                  pl.BlockSpec((2, 2, None, tw, tw), lambda b, j: (0, 0, j, 0, 0)),
                  pl.BlockSpec((4, tw), lambda b, j: (0, j)),
                  pl.BlockSpec((2, tw), lambda b, j: (0, j))],
        out_specs=out_specs,
        out_shape=out_shape,
        scratch_shapes=[pltpu.VMEM((tot, tw), F32) for _ in range(5)],
        compiler_params=_params(("parallel", "parallel")),
        name="rglru",
    )(lat["lx"][0], lat["lg"][0], ctxp["lx"][0], ctxp["lg"][0], conv_w,
      conv_b.reshape(1, BRANCH_WIDTH), gw, gb, lru_lam)
    return (res[0], res[1]) if need_ctx else (res[0], None)


NA_ROWS_PER_TILE = 4


def _na_tables(rows, rpt):
    kr = min(NA_WIN_R, rows)
    nkr = min(rows, rpt + kr - 1)
    variants, index, tile_variant, tile_kb = [], {}, [], []
    for t in range(rows // rpt):
        r0 = t * rpt
        kb = int(np.clip(r0 - kr // 2, 0, rows - nkr))
        geom = []
        for a in range(rpt):
            r = r0 + a
            rs = int(np.clip(r - kr // 2, 0, rows - kr))
            geom.append(tuple((kb + k - r + NA_WIN_R - 1) if rs <= kb + k < rs + kr else None
                              for k in range(nkr)))
        geom = tuple(geom)
        if geom not in index:
            index[geom] = len(variants)
            variants.append(geom)
        tile_variant.append(index[geom])
        tile_kb.append(kb)
    return nkr, np.array(tile_variant, np.int32), np.array(tile_kb, np.int32), variants


def _na_bias(rpb, variants):
    heads = rpb.shape[0]
    pad = GRID_W - NA_WIN_C
    padded = jnp.pad(rpb * LOG2E, ((0, 0), (0, 0), (pad, pad)))
    toep = jnp.stack([padded[..., GRID_W - 1 - qc:2 * GRID_W - 1 - qc] for qc in range(GRID_W)],
                     axis=2)
    col = np.arange(GRID_W)
    cs = np.clip(col - NA_WIN_C // 2, 0, GRID_W - NA_WIN_C)
    in_cols = (col[None, :] >= cs[:, None]) & (col[None, :] < cs[:, None] + NA_WIN_C)
    toep = jnp.where(in_cols, toep, NEG).astype(F32)
    outside = jnp.full((heads, GRID_W, GRID_W), NEG, F32)
    return jnp.stack([
        jnp.concatenate([
            jnp.concatenate([outside if dr is None else toep[:, dr] for dr in row], axis=-1)
            for row in geom], axis=-2)
        for geom in variants])


def _na_kernel(var_ref, kb_ref, q_ref, k_ref, v_ref, kc_ref, vc_ref, bias_ref, o_ref, *, nk):
    t = pl.program_id(1)
    start = pl.multiple_of(kb_ref[t] * GRID_W, GRID_W)
    for h in range(NA_HEADS):
        sl = slice(h * NA_HEAD_DIM, (h + 1) * NA_HEAD_DIM)
        qh = q_ref[:, sl]
        s_w = _dot_nt(qh, k_ref[pl.ds(start, nk), sl]) + bias_ref[h]
        s_c = _dot_nt(qh, kc_ref[:, sl])
        m = jnp.maximum(jnp.max(s_w, axis=-1, keepdims=True), jnp.max(s_c, axis=-1, keepdims=True))
        e_w = jnp.exp2(s_w - m)
        e_c = jnp.exp2(s_c - m)
        inv = 1.0 / (jnp.sum(e_w, axis=-1, keepdims=True) + jnp.sum(e_c, axis=-1, keepdims=True))
        o = (_dot(e_w.astype(BF16), v_ref[pl.ds(start, nk), sl])
             + _dot(e_c.astype(BF16), vc_ref[:, sl]))
        o_ref[:, sl] = (o * inv).astype(o_ref.dtype)


def _na_lat(lat, ctxp, rpb, bsz, seq, clen):
    n = bsz * seq
    rows = seq // GRID_W
    rpt = min(NA_ROWS_PER_TILE, rows)
    nkr, tile_variant, tile_kb, variants = _na_tables(rows, rpt)
    bias = _na_bias(rpb, variants)
    nt = rows // rpt
    rq, nk = rpt * GRID_W, nkr * GRID_W
    bw = BRANCH_WIDTH
    kern = functools.partial(_na_kernel, nk=nk)
    return pl.pallas_call(
        kern,
        grid_spec=pltpu.PrefetchScalarGridSpec(
            num_scalar_prefetch=2,
            grid=(bsz, nt),
            in_specs=[pl.BlockSpec((rq, bw), lambda b, t, var, kb: (b * nt + t, lat["nq"][1])),
                      pl.BlockSpec((seq, bw), lambda b, t, var, kb: (b, lat["nk"][1])),
                      pl.BlockSpec((seq, bw), lambda b, t, var, kb: (b, lat["nv"][1])),
                      pl.BlockSpec((clen, bw), lambda b, t, var, kb: (b, ctxp["nk"][1])),
                      pl.BlockSpec((clen, bw), lambda b, t, var, kb: (b, ctxp["nv"][1])),
                      pl.BlockSpec((None, NA_HEADS, rq, nk), lambda b, t, var, kb: (var[t], 0, 0, 0))],
            out_specs=pl.BlockSpec((rq, bw), lambda b, t, var, kb: (b * nt + t, 0))),
        out_shape=jax.ShapeDtypeStruct((n, bw), BF16),
        compiler_params=_params(("parallel", "arbitrary")),
        name="na_attn",
    )(jnp.asarray(tile_variant), jnp.asarray(tile_kb), lat["nq"][0], lat["nk"][0], lat["nv"][0],
      ctxp["nk"][0], ctxp["nv"][0], bias)


def _na_ctx_kernel(q_ref, k_ref, v_ref, o_ref):
    for h in range(NA_HEADS):
        sl = slice(h * NA_HEAD_DIM, (h + 1) * NA_HEAD_DIM)
        s = _dot_nt(q_ref[:, sl], k_ref[:, sl])
        e = jnp.exp2(s - jnp.max(s, axis=-1, keepdims=True))
        inv = 1.0 / jnp.sum(e, axis=-1, keepdims=True)
        o_ref[:, sl] = (_dot(e.astype(BF16), v_ref[:, sl]) * inv).astype(o_ref.dtype)


def _na_ctx(ctxp, bsz, clen):
    bw = BRANCH_WIDTH
    return pl.pallas_call(
        _na_ctx_kernel,
        grid=(bsz,),
        in_specs=[pl.BlockSpec((clen, bw), lambda b: (b, ctxp["nq"][1])),
                  pl.BlockSpec((clen, bw), lambda b: (b, ctxp["nk"][1])),
                  pl.BlockSpec((clen, bw), lambda b: (b, ctxp["nv"][1]))],
        out_specs=pl.BlockSpec((clen, bw), lambda b: (b, 0)),
        out_shape=jax.ShapeDtypeStruct((bsz * clen, bw), BF16),
        compiler_params=_params(("parallel",)),
        name="na_attn_ctx",
    )(ctxp["nq"][0], ctxp["nk"][0], ctxp["nv"][0])


def _merge_kernel(yd_ref, yl_ref, yn_ref, g0_ref, g1_ref, g2_ref, wb_ref, wo_ref, x_ref,
                  n1_ref, gate_ref, n2_ref, sc_ref, sh_ref, xm_ref, h2_ref):
    m = (g0_ref[...].astype(F32) * _dot(yd_ref[...], wb_ref[0])
         + g1_ref[...].astype(F32) * _dot(yl_ref[...], wb_ref[1])
         + g2_ref[...].astype(F32) * _dot(yn_ref[...], wb_ref[2]))
    mo = _dot(m.astype(BF16), wo_ref[...])
    xm = x_ref[...] + gate_ref[...] * _rms(mo, n1_ref[...])
    xm_ref[...] = xm
    h2 = _rms(xm, n2_ref[...]) * (1.0 + sc_ref[...]) + sh_ref[...]
    h2_ref[...] = h2.astype(h2_ref.dtype)


def _merge(yd, yl, yn, gates, wb, wo, x, n1, n2, mod, row_fn, tm, h2_dtype):
    n, d = x.shape
    bw = BRANCH_WIDTH
    p, gcol = gates
    resident = dict(pipeline_mode=pl.Buffered(1))
    return pl.pallas_call(
        _merge_kernel,
        grid=(n // tm,),
        in_specs=[pl.BlockSpec((tm, bw), lambda i: (i, 0)),
                  pl.BlockSpec((tm, bw), lambda i: (i, 0)),
                  pl.BlockSpec((tm, bw), lambda i: (i, 0)),
                  pl.BlockSpec((tm, d), lambda i: (i, gcol)),
                  pl.BlockSpec((tm, d), lambda i: (i, gcol + 1)),
                  pl.BlockSpec((tm, d), lambda i: (i, gcol + 2)),
                  pl.BlockSpec((3, bw, d), lambda i: (0, 0, 0), **resident),
                  pl.BlockSpec((d, d), lambda i: (0, 0), **resident),
                  pl.BlockSpec((tm, d), lambda i: (i, 0)),
                  _vec_spec(d),
                  _mod_spec(d, M_G1, row_fn),
                  _vec_spec(d),
                  _mod_spec(d, M_SC2, row_fn),
                  _mod_spec(d, M_SH2, row_fn)],
        out_specs=[pl.BlockSpec((tm, d), lambda i: (i, 0)),
                   pl.BlockSpec((tm, d), lambda i: (i, 0))],
        out_shape=[jax.ShapeDtypeStruct((n, d), F32),
                   jax.ShapeDtypeStruct((n, d), h2_dtype)],
        compiler_params=_params(("parallel",)),
        name="merge",
    )(yd, yl, yn, p, p, p, wb, wo, x, n1.reshape(1, d), mod, n2.reshape(1, d), mod, mod)


def _swiglu_step(x_ref, wg_ref, wu_ref, wd_ref, acc_ref, f):
    @pl.when(f == 0)
    def _():
        acc_ref[...] = jnp.zeros_like(acc_ref)
    xb = x_ref[...].astype(BF16)
    g = _dot(xb, wg_ref[...])
    u = _dot(xb, wu_ref[...])
    a = (g * jax.nn.sigmoid(g) * u).astype(BF16)
    acc_ref[...] += _dot(a, wd_ref[...])


def _ffn_dense_kernel(x_ref, wg_ref, wu_ref, wd_ref, xm_ref, gate_ref, n3_ref, *rest, nf, with_next):
    if with_next:
        n0_ref, sc_ref, sh_ref, xo_ref, hn_ref, acc_ref = rest
    else:
        xo_ref, acc_ref = rest
    f = pl.program_id(1)
    _swiglu_step(x_ref, wg_ref, wu_ref, wd_ref, acc_ref, f)

    @pl.when(f == nf - 1)
    def _():
        xo = xm_ref[...] + gate_ref[...] * _rms(acc_ref[...], n3_ref[...])
        xo_ref[...] = xo
        if with_next:
            hn = _rms(xo, n0_ref[...]) * (1.0 + sc_ref[...]) + sh_ref[...]
            hn_ref[...] = hn.astype(hn_ref.dtype)


def _ffn_dense(h2, wg, wu, wd, xm, n3, mod, row_fn, tm, tf, nxt):
    n, d = h2.shape
    fp = wg.shape[1]
    nf = fp // tf
    with_next = nxt is not None
    in_specs = [pl.BlockSpec((tm, d), lambda i, f: (i, 0)),
                pl.BlockSpec((d, tf), lambda i, f: (0, f)),
                pl.BlockSpec((d, tf), lambda i, f: (0, f)),
                pl.BlockSpec((tf, d), lambda i, f: (f, 0)),
                pl.BlockSpec((tm, d), lambda i, f: (i, 0)),
                _mod_spec(d, M_G2, row_fn),
                _vec_spec(d)]
    args = [h2, wg, wu, wd, xm, mod, n3.reshape(1, d)]
    out_specs = [pl.BlockSpec((tm, d), lambda i, f: (i, 0))]
    out_shape = [jax.ShapeDtypeStruct((n, d), F32)]
    if with_next:
        n0, mod_next = nxt
        in_specs += [_vec_spec(d), _mod_spec(d, M_SC1, row_fn), _mod_spec(d, M_SH1, row_fn)]
        args += [n0.reshape(1, d), mod_next, mod_next]
        out_specs.append(pl.BlockSpec((tm, d), lambda i, f: (i, 0)))
        out_shape.append(jax.ShapeDtypeStruct((n, d), BF16))
    kern = functools.partial(_ffn_dense_kernel, nf=nf, with_next=with_next)
    res = pl.pallas_call(
        kern,
        grid=(n // tm, nf),
        in_specs=in_specs,
        out_specs=out_specs,
        out_shape=out_shape,
        scratch_shapes=[pltpu.VMEM((tm, d), F32)],
        compiler_params=_params(("parallel", "arbitrary")),
        name="ffn_dense",
    )(*args)
    return (res[0], res[1]) if with_next else (res[0], None)


def _row_copy(src_hbm, src_row, dst_ref, dst_row, sem):
    return pltpu.make_async_copy(src_hbm.at[pl.ds(src_row, 1)], dst_ref.at[pl.ds(dst_row, 1)], sem)


def _ffn_grouped_kernel(te_ref, nt_ref, used_ref, src_ref, x_hbm, wg_ref, wu_ref, wd_ref, o_ref,
                        rows_ref, xb_ref, sem, *, nf, tm, tm_small, grid_tiles):
    i = pl.program_id(0)
    f = pl.program_id(1)
    nt = nt_ref[0]
    slot = i % 2
    issue_steps = max(1, nf - 2)
    rows_per_step = tm // issue_steps

    def start_row(tile, slot_, r):
        _row_copy(x_hbm, src_ref[tile * tm + r], rows_ref.at[slot_], r, sem.at[slot_]).start()

    def wait_rows(slot_):
        pltpu.make_async_copy(x_hbm.at[pl.ds(0, tm)], rows_ref.at[slot_], sem.at[slot_]).wait()

    @pl.when((i == 0) & (f == 0))
    def _():
        def body(r, c):
            start_row(0, 0, r)
            return c
        lax.fori_loop(0, tm, body, 0)

    @pl.when((i < nt) & (f == 0))
    def _():
        wait_rows(slot)
        xb_ref[...] = rows_ref[slot].astype(BF16)
        o_ref[...] = jnp.zeros_like(o_ref)

    def step(request_rows, m):
        if request_rows:
            nxt = jnp.minimum(i + 1, grid_tiles - 1)
            for j in range(rows_per_step):
                start_row(nxt, 1 - slot, f * rows_per_step + j)
        xb = xb_ref[:m, :]
        g = _dot(xb, wg_ref[...])
        u = _dot(xb, wu_ref[...])
        a = (g * jax.nn.sigmoid(g) * u).astype(BF16)
        o_ref[:m, :] += _dot(a, wd_ref[...])

    small = used_ref[i] <= tm_small
    for request, in_range in ((True, f < issue_steps), (False, f >= issue_steps)):
        if request or issue_steps < nf:
            pl.when((i < nt) & in_range & small)(functools.partial(step, request, tm_small))
            pl.when((i < nt) & in_range & jnp.logical_not(small))(functools.partial(step, request, tm))

    @pl.when((i == nt - 1) & (f == nf - 1))
    def _():
        wait_rows(1 - slot)

    @pl.when((i >= nt) & (f == nf - 1))
    def _():
        o_ref[...] = jnp.zeros_like(o_ref)


def _ffn_grouped(h2, src, wg, wu, wd, tile_expert, n_tiles, tile_used, p_max, tm, tf):
    d = h2.shape[1]
    fe = wg.shape[2]
    nf = fe // tf
    assert tm % max(1, nf - 2) == 0, (tm, nf)

    def fidx(i, f, nt):
        return jnp.where(i < nt[0], f, nf - 1)

    tm_small = min(tm, -(-(tm // 2) // 16) * 16)
    kern = functools.partial(_ffn_grouped_kernel, nf=nf, tm=tm, tm_small=tm_small,
                             grid_tiles=p_max // tm)
    return pl.pallas_call(
        kern,
        grid_spec=pltpu.PrefetchScalarGridSpec(
            num_scalar_prefetch=4,
            grid=(p_max // tm, nf),
            in_specs=[pl.BlockSpec(memory_space=pl.ANY),
                      pl.BlockSpec((None, d, tf), lambda i, f, te, nt, *_: (te[i], 0, fidx(i, f, nt))),
                      pl.BlockSpec((None, d, tf), lambda i, f, te, nt, *_: (te[i], 0, fidx(i, f, nt))),
                      pl.BlockSpec((None, tf, d), lambda i, f, te, nt, *_: (te[i], fidx(i, f, nt), 0))],
            out_specs=pl.BlockSpec((tm, d), lambda i, f, *_: (i, 0)),
            scratch_shapes=[pltpu.VMEM((2, tm, d), h2.dtype),
                            pltpu.VMEM((tm, d), BF16),
                            pltpu.SemaphoreType.DMA((2,))]),
        out_shape=jax.ShapeDtypeStruct((p_max, d), F32),
        compiler_params=pltpu.CompilerParams(dimension_semantics=("arbitrary", "arbitrary"),
                                             vmem_limit_bytes=VMEM_LIMIT,
                                             disable_bounds_checks=True),
        name="ffn_grouped",
    )(tile_expert, n_tiles, tile_used, src, h2, wg, wu, wd)


def _router_kernel(h_ref, wr_ref, o_ref, cnt_ref, carry_ref, *, n_experts):
    i = pl.program_id(0)

    @pl.when(i == 0)
    def _():
        carry_ref[...] = jnp.zeros_like(carry_ref)

    logits = _dot(h_ref[...].astype(BF16), wr_ref[...])
    tm = logits.shape[0]
    lane_i = lax.broadcasted_iota(I32, logits.shape, 1)
    lane = lane_i.astype(F32)
    logits = jnp.where(lane_i < n_experts, logits, -jnp.inf)
    m1 = jnp.max(logits, axis=-1, keepdims=True)
    i1 = jnp.min(jnp.where(logits == m1, lane, float(LANES)), axis=-1, keepdims=True)
    rest = jnp.where(lane == i1, -jnp.inf, logits)
    m2 = jnp.max(rest, axis=-1, keepdims=True)
    i2 = jnp.min(jnp.where(rest == m2, lane, float(LANES)), axis=-1, keepdims=True)
    e21 = jnp.exp(m2 - m1)
    w1 = 1.0 / (1.0 + e21)
    w2 = e21 * w1

    sel1 = lane == i1
    sel2 = lane == i2
    onehot = jnp.where(sel1 | sel2, 1.0, 0.0).astype(BF16)
    r = lax.broadcasted_iota(I32, (tm, tm), 0)
    c = lax.broadcasted_iota(I32, (tm, tm), 1)
    strict_lower = jnp.where(c < r, 1.0, 0.0).astype(BF16)
    before = _dot(strict_lower, onehot) + carry_ref[0:1, :]
    r1 = jnp.sum(jnp.where(sel1, before, 0.0), axis=-1, keepdims=True)
    r2 = jnp.sum(jnp.where(sel2, before, 0.0), axis=-1, keepdims=True)
    total = carry_ref[0:1, :] + jnp.sum(onehot.astype(F32), axis=0, keepdims=True)
    carry_ref[...] = jnp.broadcast_to(total, carry_ref.shape)
    cnt_ref[...] = jnp.broadcast_to(total, cnt_ref.shape)

    packed = jnp.where(lane_i == 0, i1, 0.0)
    packed = jnp.where(lane_i == 1, i2, packed)
    packed = jnp.where(lane_i == 2, r1, packed)
    packed = jnp.where(lane_i == 3, r2, packed)
    packed = jnp.where(lane_i == 4, w1, packed)
    packed = jnp.where(lane_i == 5, w2, packed)
    o_ref[...] = packed


def _router(h2, w_router, tm):
    n, d = h2.shape
    n_experts = w_router.shape[1]
    wr = jnp.pad(w_router, ((0, 0), (0, LANES - n_experts))).astype(BF16)
    kern = functools.partial(_router_kernel, n_experts=n_experts)
    return pl.pallas_call(
        kern,
        grid=(n // tm,),
        in_specs=[pl.BlockSpec((tm, d), lambda i: (i, 0)),
                  pl.BlockSpec((d, LANES), lambda i: (0, 0))],
        out_specs=[pl.BlockSpec((tm, LANES), lambda i: (i, 0)),
                   pl.BlockSpec((SUBLANES, LANES), lambda i: (0, 0))],
        out_shape=[jax.ShapeDtypeStruct((n, LANES), F32),
                   jax.ShapeDtypeStruct((SUBLANES, LANES), F32)],
        scratch_shapes=[pltpu.VMEM((SUBLANES, LANES), F32)],
        compiler_params=_params(("arbitrary",)),
        name="router",
    )(h2, wr)


def _combine_kernel(d1_ref, d2_ref, y_hbm, r_ref, xm_ref, gate_ref, n3_ref, xo_ref, buf_ref, sem, *,
                    tm, n_tiles):
    i = pl.program_id(0)
    slot = i % 2
    dest = (d1_ref, d2_ref)

    def start_rows(tile, slot_, r):
        for k in range(TOP_K):
            _row_copy(y_hbm, dest[k][tile * tm + r], buf_ref.at[slot_, k], r, sem.at[slot_]).start()

    def wait_rows(slot_):
        for k in range(TOP_K):
            pltpu.make_async_copy(y_hbm.at[pl.ds(0, tm)], buf_ref.at[slot_, k], sem.at[slot_]).wait()

    @pl.when(i == 0)
    def _():
        def body(r, c):
            start_rows(0, 0, r)
            return c
        lax.fori_loop(0, tm, body, 0)

    wait_rows(slot)
    nxt = jnp.minimum(i + 1, n_tiles - 1)
    for r in range(tm):
        start_rows(nxt, 1 - slot, r)
    route = r_ref[...]
    y = route[:, 4:5] * buf_ref[slot, 0] + route[:, 5:6] * buf_ref[slot, 1]
    xo_ref[...] = xm_ref[...] + gate_ref[...] * _rms(y, n3_ref[...])

    @pl.when(i == n_tiles - 1)
    def _():
        wait_rows(1 - slot)


def _combine(yo, dest1, dest2, route, xm, n3, mod, row_fn, tm):
    n, d = xm.shape
    kern = functools.partial(_combine_kernel, tm=tm, n_tiles=n // tm)
    return pl.pallas_call(
        kern,
        grid_spec=pltpu.PrefetchScalarGridSpec(
            num_scalar_prefetch=2,
            grid=(n // tm,),
            in_specs=[pl.BlockSpec(memory_space=pl.ANY),
                      pl.BlockSpec((tm, LANES), lambda i, *_: (i, 0)),
                      pl.BlockSpec((tm, d), lambda i, *_: (i, 0)),
                      _mod_spec(d, M_G2, row_fn),
                      _vec_spec(d)],
            out_specs=pl.BlockSpec((tm, d), lambda i, *_: (i, 0)),
            scratch_shapes=[pltpu.VMEM((2, TOP_K, tm, d), F32), pltpu.SemaphoreType.DMA((2,))]),
        out_shape=jax.ShapeDtypeStruct((n, d), F32),
        compiler_params=pltpu.CompilerParams(dimension_semantics=("arbitrary",),
                                             vmem_limit_bytes=VMEM_LIMIT,
                                             disable_bounds_checks=True),
        name="moe_combine",
    )(dest1, dest2, yo, route, xm, mod, n3.reshape(1, d))


def _moe(h2, xm, w_router, wg, wu, wd, n3, mod, row_fn, tm_g, tf):
    n, d = h2.shape
    n_experts = w_router.shape[1]
    route, counts = _router(h2, w_router, min(512, n))
    e1 = route[:, 0].astype(I32)
    e2 = route[:, 1].astype(I32)
    r1 = route[:, 2].astype(I32)
    r2 = route[:, 3].astype(I32)
    cnt = counts[0, :n_experts].astype(I32)
    padded = ((cnt + tm_g - 1) // tm_g) * tm_g
    ends = jnp.cumsum(padded)
    offs = ends - padded
    dest1 = offs[e1] + r1
    dest2 = offs[e2] + r2
    p_max = ((TOP_K * n + n_experts * (tm_g - 1)) // tm_g) * tm_g
    n_rows = ends[-1:]
    tok = jnp.arange(n, dtype=I32)
    src = jnp.zeros((p_max,), I32).at[dest1].set(tok).at[dest2].set(tok)
    tile_start = jnp.arange(p_max // tm_g, dtype=I32) * tm_g
    tile_expert = jnp.minimum(jnp.sum(tile_start[:, None] >= ends[None, :], axis=1),
                              n_experts - 1).astype(I32)
    last_expert = tile_expert[jnp.maximum(n_rows[0] // tm_g - 1, 0)]
    tile_expert = jnp.where(tile_start < n_rows[0], tile_expert, last_expert)
    tile_used = jnp.clip((offs + cnt)[tile_expert] - tile_start, 0, tm_g).astype(I32)
    yo = _ffn_grouped(h2, src, wg, wu, wd, tile_expert, n_rows // tm_g, tile_used, p_max, tm_g, tf)
    return _combine(yo, dest1, dest2, route, xm, n3, mod, row_fn, min(256, n))


def _rope_tables(seq):
    inv = ROPE_BASE ** (-jnp.arange(ROPE_FREQ, dtype=F32) / ROPE_FREQ)
    t = jnp.arange(seq, dtype=I32)
    pos = jnp.stack([t // GRID_W, t % GRID_W], axis=-1).astype(F32)
    ang = pos[:, :, None] * inv
    cos, sin = jnp.cos(ang), jnp.sin(ang)
    cos = jnp.concatenate([cos, cos], axis=-1).reshape(seq, DIFF_HEAD_DIM)
    sin = jnp.concatenate([-sin, sin], axis=-1).reshape(seq, DIFF_HEAD_DIM)
    reps = LANES // DIFF_HEAD_DIM
    return jnp.tile(cos, (1, reps)), jnp.tile(sin, (1, reps))


def _pad_cols(w, mult):
    pad = (-w.shape[-1]) % mult
    return jnp.pad(w, [(0, 0)] * (w.ndim - 1) + [(0, pad)]) if pad else w


def _pad_rows(w, mult):
    pad = (-w.shape[-2]) % mult
    return jnp.pad(w, [(0, 0)] * (w.ndim - 2) + [(0, pad), (0, 0)]) if pad else w


def kernel(x, c, ctx, c_ctx, norm_g, w_ada, b_ada, w_in, diff_lambda, diff_subln_g, lru_conv_w, lru_conv_b, lru_gate_w, lru_gate_b, lru_lambda, na_rpb, w_branch, w_merge, b_merge, w_out, ffn_w_gate, ffn_w_up, ffn_w_down, moe_w_router, moe_w_gate, moe_w_up, moe_w_down):
    bsz, seq, d = x.shape
    clen = ctx.shape[1]
    depth = w_in.shape[0]
    n, nc = bsz * seq, bsz * clen
    x_lat = x.reshape(n, d)
    x_ctx = ctx.reshape(nc, d)

    mr = -(-(bsz + 1) // SUBLANES) * SUBLANES
    cvec = jnp.concatenate([c, c_ctx[None], jnp.zeros((mr - bsz - 1, d), F32)], axis=0)
    mod_all = _ada(cvec, w_ada, b_ada).reshape(depth, mr, 1, 6 * d)
    cos_t, sin_t = _rope_tables(seq)

    tm_lat, tm_ctx = min(1024, seq), min(1024, nc, seq)
    tm_mix_lat, tm_mix_ctx = min(256, seq), min(256, nc)
    tm_ffn_lat, tm_ffn_ctx = min(512, seq), min(512, nc)
    tf = min(512, d)
    tf_moe = min(MOE_TF, d)
    tq = min(DIFF_TQ, seq)

    def lat_row(tile_rows):
        per_batch = seq // tile_rows
        return lambda i: i // per_batch

    def ctx_row(i):
        return bsz

    h_lat = _prenorm(x_lat, norm_g[0, 0], mod_all[0], lat_row(tm_lat), tm_lat)
    h_ctx = _prenorm(x_ctx, norm_g[0, 0], mod_all[0], ctx_row, tm_ctx)

    rest_names = ("dv", "lx", "lg", "nq", "nk", "nv")
    part_scale = [1.0] * N_IN_PARTS
    part_scale[P_DQ], part_scale[P_NQ] = DIFF_QSCALE, NA_QSCALE
    col_scale = jnp.repeat(jnp.asarray(part_scale, F32), BRANCH_WIDTH)[None]
    scale_qk, scale_rest = col_scale[:, :2 * BRANCH_WIDTH], col_scale[:, 2 * BRANCH_WIDTH:]
    diff_steps = bsz * DIFF_HEADS * (seq // tq)
    moe_src = (moe_w_gate, moe_w_up, moe_w_down)

    def halves(w):
        return w.reshape(2, -1, w.shape[-1])

    def side_cast_ok(mi):
        return moe_w_router.shape[-1] % 2 == 0 and all(
            halves(w[mi]).shape[1] % (16 * diff_steps) == 0 for w in moe_src)

    moe_bf16 = {}
    for l in range(depth):
        need_ctx = l < depth - 1
        mod = mod_all[l]
        lam_init = 0.8 - 0.6 * math.exp(-0.3 * l)
        w_qk = w_in[l][:, :2 * BRANCH_WIDTH].astype(BF16)
        w_rest = w_in[l][:, 2 * BRANCH_WIDTH:].astype(BF16)
        w_gate = w_merge[l].astype(BF16)
        b_gate = b_merge[l][None]

        def project(h, tm, rope):
            qk = _proj(h, w_qk, scale_qk, cos_t, sin_t, seq, tm, PROJ_TN_ROPE if rope else PROJ_TN,
                       "rope" if rope else "plain")
            rest = _proj(h, w_rest, scale_rest, cos_t, sin_t, seq, tm, PROJ_TN, "plain")
            parts = {"dq": (qk, 0), "dk": (qk, 1)}
            parts.update({name: (rest, k) for k, name in enumerate(rest_names)})
            return parts

        lat = project(h_lat, tm_lat, True)
        ctxp = project(h_ctx, tm_ctx, False)
        gate_lat = _proj(h_lat, w_gate, b_gate, cos_t, sin_t, seq, tm_lat, PROJ_TN, "gate")

        mi, half = l // 2, l % 2
        jobs = ()
        if mi < moe_w_gate.shape[0] and side_cast_ok(mi):
            jobs = tuple((halves(w[mi]), half, moe_bf16[mi][k] if half else None)
                         for k, w in enumerate(moe_src))
        y_diff, cast = _diff_attn_lat(lat, ctxp, diff_lambda[l], diff_subln_g[l], lam_init,
                                      bsz, seq, clen, tq, jobs)
        if jobs:
            moe_bf16[mi] = cast
        y_lru, y_lru_c = _lru(lat, ctxp, lru_conv_w[l], lru_conv_b[l], lru_gate_w[l],
                              lru_gate_b[l], lru_lambda[l], bsz, seq, clen, need_ctx)
        y_na = _na_lat(lat, ctxp, na_rpb[l], bsz, seq, clen)

        wb = w_branch[l].astype(BF16)
        wo = w_out[l].astype(BF16)
        is_moe = l % 2 == 1
        h2_dtype = F32 if is_moe else BF16
        xm_lat, h2_lat = _merge(y_diff, y_lru, y_na, (gate_lat, 0), wb, wo, x_lat, norm_g[l, 1],
                                norm_g[l, 2], mod, lat_row(tm_mix_lat), tm_mix_lat, h2_dtype)
        if need_ctx:
            y_diff_c = _diff_attn_ctx(ctxp, diff_lambda[l], diff_subln_g[l], lam_init, bsz, clen)
            y_na_c = _na_ctx(ctxp, bsz, clen)
            gate_ctx = _proj(h_ctx, w_gate, b_gate, cos_t, sin_t, seq, tm_ctx, PROJ_TN, "gate")
            xm_ctx, h2_ctx = _merge(y_diff_c, y_lru_c, y_na_c, (gate_ctx, 0), wb, wo, x_ctx,
                                    norm_g[l, 1], norm_g[l, 2], mod, ctx_row, tm_mix_ctx, h2_dtype)

        nxt = (norm_g[l + 1, 0], mod_all[l + 1]) if need_ctx else None
        i = l // 2
        if is_moe:
            if i in moe_bf16:
                wg, wu, wd = (c.reshape(w.shape[1:]) for c, w in zip(moe_bf16[i], moe_src))
            else:
                wg, wu, wd = (w[i].astype(BF16) for w in moe_src)
            tm_g = min(MOE_TILE, n)
            x_lat = _moe(h2_lat, xm_lat, moe_w_router[i], wg, wu, wd, norm_g[l, 3], mod,
                         lat_row(min(256, n)), tm_g, tf_moe)
            if need_ctx:
                x_ctx = _moe(h2_ctx, xm_ctx, moe_w_router[i], wg, wu, wd, norm_g[l, 3], mod,
                             ctx_row, min(MOE_TILE, nc), tf_moe)
                h_lat = _prenorm(x_lat, nxt[0], nxt[1], lat_row(tm_lat), tm_lat)
                h_ctx = _prenorm(x_ctx, nxt[0], nxt[1], ctx_row, tm_ctx)
        else:
            wg = _pad_cols(ffn_w_gate[i], tf).astype(BF16)
            wu = _pad_cols(ffn_w_up[i], tf).astype(BF16)
            wd = _pad_rows(ffn_w_down[i], tf).astype(BF16)
            x_lat, h_lat = _ffn_dense(h2_lat, wg, wu, wd, xm_lat, norm_g[l, 3], mod,
                                      lat_row(tm_ffn_lat), tm_ffn_lat, tf, nxt)
            if need_ctx:
                x_ctx, h_ctx = _ffn_dense(h2_ctx, wg, wu, wd, xm_ctx, norm_g[l, 3], mod,
                                          ctx_row, tm_ffn_ctx, tf, nxt)
    return x_lat.reshape(bsz, seq, d)
```

```python
import functools
import math

import numpy as np
import jax
import jax.numpy as jnp
from jax import lax
from jax.experimental import pallas as pl
from jax.experimental.pallas import tpu as pltpu

F32 = jnp.float32
BF16 = jnp.bfloat16
I32 = jnp.int32

EPS = 1e-6
GRID_W = 64
BRANCH_WIDTH = 1024
N_IN_PARTS = 8
IN_WIDTH = N_IN_PARTS * BRANCH_WIDTH
DIFF_HEADS = 8
DIFF_HEAD_DIM = 64
DIFF_V_DIM = 2 * DIFF_HEAD_DIM
ROPE_BASE = 10000.0
ROPE_FREQ = DIFF_HEAD_DIM // 4
LRU_BLOCKS = 8
LRU_BLOCK_DIM = BRANCH_WIDTH // LRU_BLOCKS
LRU_CONV_W = 4
LRU_C = 8.0
NA_HEADS = 8
NA_HEAD_DIM = BRANCH_WIDTH // NA_HEADS
NA_WIN_R = 8
NA_WIN_C = 16
TOP_K = 2
MOE_TILE = 560
MOE_TF = 1024
MOE_HEIGHTS = 4
DIFF_TQ = 1024
PROJ_TN = 2048
PROJ_TN_ROPE = 1024

LANES = 128
SUBLANES = 8
VMEM_LIMIT = 56 * 1024 * 1024
NEG = -1e30
LOG2E = 1.0 / math.log(2.0)
DIFF_QSCALE = DIFF_HEAD_DIM ** -0.5 * LOG2E
NA_QSCALE = NA_HEAD_DIM ** -0.5 * LOG2E

P_DQ, P_DK, P_DV, P_LX, P_LG, P_NQ, P_NK, P_NV = range(8)
M_SH1, M_SC1, M_G1, M_SH2, M_SC2, M_G2 = range(6)


def _params(sem):
    return pltpu.CompilerParams(dimension_semantics=sem, vmem_limit_bytes=VMEM_LIMIT)


def _rms(x, g):
    return x * lax.rsqrt(jnp.mean(x * x, axis=-1, keepdims=True) + EPS) * g


def _dot(a, b):
    return jnp.dot(a, b, preferred_element_type=F32)


def _dot_nt(a, b):
    return lax.dot_general(a, b, (((1,), (1,)), ((), ())), preferred_element_type=F32)


def _ada_kernel(c_ref, w_ref, b_ref, o_ref):
    c = c_ref[...]
    s = (c * jax.nn.sigmoid(c)).astype(BF16)
    o_ref[...] = _dot(s, w_ref[...].astype(BF16)) + b_ref[...]


def _ada(cvec, w_ada, b_ada):
    n_layers, d, n6 = w_ada.shape
    mr = cvec.shape[0]
    tn = min(1024, d)
    return pl.pallas_call(
        _ada_kernel,
        grid=(n_layers, n6 // tn),
        in_specs=[pl.BlockSpec((mr, d), lambda l, j: (0, 0)),
                  pl.BlockSpec((None, d, tn), lambda l, j: (l, 0, j)),
                  pl.BlockSpec((None, 1, tn), lambda l, j: (l, 0, j))],
        out_specs=pl.BlockSpec((None, mr, tn), lambda l, j: (l, 0, j)),
        out_shape=jax.ShapeDtypeStruct((n_layers, mr, n6), F32),
        compiler_params=_params(("arbitrary", "arbitrary")),
        name="ada_mod",
    )(cvec, w_ada, b_ada.reshape(n_layers, 1, n6))


def _mod_spec(d, chunk, row_fn):
    return pl.BlockSpec((None, 1, d), lambda i, *_: (row_fn(i), 0, chunk))


def _vec_spec(d):
    return pl.BlockSpec((1, d), lambda i, *_: (0, 0))


def _prenorm_kernel(x_ref, g_ref, sc_ref, sh_ref, o_ref):
    y = _rms(x_ref[...], g_ref[...])
    o_ref[...] = (y * (1.0 + sc_ref[...]) + sh_ref[...]).astype(o_ref.dtype)


def _prenorm(x, g, mod, row_fn, tm):
    n, d = x.shape
    return pl.pallas_call(
        _prenorm_kernel,
        grid=(n // tm,),
        in_specs=[pl.BlockSpec((tm, d), lambda i: (i, 0)),
                  _vec_spec(d),
                  _mod_spec(d, M_SC1, row_fn),
                  _mod_spec(d, M_SH1, row_fn)],
        out_specs=pl.BlockSpec((tm, d), lambda i: (i, 0)),
        out_shape=jax.ShapeDtypeStruct((n, d), BF16),
        compiler_params=_params(("parallel",)),
        name="prenorm",
    )(x, g.reshape(1, d), mod, mod)


def _proj_kernel(h_ref, w_ref, v_ref, cos_ref, sin_ref, o_ref, *, kind):
    acc = _dot(h_ref[...], w_ref[...])
    tn = acc.shape[1]
    if kind == "gate":
        o_ref[...] = jax.nn.sigmoid(acc + v_ref[...]).astype(o_ref.dtype)
    elif kind == "rope":
        lane = lax.broadcasted_iota(I32, acc.shape, 1)
        first_half = (lane % (2 * ROPE_FREQ)) < ROPE_FREQ
        partner = jnp.where(first_half, pltpu.roll(acc, tn - ROPE_FREQ, 1),
                            pltpu.roll(acc, ROPE_FREQ, 1))
        cos = jnp.tile(cos_ref[...], (1, tn // LANES)) * v_ref[...]
        sin = jnp.tile(sin_ref[...], (1, tn // LANES)) * v_ref[...]
        o_ref[...] = (acc * cos + partner * sin).astype(o_ref.dtype)
    else:
        o_ref[...] = (acc * v_ref[...]).astype(o_ref.dtype)


def _proj(h, w, v, cos_t, sin_t, seq, tm, tn, kind):
    n, d = h.shape
    nw = w.shape[1]
    tn = min(tn, nw)
    tpb = seq // tm
    kern = functools.partial(_proj_kernel, kind=kind)
    return pl.pallas_call(
        kern,
        grid=(n // tm, nw // tn),
        in_specs=[pl.BlockSpec((tm, d), lambda i, j: (i, 0)),
                  pl.BlockSpec((d, tn), lambda i, j: (0, j)),
                  pl.BlockSpec((1, tn), lambda i, j: (0, j)),
                  pl.BlockSpec((tm, LANES), lambda i, j: (i % tpb, 0)),
                  pl.BlockSpec((tm, LANES), lambda i, j: (i % tpb, 0))],
        out_specs=pl.BlockSpec((tm, tn), lambda i, j: (i, j)),
        out_shape=jax.ShapeDtypeStruct((n, nw), BF16),
        compiler_params=_params(("parallel", "arbitrary")),
        name="proj_" + kind,
    )(h, w, v, cos_t, sin_t)


DIFF_KEY_CHUNK = 512


def _diff_kernel(*refs, lam_init, has_lat, n_cast, n_alias=0):
    n_in = 7 if has_lat else 5
    n_extra = n_cast + n_alias
    for src_ref, dst_ref in zip(refs[n_in:n_in + n_cast], refs[n_in + n_extra + 1:]):
        dst_ref[0] = src_ref[...].astype(dst_ref.dtype)
        if dst_ref.shape[0] == 2:
            dst_ref[1] = jnp.zeros(dst_ref.shape[1:], dst_ref.dtype)
    if has_lat:
        lamv_ref, sg_ref, q_ref, k_ref, v_ref, kc_ref, vc_ref = refs[:n_in]
    else:
        lamv_ref, sg_ref, q_ref, kc_ref, vc_ref = refs[:n_in]
    o_ref = refs[n_in + n_extra]
    lv = lamv_ref[...]
    lam = (jnp.exp(jnp.sum(lv[0:1] * lv[1:2], axis=-1, keepdims=True))
           - jnp.exp(jnp.sum(lv[2:3] * lv[3:4], axis=-1, keepdims=True)) + lam_init)
    q = q_ref[...]
    first = lax.broadcasted_iota(I32, q.shape, 1) < DIFF_HEAD_DIM
    zero = jnp.zeros_like(q)
    qz = (jnp.where(first, q, zero), jnp.where(first, zero, q))
    chunks = [(kc_ref, vc_ref, 0, kc_ref.shape[0])]
    if has_lat:
        kt = min(DIFF_KEY_CHUNK, k_ref.shape[0])
        chunks += [(k_ref, v_ref, j * kt, kt) for j in range(k_ref.shape[0] // kt)]

    def qk(c, chunk):
        k, _, off, n = chunk
        return _dot_nt(qz[c], k[off:off + n, :])

    def row_max(parts):
        m = jnp.max(parts[0], axis=-1, keepdims=True)
        for p in parts[1:]:
            m = jnp.maximum(m, jnp.max(p, axis=-1, keepdims=True))
        return m

    def softmax_av(s_parts, between=None):
        m = row_max(s_parts)
        den, out = 0.0, 0.0
        for j, (_, v, off, n) in enumerate(chunks):
            if between is not None:
                between(j)
            e = jnp.exp2(s_parts[j] - m)
            den = den + jnp.sum(e, axis=-1, keepdims=True)
            out = out + _dot(e.astype(BF16), v[off:off + n, :])
        return out, den

    s1 = [qk(0, ch) for ch in chunks]
    s2 = []
    o1, den1 = softmax_av(s1, between=lambda j: s2.append(qk(1, chunks[j])))
    o2, den2 = softmax_av(s2)
    o = o1 * (1.0 / den1) - o2 * (lam / den2)
    y = _rms(o, sg_ref[...]) * (1.0 - lam_init)
    o_ref[...] = y.astype(o_ref.dtype)


def _diff_attn_lat(lat, ctxp, lamv, subln_g, lam_init, bsz, seq, clen, tq, cast_jobs=()):
    nq = seq // tq
    n_steps = bsz * DIFF_HEADS * nq
    hw = DIFF_V_DIM
    hpb = BRANCH_WIDTH // hw

    def spec(rows, src, row_fn):
        blk = src[1] * hpb
        return pl.BlockSpec((rows, hw), lambda b, h, i: (row_fn(b, i), blk + h))

    def step(b, h, i):
        return (b * DIFF_HEADS + h) * nq + i

    n_base = 7
    cast_in, cast_out, cast_shapes, cast_args, aliases = [], [], [], [], {}
    for k, (arr, half, dst) in enumerate(cast_jobs):
        _, rows, cols = arr.shape
        per_step = rows // n_steps
        assert per_step * n_steps == rows and per_step % 16 == 0
        cast_in.append(pl.BlockSpec((None, per_step, cols), lambda b, h, i, half=half: (half, step(b, h, i), 0)))
        cast_args.append(arr)
        cast_shapes.append(jax.ShapeDtypeStruct(arr.shape, BF16))
        if dst is None:
            cast_out.append(pl.BlockSpec((2, per_step, cols), lambda b, h, i: (0, step(b, h, i), 0)))
        else:
            assert half == 1
            cast_out.append(pl.BlockSpec((1, per_step, cols), lambda b, h, i: (1, step(b, h, i), 0)))
    for k, (arr, half, dst) in enumerate(cast_jobs):
        if dst is not None:
            aliases[n_base + len(cast_jobs) + len(aliases)] = 1 + k
            cast_in.append(pl.BlockSpec(memory_space=pl.ANY))
            cast_args.append(dst)
    n_alias = len(aliases)
    kern = functools.partial(_diff_kernel, lam_init=lam_init, has_lat=True, n_cast=len(cast_jobs),
                             n_alias=n_alias)
    res = pl.pallas_call(
        kern,
        grid=(bsz, DIFF_HEADS, nq),
        in_specs=[pl.BlockSpec((4, DIFF_HEAD_DIM), lambda b, h, i: (0, 0)),
                  pl.BlockSpec((1, hw), lambda b, h, i: (0, 0)),
                  spec(tq, lat["dq"], lambda b, i: b * nq + i),
                  spec(seq, lat["dk"], lambda b, i: b),
                  spec(seq, lat["dv"], lambda b, i: b),
                  spec(clen, ctxp["dk"], lambda b, i: b),
                  spec(clen, ctxp["dv"], lambda b, i: b)] + cast_in,
        out_specs=[pl.BlockSpec((tq, hw), lambda b, h, i: (b * nq + i, h))] + cast_out,
        out_shape=[jax.ShapeDtypeStruct((bsz * seq, BRANCH_WIDTH), BF16)] + cast_shapes,
        input_output_aliases=aliases,
        compiler_params=_params(("parallel", "parallel", "arbitrary")),
        name="diff_attn",
    )(lamv, subln_g.reshape(1, hw), lat["dq"][0], lat["dk"][0], lat["dv"][0],
      ctxp["dk"][0], ctxp["dv"][0], *cast_args)
    return res[0], res[1:]


def _diff_attn_ctx(ctxp, lamv, subln_g, lam_init, bsz, clen):
    hw = DIFF_V_DIM
    hpb = BRANCH_WIDTH // hw

    def spec(src):
        blk = src[1] * hpb
        return pl.BlockSpec((clen, hw), lambda b, h: (b, blk + h))

    kern = functools.partial(_diff_kernel, lam_init=lam_init, has_lat=False, n_cast=0)
    return pl.pallas_call(
        kern,
        grid=(bsz, DIFF_HEADS),
        in_specs=[pl.BlockSpec((4, DIFF_HEAD_DIM), lambda b, h: (0, 0)),
                  pl.BlockSpec((1, hw), lambda b, h: (0, 0)),
                  spec(ctxp["dq"]), spec(ctxp["dk"]), spec(ctxp["dv"])],
        out_specs=pl.BlockSpec((clen, hw), lambda b, h: (b, h)),
        out_shape=jax.ShapeDtypeStruct((bsz * clen, BRANCH_WIDTH), BF16),
        compiler_params=_params(("parallel", "parallel")),
        name="diff_attn_ctx",
    )(lamv, subln_g.reshape(1, hw), ctxp["dq"][0], ctxp["dk"][0], ctxp["dv"][0])


LRU_TW = 256
LRU_UNROLL = 4


def _gelu_tanh(x):
    return 0.5 * x * (1.0 + jnp.tanh(math.sqrt(2.0 / math.pi) * (x + 0.044715 * (x * x * x))))


def _tile_scan(a, b, reverse):
    rows = lax.broadcasted_iota(I32, a.shape, 0)
    for s in (1, 2, 4):
        if reverse:
            a_s = pltpu.roll(a, SUBLANES - s, 0)
            b_s = pltpu.roll(b, SUBLANES - s, 0)
            valid = rows < SUBLANES - s
        else:
            a_s = pltpu.roll(a, s, 0)
            b_s = pltpu.roll(b, s, 0)
            valid = rows >= s
        b = jnp.where(valid, a * b_s + b, b)
        a = jnp.where(valid, a * a_s, a)
    return a, b


def _lru_kernel(xl_ref, gl_ref, xc_ref, gc_ref, cw_ref, cb_ref, gw_ref, gb_ref, lam_ref,
                *out_and_scratch, seq, clen, need_ctx):
    if need_ctx:
        yl_ref, yc_ref, af_ref, bf_ref, ab_ref, bb_ref, hf_ref = out_and_scratch
    else:
        yl_ref, af_ref, bf_ref, ab_ref, bb_ref, hf_ref = out_and_scratch
        yc_ref = None
    tot = clen + seq
    x = jnp.concatenate([xc_ref[...], xl_ref[...]], axis=0).astype(F32)
    row = lax.broadcasted_iota(I32, x.shape, 0)
    pos = jnp.where(row < clen, row, row - clen)
    seg_len = jnp.where(row < clen, clen, seq)
    cw = cw_ref[...]
    conv = cb_ref[...] + x * cw[2:3]
    conv = conv + jnp.where(pos >= 2, pltpu.roll(x, 2, 0), 0.0) * cw[0:1]
    conv = conv + jnp.where(pos >= 1, pltpu.roll(x, 1, 0), 0.0) * cw[1:2]
    conv = conv + jnp.where(pos < seg_len - 1, pltpu.roll(x, tot - 1, 0), 0.0) * cw[3:4]
    xb = conv.astype(BF16)
    gb = gb_ref[...]
    lam = lam_ref[...]
    for d, (a_ref, b_ref) in enumerate(((af_ref, bf_ref), (ab_ref, bb_ref))):
        r = jax.nn.sigmoid(_dot(xb, gw_ref[d, 0]) + gb[2 * d:2 * d + 1])
        i = jax.nn.sigmoid(_dot(xb, gw_ref[d, 1]) + gb[2 * d + 1:2 * d + 2])
        z = -lam[d:d + 1]
        softplus = jnp.maximum(z, 0.0) + jnp.log1p(jnp.exp(-jnp.abs(z)))
        log_a = -LRU_C * r * softplus
        a = jnp.exp(log_a)
        beta = jnp.sqrt(-jnp.tanh(log_a) * (a * a + 1.0))
        a_ref[...] = a
        b_ref[...] = beta * i * conv

    n_ct = clen // SUBLANES
    n_t = tot // SUBLANES
    w = x.shape[1]
    assert n_ct % LRU_UNROLL == 0 and n_t % LRU_UNROLL == 0
    rows_per_iter = LRU_UNROLL * SUBLANES

    def fwd_step(t, carry):
        base = pl.multiple_of(t * rows_per_iter, rows_per_iter)
        scans = [_tile_scan(af_ref[pl.ds(base + k * SUBLANES, SUBLANES), :],
                            bf_ref[pl.ds(base + k * SUBLANES, SUBLANES), :], False)
                 for k in range(LRU_UNROLL)]
        for k, (a, b) in enumerate(scans):
            h = a * carry + b
            hf_ref[pl.ds(base + k * SUBLANES, SUBLANES), :] = h
            carry = h[SUBLANES - 1:SUBLANES, :]
        return carry

    lax.fori_loop(0, n_t // LRU_UNROLL, fwd_step, jnp.zeros((1, w), F32))

    def bwd_iter(first_tile, n_tiles, g_ref, y_ref, seg_off):
        def step(t, carry):
            base = pl.multiple_of((first_tile + n_tiles) * SUBLANES - (t + 1) * rows_per_iter,
                                  rows_per_iter)
            offs = [base + k * SUBLANES for k in reversed(range(LRU_UNROLL))]
            scans = [_tile_scan(ab_ref[pl.ds(off, SUBLANES), :], bb_ref[pl.ds(off, SUBLANES), :], True)
                     for off in offs]
            for off, (a, b) in zip(offs, scans):
                h = a * carry + b
                carry = h[0:1, :]
                if y_ref is not None:
                    g = g_ref[pl.ds(off - seg_off, SUBLANES), :].astype(F32)
                    y_ref[pl.ds(off - seg_off, SUBLANES), :] = (
                        (hf_ref[pl.ds(off, SUBLANES), :] + h) * _gelu_tanh(g)).astype(y_ref.dtype)
            return carry
        return step

    carry = lax.fori_loop(0, n_ct // LRU_UNROLL, bwd_iter(0, n_ct, gc_ref, yc_ref, 0),
                          jnp.zeros((1, w), F32))
    lax.fori_loop(0, (n_t - n_ct) // LRU_UNROLL, bwd_iter(n_ct, n_t - n_ct, gl_ref, yl_ref, clen), carry)


def _lru(lat, ctxp, conv_w, conv_b, gate_w, gate_b, lru_lam, bsz, seq, clen, need_ctx):
    n = bsz * seq
    tw = LRU_TW
    nwt = BRANCH_WIDTH // tw
    per = tw // LRU_BLOCK_DIM
    gw = gate_w.reshape(2, 2, nwt, per, LRU_BLOCK_DIM, LRU_BLOCK_DIM)
    eye = jnp.eye(per, dtype=gate_w.dtype)
    gw = jnp.einsum('dgnpcf,pq->dgnpcqf', gw, eye).reshape(2, 2, nwt, tw, tw).astype(BF16)
    gb = gate_b.reshape(4, BRANCH_WIDTH)
    tot = seq + clen
    kern = functools.partial(_lru_kernel, seq=seq, clen=clen, need_ctx=need_ctx)

    def part_spec(rows, src):
        blk = src[1] * nwt
        return pl.BlockSpec((rows, tw), lambda b, j: (b, blk + j))

    out_shape = [jax.ShapeDtypeStruct((n, BRANCH_WIDTH), BF16)]
    out_specs = [pl.BlockSpec((seq, tw), lambda b, j: (b, j))]
    if need_ctx:
        out_shape.append(jax.ShapeDtypeStruct((bsz * clen, BRANCH_WIDTH), BF16))
        out_specs.append(pl.BlockSpec((clen, tw), lambda b, j: (b, j)))
    res = pl.pallas_call(
        kern,
        grid=(bsz, nwt),
        in_specs=[part_spec(seq, lat["lx"]), part_spec(seq, lat["lg"]),
                  part_spec(clen, ctxp["lx"]), part_spec(clen, ctxp["lg"]),
                  pl.BlockSpec((LRU_CONV_W, tw), lambda b, j: (0, j)),
                  pl.BlockSpec((1, tw), lambda b, j: (0, j)),
                  pl.BlockSpec((2, 2, None, tw, tw), lambda b, j: (0, 0, j, 0, 0)),
                  pl.BlockSpec((4, tw), lambda b, j: (0, j)),
                  pl.BlockSpec((2, tw), lambda b, j: (0, j))],
        out_specs=out_specs,
        out_shape=out_shape,
        scratch_shapes=[pltpu.VMEM((tot, tw), F32) for _ in range(5)],
        compiler_params=_params(("parallel", "parallel")),
        name="rglru",
    )(lat["lx"][0], lat["lg"][0], ctxp["lx"][0], ctxp["lg"][0], conv_w,
      conv_b.reshape(1, BRANCH_WIDTH), gw, gb, lru_lam)
    return (res[0], res[1]) if need_ctx else (res[0], None)


NA_ROWS_PER_TILE = 4


def _na_tables(rows, rpt):
    kr = min(NA_WIN_R, rows)
    nkr = min(rows, rpt + kr - 1)
    variants, index, tile_variant, tile_kb = [], {}, [], []
    for t in range(rows // rpt):
        r0 = t * rpt
        kb = int(np.clip(r0 - kr // 2, 0, rows - nkr))
        geom = []
        for a in range(rpt):
            r = r0 + a
            rs = int(np.clip(r - kr // 2, 0, rows - kr))
            geom.append(tuple((kb + k - r + NA_WIN_R - 1) if rs <= kb + k < rs + kr else None
                              for k in range(nkr)))
        geom = tuple(geom)
        if geom not in index:
            index[geom] = len(variants)
            variants.append(geom)
        tile_variant.append(index[geom])
        tile_kb.append(kb)
    return nkr, np.array(tile_variant, np.int32), np.array(tile_kb, np.int32), variants


def _na_bias(rpb, variants):
    heads = rpb.shape[0]
    pad = GRID_W - NA_WIN_C
    padded = jnp.pad(rpb * LOG2E, ((0, 0), (0, 0), (pad, pad)))
    toep = jnp.stack([padded[..., GRID_W - 1 - qc:2 * GRID_W - 1 - qc] for qc in range(GRID_W)],
                     axis=2)
    col = np.arange(GRID_W)
    cs = np.clip(col - NA_WIN_C // 2, 0, GRID_W - NA_WIN_C)
    in_cols = (col[None, :] >= cs[:, None]) & (col[None, :] < cs[:, None] + NA_WIN_C)
    toep = jnp.where(in_cols, toep, NEG).astype(F32)
    outside = jnp.full((heads, GRID_W, GRID_W), NEG, F32)
    return jnp.stack([
        jnp.concatenate([
            jnp.concatenate([outside if dr is None else toep[:, dr] for dr in row], axis=-1)
            for row in geom], axis=-2)
        for geom in variants])


def _na_kernel(var_ref, kb_ref, q_ref, k_ref, v_ref, kc_ref, vc_ref, bias_ref, o_ref, *, nk):
    t = pl.program_id(1)
    start = pl.multiple_of(kb_ref[t] * GRID_W, GRID_W)
    for h in range(NA_HEADS):
        sl = slice(h * NA_HEAD_DIM, (h + 1) * NA_HEAD_DIM)
        qh = q_ref[:, sl]
        s_w = _dot_nt(qh, k_ref[pl.ds(start, nk), sl]) + bias_ref[h]
        s_c = _dot_nt(qh, kc_ref[:, sl])
        m = jnp.maximum(jnp.max(s_w, axis=-1, keepdims=True), jnp.max(s_c, axis=-1, keepdims=True))
        e_w = jnp.exp2(s_w - m)
        e_c = jnp.exp2(s_c - m)
        inv = 1.0 / (jnp.sum(e_w, axis=-1, keepdims=True) + jnp.sum(e_c, axis=-1, keepdims=True))
        o = (_dot(e_w.astype(BF16), v_ref[pl.ds(start, nk), sl])
             + _dot(e_c.astype(BF16), vc_ref[:, sl]))
        o_ref[:, sl] = (o * inv).astype(o_ref.dtype)


def _na_lat(lat, ctxp, rpb, bsz, seq, clen):
    n = bsz * seq
    rows = seq // GRID_W
    rpt = min(NA_ROWS_PER_TILE, rows)
    nkr, tile_variant, tile_kb, variants = _na_tables(rows, rpt)
    bias = _na_bias(rpb, variants)
    nt = rows // rpt
    rq, nk = rpt * GRID_W, nkr * GRID_W
    bw = BRANCH_WIDTH
    kern = functools.partial(_na_kernel, nk=nk)
    return pl.pallas_call(
        kern,
        grid_spec=pltpu.PrefetchScalarGridSpec(
            num_scalar_prefetch=2,
            grid=(bsz, nt),
            in_specs=[pl.BlockSpec((rq, bw), lambda b, t, var, kb: (b * nt + t, lat["nq"][1])),
                      pl.BlockSpec((seq, bw), lambda b, t, var, kb: (b, lat["nk"][1])),
                      pl.BlockSpec((seq, bw), lambda b, t, var, kb: (b, lat["nv"][1])),
                      pl.BlockSpec((clen, bw), lambda b, t, var, kb: (b, ctxp["nk"][1])),
                      pl.BlockSpec((clen, bw), lambda b, t, var, kb: (b, ctxp["nv"][1])),
                      pl.BlockSpec((None, NA_HEADS, rq, nk), lambda b, t, var, kb: (var[t], 0, 0, 0))],
            out_specs=pl.BlockSpec((rq, bw), lambda b, t, var, kb: (b * nt + t, 0))),
        out_shape=jax.ShapeDtypeStruct((n, bw), BF16),
        compiler_params=_params(("parallel", "arbitrary")),
        name="na_attn",
    )(jnp.asarray(tile_variant), jnp.asarray(tile_kb), lat["nq"][0], lat["nk"][0], lat["nv"][0],
      ctxp["nk"][0], ctxp["nv"][0], bias)


def _na_ctx_kernel(q_ref, k_ref, v_ref, o_ref):
    for h in range(NA_HEADS):
        sl = slice(h * NA_HEAD_DIM, (h + 1) * NA_HEAD_DIM)
        s = _dot_nt(q_ref[:, sl], k_ref[:, sl])
        e = jnp.exp2(s - jnp.max(s, axis=-1, keepdims=True))
        inv = 1.0 / jnp.sum(e, axis=-1, keepdims=True)
        o_ref[:, sl] = (_dot(e.astype(BF16), v_ref[:, sl]) * inv).astype(o_ref.dtype)


def _na_ctx(ctxp, bsz, clen):
    bw = BRANCH_WIDTH
    return pl.pallas_call(
        _na_ctx_kernel,
        grid=(bsz,),
        in_specs=[pl.BlockSpec((clen, bw), lambda b: (b, ctxp["nq"][1])),
                  pl.BlockSpec((clen, bw), lambda b: (b, ctxp["nk"][1])),
                  pl.BlockSpec((clen, bw), lambda b: (b, ctxp["nv"][1]))],
        out_specs=pl.BlockSpec((clen, bw), lambda b: (b, 0)),
        out_shape=jax.ShapeDtypeStruct((bsz * clen, bw), BF16),
        compiler_params=_params(("parallel",)),
        name="na_attn_ctx",
    )(ctxp["nq"][0], ctxp["nk"][0], ctxp["nv"][0])


def _merge_kernel(yd_ref, yl_ref, yn_ref, g0_ref, g1_ref, g2_ref, wb_ref, wo_ref, x_ref,
                  n1_ref, gate_ref, n2_ref, sc_ref, sh_ref, xm_ref, h2_ref):
    m = (g0_ref[...].astype(F32) * _dot(yd_ref[...], wb_ref[0])
         + g1_ref[...].astype(F32) * _dot(yl_ref[...], wb_ref[1])
         + g2_ref[...].astype(F32) * _dot(yn_ref[...], wb_ref[2]))
    mo = _dot(m.astype(BF16), wo_ref[...])
    xm = x_ref[...] + gate_ref[...] * _rms(mo, n1_ref[...])
    xm_ref[...] = xm
    h2 = _rms(xm, n2_ref[...]) * (1.0 + sc_ref[...]) + sh_ref[...]
    h2_ref[...] = h2.astype(h2_ref.dtype)


def _merge(yd, yl, yn, gates, wb, wo, x, n1, n2, mod, row_fn, tm, h2_dtype):
    n, d = x.shape
    bw = BRANCH_WIDTH
    p, gcol = gates
    resident = dict(pipeline_mode=pl.Buffered(1))
    return pl.pallas_call(
        _merge_kernel,
        grid=(n // tm,),
        in_specs=[pl.BlockSpec((tm, bw), lambda i: (i, 0)),
                  pl.BlockSpec((tm, bw), lambda i: (i, 0)),
                  pl.BlockSpec((tm, bw), lambda i: (i, 0)),
                  pl.BlockSpec((tm, d), lambda i: (i, gcol)),
                  pl.BlockSpec((tm, d), lambda i: (i, gcol + 1)),
                  pl.BlockSpec((tm, d), lambda i: (i, gcol + 2)),
                  pl.BlockSpec((3, bw, d), lambda i: (0, 0, 0), **resident),
                  pl.BlockSpec((d, d), lambda i: (0, 0), **resident),
                  pl.BlockSpec((tm, d), lambda i: (i, 0)),
                  _vec_spec(d),
                  _mod_spec(d, M_G1, row_fn),
                  _vec_spec(d),
                  _mod_spec(d, M_SC2, row_fn),
                  _mod_spec(d, M_SH2, row_fn)],
        out_specs=[pl.BlockSpec((tm, d), lambda i: (i, 0)),
                   pl.BlockSpec((tm, d), lambda i: (i, 0))],
        out_shape=[jax.ShapeDtypeStruct((n, d), F32),
                   jax.ShapeDtypeStruct((n, d), h2_dtype)],
        compiler_params=_params(("parallel",)),
        name="merge",
    )(yd, yl, yn, p, p, p, wb, wo, x, n1.reshape(1, d), mod, n2.reshape(1, d), mod, mod)


def _swiglu_step(x_ref, wg_ref, wu_ref, wd_ref, acc_ref, f):
    @pl.when(f == 0)
    def _():
        acc_ref[...] = jnp.zeros_like(acc_ref)
    xb = x_ref[...].astype(BF16)
    g = _dot(xb, wg_ref[...])
    u = _dot(xb, wu_ref[...])
    a = (g * jax.nn.sigmoid(g) * u).astype(BF16)
    acc_ref[...] += _dot(a, wd_ref[...])


def _ffn_dense_kernel(x_ref, wg_ref, wu_ref, wd_ref, xm_ref, gate_ref, n3_ref, *rest, nf, with_next):
    if with_next:
        n0_ref, sc_ref, sh_ref, xo_ref, hn_ref, acc_ref = rest
    else:
        xo_ref, acc_ref = rest
    f = pl.program_id(1)
    _swiglu_step(x_ref, wg_ref, wu_ref, wd_ref, acc_ref, f)

    @pl.when(f == nf - 1)
    def _():
        xo = xm_ref[...] + gate_ref[...] * _rms(acc_ref[...], n3_ref[...])
        xo_ref[...] = xo
        if with_next:
            hn = _rms(xo, n0_ref[...]) * (1.0 + sc_ref[...]) + sh_ref[...]
            hn_ref[...] = hn.astype(hn_ref.dtype)


def _ffn_dense(h2, wg, wu, wd, xm, n3, mod, row_fn, tm, tf, nxt):
    n, d = h2.shape
    fp = wg.shape[1]
    nf = fp // tf
    with_next = nxt is not None
    in_specs = [pl.BlockSpec((tm, d), lambda i, f: (i, 0)),
                pl.BlockSpec((d, tf), lambda i, f: (0, f)),
                pl.BlockSpec((d, tf), lambda i, f: (0, f)),
                pl.BlockSpec((tf, d), lambda i, f: (f, 0)),
                pl.BlockSpec((tm, d), lambda i, f: (i, 0)),
                _mod_spec(d, M_G2, row_fn),
                _vec_spec(d)]
    args = [h2, wg, wu, wd, xm, mod, n3.reshape(1, d)]
    out_specs = [pl.BlockSpec((tm, d), lambda i, f: (i, 0))]
    out_shape = [jax.ShapeDtypeStruct((n, d), F32)]
    if with_next:
        n0, mod_next = nxt
        in_specs += [_vec_spec(d), _mod_spec(d, M_SC1, row_fn), _mod_spec(d, M_SH1, row_fn)]
        args += [n0.reshape(1, d), mod_next, mod_next]
        out_specs.append(pl.BlockSpec((tm, d), lambda i, f: (i, 0)))
        out_shape.append(jax.ShapeDtypeStruct((n, d), BF16))
    kern = functools.partial(_ffn_dense_kernel, nf=nf, with_next=with_next)
    res = pl.pallas_call(
        kern,
        grid=(n // tm, nf),
        in_specs=in_specs,
        out_specs=out_specs,
        out_shape=out_shape,
        scratch_shapes=[pltpu.VMEM((tm, d), F32)],
        compiler_params=_params(("parallel", "arbitrary")),
        name="ffn_dense",
    )(*args)
    return (res[0], res[1]) if with_next else (res[0], None)


def _row_copy(src_hbm, src_row, dst_ref, dst_row, sem):
    return pltpu.make_async_copy(src_hbm.at[pl.ds(src_row, 1)], dst_ref.at[pl.ds(dst_row, 1)], sem)


def _ffn_grouped_kernel(te_ref, nt_ref, used_ref, src_ref, x_hbm, wg_ref, wu_ref, wd_ref, o_ref,
                        rows_ref, xb_ref, sem, *, nf, tm, heights, grid_tiles):
    i = pl.program_id(0)
    f = pl.program_id(1)
    nt = nt_ref[0]
    slot = i % 2
    issue_steps = max(1, nf - 2)
    rows_per_step = tm // issue_steps

    def start_row(tile, slot_, r):
        _row_copy(x_hbm, src_ref[tile * tm + r], rows_ref.at[slot_], r, sem.at[slot_]).start()

    def wait_rows(slot_):
        pltpu.make_async_copy(x_hbm.at[pl.ds(0, tm)], rows_ref.at[slot_], sem.at[slot_]).wait()

    @pl.when((i == 0) & (f == 0))
    def _():
        def body(r, c):
            start_row(0, 0, r)
            return c
        lax.fori_loop(0, tm, body, 0)

    @pl.when((i < nt) & (f == 0))
    def _():
        wait_rows(slot)
        xb_ref[...] = rows_ref[slot].astype(BF16)
        o_ref[...] = jnp.zeros_like(o_ref)

    def step(request_rows, m):
        if request_rows:
            nxt = jnp.minimum(i + 1, grid_tiles - 1)
            for j in range(rows_per_step):
                start_row(nxt, 1 - slot, f * rows_per_step + j)
        xb = xb_ref[:m, :]
        g = _dot(xb, wg_ref[...])
        u = _dot(xb, wu_ref[...])
        a = (g * jax.nn.sigmoid(g) * u).astype(BF16)
        o_ref[:m, :] += _dot(a, wd_ref[...])

    used = used_ref[i]
    for request, in_range in ((True, f < issue_steps), (False, f >= issue_steps)):
        if request or issue_steps < nf:
            lower = 0
            for k, m in enumerate(heights):
                fits = (used > lower) if k == len(heights) - 1 else (used > lower) & (used <= m)
                pl.when((i < nt) & in_range & fits)(functools.partial(step, request, m))
                lower = m

    @pl.when((i == nt - 1) & (f == nf - 1))
    def _():
        wait_rows(1 - slot)

    @pl.when((i >= nt) & (f == nf - 1))
    def _():
        o_ref[...] = jnp.zeros_like(o_ref)


def _ffn_grouped(h2, src, wg, wu, wd, tile_expert, n_tiles, tile_used, p_max, tm, tf):
    d = h2.shape[1]
    fe = wg.shape[2]
    nf = fe // tf
    assert tm % max(1, nf - 2) == 0, (tm, nf)

    def fidx(i, f, nt):
        return jnp.where(i < nt[0], f, nf - 1)

    heights = tuple(sorted({min(tm, -(-(tm * k // MOE_HEIGHTS) // 16) * 16)
                            for k in range(1, MOE_HEIGHTS + 1)}))
    kern = functools.partial(_ffn_grouped_kernel, nf=nf, tm=tm, heights=heights,
                             grid_tiles=p_max // tm)
    return pl.pallas_call(
        kern,
        grid_spec=pltpu.PrefetchScalarGridSpec(
            num_scalar_prefetch=4,
            grid=(p_max // tm, nf),
            in_specs=[pl.BlockSpec(memory_space=pl.ANY),
                      pl.BlockSpec((None, d, tf), lambda i, f, te, nt, *_: (te[i], 0, fidx(i, f, nt))),
                      pl.BlockSpec((None, d, tf), lambda i, f, te, nt, *_: (te[i], 0, fidx(i, f, nt))),
                      pl.BlockSpec((None, tf, d), lambda i, f, te, nt, *_: (te[i], fidx(i, f, nt), 0))],
            out_specs=pl.BlockSpec((tm, d), lambda i, f, *_: (i, 0)),
            scratch_shapes=[pltpu.VMEM((2, tm, d), h2.dtype),
                            pltpu.VMEM((tm, d), BF16),
                            pltpu.SemaphoreType.DMA((2,))]),
        out_shape=jax.ShapeDtypeStruct((p_max, d), F32),
        compiler_params=pltpu.CompilerParams(dimension_semantics=("arbitrary", "arbitrary"),
                                             vmem_limit_bytes=VMEM_LIMIT,
                                             disable_bounds_checks=True),
        name="ffn_grouped",
    )(tile_expert, n_tiles, tile_used, src, h2, wg, wu, wd)


def _router_kernel(h_ref, wr_ref, o_ref, cnt_ref, carry_ref, *, n_experts):
    i = pl.program_id(0)

    @pl.when(i == 0)
    def _():
        carry_ref[...] = jnp.zeros_like(carry_ref)

    logits = _dot(h_ref[...].astype(BF16), wr_ref[...])
    tm = logits.shape[0]
    lane_i = lax.broadcasted_iota(I32, logits.shape, 1)
    lane = lane_i.astype(F32)
    logits = jnp.where(lane_i < n_experts, logits, -jnp.inf)
    m1 = jnp.max(logits, axis=-1, keepdims=True)
    i1 = jnp.min(jnp.where(logits == m1, lane, float(LANES)), axis=-1, keepdims=True)
    rest = jnp.where(lane == i1, -jnp.inf, logits)
    m2 = jnp.max(rest, axis=-1, keepdims=True)
    i2 = jnp.min(jnp.where(rest == m2, lane, float(LANES)), axis=-1, keepdims=True)
    e21 = jnp.exp(m2 - m1)
    w1 = 1.0 / (1.0 + e21)
    w2 = e21 * w1

    sel1 = lane == i1
    sel2 = lane == i2
    onehot = jnp.where(sel1 | sel2, 1.0, 0.0).astype(BF16)
    r = lax.broadcasted_iota(I32, (tm, tm), 0)
    c = lax.broadcasted_iota(I32, (tm, tm), 1)
    strict_lower = jnp.where(c < r, 1.0, 0.0).astype(BF16)
    before = _dot(strict_lower, onehot) + carry_ref[0:1, :]
    r1 = jnp.sum(jnp.where(sel1, before, 0.0), axis=-1, keepdims=True)
    r2 = jnp.sum(jnp.where(sel2, before, 0.0), axis=-1, keepdims=True)
    total = carry_ref[0:1, :] + jnp.sum(onehot.astype(F32), axis=0, keepdims=True)
    carry_ref[...] = jnp.broadcast_to(total, carry_ref.shape)
    cnt_ref[...] = jnp.broadcast_to(total, cnt_ref.shape)

    packed = jnp.where(lane_i == 0, i1, 0.0)
    packed = jnp.where(lane_i == 1, i2, packed)
    packed = jnp.where(lane_i == 2, r1, packed)
    packed = jnp.where(lane_i == 3, r2, packed)
    packed = jnp.where(lane_i == 4, w1, packed)
    packed = jnp.where(lane_i == 5, w2, packed)
    o_ref[...] = packed


def _router(h2, w_router, tm):
    n, d = h2.shape
    n_experts = w_router.shape[1]
    wr = jnp.pad(w_router, ((0, 0), (0, LANES - n_experts))).astype(BF16)
    kern = functools.partial(_router_kernel, n_experts=n_experts)
    return pl.pallas_call(
        kern,
        grid=(n // tm,),
        in_specs=[pl.BlockSpec((tm, d), lambda i: (i, 0)),
                  pl.BlockSpec((d, LANES), lambda i: (0, 0))],
        out_specs=[pl.BlockSpec((tm, LANES), lambda i: (i, 0)),
                   pl.BlockSpec((SUBLANES, LANES), lambda i: (0, 0))],
        out_shape=[jax.ShapeDtypeStruct((n, LANES), F32),
                   jax.ShapeDtypeStruct((SUBLANES, LANES), F32)],
        scratch_shapes=[pltpu.VMEM((SUBLANES, LANES), F32)],
        compiler_params=_params(("arbitrary",)),
        name="router",
    )(h2, wr)


def _combine_kernel(d1_ref, d2_ref, y_hbm, r_ref, xm_ref, gate_ref, n3_ref, xo_ref, buf_ref, sem, *,
                    tm, n_tiles):
    i = pl.program_id(0)
    slot = i % 2
    dest = (d1_ref, d2_ref)

    def start_rows(tile, slot_, r):
        for k in range(TOP_K):
            _row_copy(y_hbm, dest[k][tile * tm + r], buf_ref.at[slot_, k], r, sem.at[slot_]).start()

    def wait_rows(slot_):
        for k in range(TOP_K):
            pltpu.make_async_copy(y_hbm.at[pl.ds(0, tm)], buf_ref.at[slot_, k], sem.at[slot_]).wait()

    @pl.when(i == 0)
    def _():
        def body(r, c):
            start_rows(0, 0, r)
            return c
        lax.fori_loop(0, tm, body, 0)

    wait_rows(slot)
    nxt = jnp.minimum(i + 1, n_tiles - 1)
    for r in range(tm):
        start_rows(nxt, 1 - slot, r)
    route = r_ref[...]
    y = route[:, 4:5] * buf_ref[slot, 0] + route[:, 5:6] * buf_ref[slot, 1]
    xo_ref[...] = xm_ref[...] + gate_ref[...] * _rms(y, n3_ref[...])

    @pl.when(i == n_tiles - 1)
    def _():
        wait_rows(1 - slot)


def _combine(yo, dest1, dest2, route, xm, n3, mod, row_fn, tm):
    n, d = xm.shape
    kern = functools.partial(_combine_kernel, tm=tm, n_tiles=n // tm)
    return pl.pallas_call(
        kern,
        grid_spec=pltpu.PrefetchScalarGridSpec(
            num_scalar_prefetch=2,
            grid=(n // tm,),
            in_specs=[pl.BlockSpec(memory_space=pl.ANY),
                      pl.BlockSpec((tm, LANES), lambda i, *_: (i, 0)),
                      pl.BlockSpec((tm, d), lambda i, *_: (i, 0)),
                      _mod_spec(d, M_G2, row_fn),
                      _vec_spec(d)],
            out_specs=pl.BlockSpec((tm, d), lambda i, *_: (i, 0)),
            scratch_shapes=[pltpu.VMEM((2, TOP_K, tm, d), F32), pltpu.SemaphoreType.DMA((2,))]),
        out_shape=jax.ShapeDtypeStruct((n, d), F32),
        compiler_params=pltpu.CompilerParams(dimension_semantics=("arbitrary",),
                                             vmem_limit_bytes=VMEM_LIMIT,
                                             disable_bounds_checks=True),
        name="moe_combine",
    )(dest1, dest2, yo, route, xm, mod, n3.reshape(1, d))


def _moe(h2, xm, w_router, wg, wu, wd, n3, mod, row_fn, tm_g, tf):
    n, d = h2.shape
    n_experts = w_router.shape[1]
    route, counts = _router(h2, w_router, min(512, n))
    e1 = route[:, 0].astype(I32)
    e2 = route[:, 1].astype(I32)
    r1 = route[:, 2].astype(I32)
    r2 = route[:, 3].astype(I32)
    cnt = counts[0, :n_experts].astype(I32)
    padded = ((cnt + tm_g - 1) // tm_g) * tm_g
    ends = jnp.cumsum(padded)
    offs = ends - padded
    dest1 = offs[e1] + r1
    dest2 = offs[e2] + r2
    p_max = ((TOP_K * n + n_experts * (tm_g - 1)) // tm_g) * tm_g
    n_rows = ends[-1:]
    tok = jnp.arange(n, dtype=I32)
    src = jnp.zeros((p_max,), I32).at[dest1].set(tok).at[dest2].set(tok)
    tile_start = jnp.arange(p_max // tm_g, dtype=I32) * tm_g
    tile_expert = jnp.minimum(jnp.sum(tile_start[:, None] >= ends[None, :], axis=1),
                              n_experts - 1).astype(I32)
    last_expert = tile_expert[jnp.maximum(n_rows[0] // tm_g - 1, 0)]
    tile_expert = jnp.where(tile_start < n_rows[0], tile_expert, last_expert)
    tile_used = jnp.clip((offs + cnt)[tile_expert] - tile_start, 0, tm_g).astype(I32)
    yo = _ffn_grouped(h2, src, wg, wu, wd, tile_expert, n_rows // tm_g, tile_used, p_max, tm_g, tf)
    return _combine(yo, dest1, dest2, route, xm, n3, mod, row_fn, min(256, n))


def _rope_tables(seq):
    inv = ROPE_BASE ** (-jnp.arange(ROPE_FREQ, dtype=F32) / ROPE_FREQ)
    t = jnp.arange(seq, dtype=I32)
    pos = jnp.stack([t // GRID_W, t % GRID_W], axis=-1).astype(F32)
    ang = pos[:, :, None] * inv
    cos, sin = jnp.cos(ang), jnp.sin(ang)
    cos = jnp.concatenate([cos, cos], axis=-1).reshape(seq, DIFF_HEAD_DIM)
    sin = jnp.concatenate([-sin, sin], axis=-1).reshape(seq, DIFF_HEAD_DIM)
    reps = LANES // DIFF_HEAD_DIM
    return jnp.tile(cos, (1, reps)), jnp.tile(sin, (1, reps))


def _pad_cols(w, mult):
    pad = (-w.shape[-1]) % mult
    return jnp.pad(w, [(0, 0)] * (w.ndim - 1) + [(0, pad)]) if pad else w


def _pad_rows(w, mult):
    pad = (-w.shape[-2]) % mult
    return jnp.pad(w, [(0, 0)] * (w.ndim - 2) + [(0, pad), (0, 0)]) if pad else w


def kernel(x, c, ctx, c_ctx, norm_g, w_ada, b_ada, w_in, diff_lambda, diff_subln_g, lru_conv_w, lru_conv_b, lru_gate_w, lru_gate_b, lru_lambda, na_rpb, w_branch, w_merge, b_merge, w_out, ffn_w_gate, ffn_w_up, ffn_w_down, moe_w_router, moe_w_gate, moe_w_up, moe_w_down):
    bsz, seq, d = x.shape
    clen = ctx.shape[1]
    depth = w_in.shape[0]
    n, nc = bsz * seq, bsz * clen
    x_lat = x.reshape(n, d)
    x_ctx = ctx.reshape(nc, d)

    mr = -(-(bsz + 1) // SUBLANES) * SUBLANES
    cvec = jnp.concatenate([c, c_ctx[None], jnp.zeros((mr - bsz - 1, d), F32)], axis=0)
    mod_all = _ada(cvec, w_ada, b_ada).reshape(depth, mr, 1, 6 * d)
    cos_t, sin_t = _rope_tables(seq)

    tm_lat, tm_ctx = min(1024, seq), min(1024, nc, seq)
    tm_mix_lat, tm_mix_ctx = min(256, seq), min(256, nc)
    tm_ffn_lat, tm_ffn_ctx = min(512, seq), min(512, nc)
    tf = min(512, d)
    tf_moe = min(MOE_TF, d)
    tq = min(DIFF_TQ, seq)

    def lat_row(tile_rows):
        per_batch = seq // tile_rows
        return lambda i: i // per_batch

    def ctx_row(i):
        return bsz

    h_lat = _prenorm(x_lat, norm_g[0, 0], mod_all[0], lat_row(tm_lat), tm_lat)
    h_ctx = _prenorm(x_ctx, norm_g[0, 0], mod_all[0], ctx_row, tm_ctx)

    rest_names = ("dv", "lx", "lg", "nq", "nk", "nv")
    part_scale = [1.0] * N_IN_PARTS
    part_scale[P_DQ], part_scale[P_NQ] = DIFF_QSCALE, NA_QSCALE
    col_scale = jnp.repeat(jnp.asarray(part_scale, F32), BRANCH_WIDTH)[None]
    scale_qk, scale_rest = col_scale[:, :2 * BRANCH_WIDTH], col_scale[:, 2 * BRANCH_WIDTH:]
    diff_steps = bsz * DIFF_HEADS * (seq // tq)
    moe_src = (moe_w_gate, moe_w_up, moe_w_down)

    def halves(w):
        return w.reshape(2, -1, w.shape[-1])

    def side_cast_ok(mi):
        return moe_w_router.shape[-1] % 2 == 0 and all(
            halves(w[mi]).shape[1] % (16 * diff_steps) == 0 for w in moe_src)

    moe_bf16 = {}
    for l in range(depth):
        need_ctx = l < depth - 1
        mod = mod_all[l]
        lam_init = 0.8 - 0.6 * math.exp(-0.3 * l)
        w_qk = w_in[l][:, :2 * BRANCH_WIDTH].astype(BF16)
        w_rest = w_in[l][:, 2 * BRANCH_WIDTH:].astype(BF16)
        w_gate = w_merge[l].astype(BF16)
        b_gate = b_merge[l][None]

        def project(h, tm, rope):
            qk = _proj(h, w_qk, scale_qk, cos_t, sin_t, seq, tm, PROJ_TN_ROPE if rope else PROJ_TN,
                       "rope" if rope else "plain")
            rest = _proj(h, w_rest, scale_rest, cos_t, sin_t, seq, tm, PROJ_TN, "plain")
            parts = {"dq": (qk, 0), "dk": (qk, 1)}
            parts.update({name: (rest, k) for k, name in enumerate(rest_names)})
            return parts

        lat = project(h_lat, tm_lat, True)
        ctxp = project(h_ctx, tm_ctx, False)
        gate_lat = _proj(h_lat, w_gate, b_gate, cos_t, sin_t, seq, tm_lat, PROJ_TN, "gate")

        mi, half = l // 2, l % 2
        jobs = ()
        if mi < moe_w_gate.shape[0] and side_cast_ok(mi):
            jobs = tuple((halves(w[mi]), half, moe_bf16[mi][k] if half else None)
                         for k, w in enumerate(moe_src))
        y_diff, cast = _diff_attn_lat(lat, ctxp, diff_lambda[l], diff_subln_g[l], lam_init,
                                      bsz, seq, clen, tq, jobs)
        if jobs:
            moe_bf16[mi] = cast
        y_lru, y_lru_c = _lru(lat, ctxp, lru_conv_w[l], lru_conv_b[l], lru_gate_w[l],
                              lru_gate_b[l], lru_lambda[l], bsz, seq, clen, need_ctx)
        y_na = _na_lat(lat, ctxp, na_rpb[l], bsz, seq, clen)

        wb = w_branch[l].astype(BF16)
        wo = w_out[l].astype(BF16)
        is_moe = l % 2 == 1
        h2_dtype = F32 if is_moe else BF16
        xm_lat, h2_lat = _merge(y_diff, y_lru, y_na, (gate_lat, 0), wb, wo, x_lat, norm_g[l, 1],
                                norm_g[l, 2], mod, lat_row(tm_mix_lat), tm_mix_lat, h2_dtype)
        if need_ctx:
            y_diff_c = _diff_attn_ctx(ctxp, diff_lambda[l], diff_subln_g[l], lam_init, bsz, clen)
            y_na_c = _na_ctx(ctxp, bsz, clen)
            gate_ctx = _proj(h_ctx, w_gate, b_gate, cos_t, sin_t, seq, tm_ctx, PROJ_TN, "gate")
            xm_ctx, h2_ctx = _merge(y_diff_c, y_lru_c, y_na_c, (gate_ctx, 0), wb, wo, x_ctx,
                                    norm_g[l, 1], norm_g[l, 2], mod, ctx_row, tm_mix_ctx, h2_dtype)

        nxt = (norm_g[l + 1, 0], mod_all[l + 1]) if need_ctx else None
        i = l // 2
        if is_moe:
            if i in moe_bf16:
                wg, wu, wd = (c.reshape(w.shape[1:]) for c, w in zip(moe_bf16[i], moe_src))
            else:
                wg, wu, wd = (w[i].astype(BF16) for w in moe_src)
            tm_g = min(MOE_TILE, n)
            x_lat = _moe(h2_lat, xm_lat, moe_w_router[i], wg, wu, wd, norm_g[l, 3], mod,
                         lat_row(min(256, n)), tm_g, tf_moe)
            if need_ctx:
                x_ctx = _moe(h2_ctx, xm_ctx, moe_w_router[i], wg, wu, wd, norm_g[l, 3], mod,
                             ctx_row, min(MOE_TILE, nc), tf_moe)
                h_lat = _prenorm(x_lat, nxt[0], nxt[1], lat_row(tm_lat), tm_lat)
                h_ctx = _prenorm(x_ctx, nxt[0], nxt[1], ctx_row, tm_ctx)
        else:
            wg = _pad_cols(ffn_w_gate[i], tf).astype(BF16)
            wu = _pad_cols(ffn_w_up[i], tf).astype(BF16)
            wd = _pad_rows(ffn_w_down[i], tf).astype(BF16)
            x_lat, h_lat = _ffn_dense(h2_lat, wg, wu, wd, xm_lat, norm_g[l, 3], mod,
                                      lat_row(tm_ffn_lat), tm_ffn_lat, tf, nxt)
            if need_ctx:
                x_ctx, h_ctx = _ffn_dense(h2_ctx, wg, wu, wd, xm_ctx, norm_g[l, 3], mod,
                                          ctx_row, tm_ffn_ctx, tf, nxt)
    return x_lat.reshape(bsz, seq, d)
```

```python
import functools
import math

import numpy as np
import jax
import jax.numpy as jnp
from jax import lax
from jax.experimental import pallas as pl
from jax.experimental.pallas import tpu as pltpu

F32 = jnp.float32
BF16 = jnp.bfloat16
I32 = jnp.int32

EPS = 1e-6
GRID_W = 64
BRANCH_WIDTH = 1024
N_IN_PARTS = 8
IN_WIDTH = N_IN_PARTS * BRANCH_WIDTH
DIFF_HEADS = 8
DIFF_HEAD_DIM = 64
DIFF_V_DIM = 2 * DIFF_HEAD_DIM
ROPE_BASE = 10000.0
ROPE_FREQ = DIFF_HEAD_DIM // 4
LRU_BLOCKS = 8
LRU_BLOCK_DIM = BRANCH_WIDTH // LRU_BLOCKS
LRU_CONV_W = 4
LRU_C = 8.0
NA_HEADS = 8
NA_HEAD_DIM = BRANCH_WIDTH // NA_HEADS
NA_WIN_R = 8
NA_WIN_C = 16
TOP_K = 2
MOE_TILE = 560
MOE_TF = 1024
DIFF_TQ = 1024
PROJ_TN = 2048
PROJ_TN_ROPE = 1024

LANES = 128
SUBLANES = 8
VMEM_LIMIT = 56 * 1024 * 1024
NEG = -1e30
LOG2E = 1.0 / math.log(2.0)
DIFF_QSCALE = DIFF_HEAD_DIM ** -0.5 * LOG2E
NA_QSCALE = NA_HEAD_DIM ** -0.5 * LOG2E

P_DQ, P_DK, P_DV, P_LX, P_LG, P_NQ, P_NK, P_NV = range(8)
M_SH1, M_SC1, M_G1, M_SH2, M_SC2, M_G2 = range(6)


def _params(sem):
    return pltpu.CompilerParams(dimension_semantics=sem, vmem_limit_bytes=VMEM_LIMIT)


def _rms(x, g):
    return x * lax.rsqrt(jnp.mean(x * x, axis=-1, keepdims=True) + EPS) * g


def _dot(a, b):
    return jnp.dot(a, b, preferred_element_type=F32)


def _dot_nt(a, b):
    return lax.dot_general(a, b, (((1,), (1,)), ((), ())), preferred_element_type=F32)


def _ada_kernel(c_ref, w_ref, b_ref, o_ref):
    c = c_ref[...]
    s = (c * jax.nn.sigmoid(c)).astype(BF16)
    o_ref[...] = _dot(s, w_ref[...].astype(BF16)) + b_ref[...]


def _ada(cvec, w_ada, b_ada):
    n_layers, d, n6 = w_ada.shape
    mr = cvec.shape[0]
    tn = min(1024, d)
    return pl.pallas_call(
        _ada_kernel,
        grid=(n_layers, n6 // tn),
        in_specs=[pl.BlockSpec((mr, d), lambda l, j: (0, 0)),
                  pl.BlockSpec((None, d, tn), lambda l, j: (l, 0, j)),
                  pl.BlockSpec((None, 1, tn), lambda l, j: (l, 0, j))],
        out_specs=pl.BlockSpec((None, mr, tn), lambda l, j: (l, 0, j)),
        out_shape=jax.ShapeDtypeStruct((n_layers, mr, n6), F32),
        compiler_params=_params(("arbitrary", "arbitrary")),
        name="ada_mod",
    )(cvec, w_ada, b_ada.reshape(n_layers, 1, n6))


def _mod_spec(d, chunk, row_fn):
    return pl.BlockSpec((None, 1, d), lambda i, *_: (row_fn(i), 0, chunk))


def _vec_spec(d):
    return pl.BlockSpec((1, d), lambda i, *_: (0, 0))


def _prenorm_kernel(x_ref, g_ref, sc_ref, sh_ref, o_ref):
    y = _rms(x_ref[...], g_ref[...])
    o_ref[...] = (y * (1.0 + sc_ref[...]) + sh_ref[...]).astype(o_ref.dtype)


def _prenorm(x, g, mod, row_fn, tm):
    n, d = x.shape
    return pl.pallas_call(
        _prenorm_kernel,
        grid=(n // tm,),
        in_specs=[pl.BlockSpec((tm, d), lambda i: (i, 0)),
                  _vec_spec(d),
                  _mod_spec(d, M_SC1, row_fn),
                  _mod_spec(d, M_SH1, row_fn)],
        out_specs=pl.BlockSpec((tm, d), lambda i: (i, 0)),
        out_shape=jax.ShapeDtypeStruct((n, d), BF16),
        compiler_params=_params(("parallel",)),
        name="prenorm",
    )(x, g.reshape(1, d), mod, mod)


def _proj_kernel(h_ref, w_ref, v_ref, cos_ref, sin_ref, o_ref, *, kind):
    acc = _dot(h_ref[...], w_ref[...])
    tn = acc.shape[1]
    if kind == "gate":
        o_ref[...] = jax.nn.sigmoid(acc + v_ref[...]).astype(o_ref.dtype)
    elif kind == "rope":
        lane = lax.broadcasted_iota(I32, acc.shape, 1)
        first_half = (lane % (2 * ROPE_FREQ)) < ROPE_FREQ
        partner = jnp.where(first_half, pltpu.roll(acc, tn - ROPE_FREQ, 1),
                            pltpu.roll(acc, ROPE_FREQ, 1))
        cos = jnp.tile(cos_ref[...], (1, tn // LANES)) * v_ref[...]
        sin = jnp.tile(sin_ref[...], (1, tn // LANES)) * v_ref[...]
        o_ref[...] = (acc * cos + partner * sin).astype(o_ref.dtype)
    else:
        o_ref[...] = (acc * v_ref[...]).astype(o_ref.dtype)


def _proj(h, w, v, cos_t, sin_t, seq, tm, tn, kind):
    n, d = h.shape
    nw = w.shape[1]
    tn = min(tn, nw)
    tpb = seq // tm
    kern = functools.partial(_proj_kernel, kind=kind)
    return pl.pallas_call(
        kern,
        grid=(n // tm, nw // tn),
        in_specs=[pl.BlockSpec((tm, d), lambda i, j: (i, 0)),
                  pl.BlockSpec((d, tn), lambda i, j: (0, j)),
                  pl.BlockSpec((1, tn), lambda i, j: (0, j)),
                  pl.BlockSpec((tm, LANES), lambda i, j: (i % tpb, 0)),
                  pl.BlockSpec((tm, LANES), lambda i, j: (i % tpb, 0))],
        out_specs=pl.BlockSpec((tm, tn), lambda i, j: (i, j)),
        out_shape=jax.ShapeDtypeStruct((n, nw), BF16),
        compiler_params=_params(("parallel", "arbitrary")),
        name="proj_" + kind,
    )(h, w, v, cos_t, sin_t)


DIFF_KEY_CHUNK = 512


def _diff_kernel(*refs, lam_init, has_lat, n_cast, n_alias=0):
    n_in = 7 if has_lat else 5
    n_extra = n_cast + n_alias
    for src_ref, dst_ref in zip(refs[n_in:n_in + n_cast], refs[n_in + n_extra + 1:]):
        dst_ref[0] = src_ref[...].astype(dst_ref.dtype)
        if dst_ref.shape[0] == 2:
            dst_ref[1] = jnp.zeros(dst_ref.shape[1:], dst_ref.dtype)
    if has_lat:
        lamv_ref, sg_ref, q_ref, k_ref, v_ref, kc_ref, vc_ref = refs[:n_in]
    else:
        lamv_ref, sg_ref, q_ref, kc_ref, vc_ref = refs[:n_in]
    o_ref = refs[n_in + n_extra]
    lv = lamv_ref[...]
    lam = (jnp.exp(jnp.sum(lv[0:1] * lv[1:2], axis=-1, keepdims=True))
           - jnp.exp(jnp.sum(lv[2:3] * lv[3:4], axis=-1, keepdims=True)) + lam_init)
    q = q_ref[...]
    first = lax.broadcasted_iota(I32, q.shape, 1) < DIFF_HEAD_DIM
    zero = jnp.zeros_like(q)
    qz = (jnp.where(first, q, zero), jnp.where(first, zero, q))
    chunks = [(kc_ref, vc_ref, 0, kc_ref.shape[0])]
    if has_lat:
        kt = min(DIFF_KEY_CHUNK, k_ref.shape[0])
        chunks += [(k_ref, v_ref, j * kt, kt) for j in range(k_ref.shape[0] // kt)]

    def qk(c, chunk):
        k, _, off, n = chunk
        return _dot_nt(qz[c], k[off:off + n, :])

    def row_max(parts):
        m = jnp.max(parts[0], axis=-1, keepdims=True)
        for p in parts[1:]:
            m = jnp.maximum(m, jnp.max(p, axis=-1, keepdims=True))
        return m

    def softmax_av(s_parts, between=None):
        m = row_max(s_parts)
        den, out = 0.0, 0.0
        for j, (_, v, off, n) in enumerate(chunks):
            if between is not None:
                between(j)
            e = jnp.exp2(s_parts[j] - m)
            den = den + jnp.sum(e, axis=-1, keepdims=True)
            out = out + _dot(e.astype(BF16), v[off:off + n, :])
        return out, den

    s1 = [qk(0, ch) for ch in chunks]
    s2 = []
    o1, den1 = softmax_av(s1, between=lambda j: s2.append(qk(1, chunks[j])))
    o2, den2 = softmax_av(s2)
    o = o1 * (1.0 / den1) - o2 * (lam / den2)
    y = _rms(o, sg_ref[...]) * (1.0 - lam_init)
    o_ref[...] = y.astype(o_ref.dtype)


def _diff_attn_lat(lat, ctxp, lamv, subln_g, lam_init, bsz, seq, clen, tq, cast_jobs=()):
    nq = seq // tq
    n_steps = bsz * DIFF_HEADS * nq
    hw = DIFF_V_DIM
    hpb = BRANCH_WIDTH // hw

    def spec(rows, src, row_fn):
        blk = src[1] * hpb
        return pl.BlockSpec((rows, hw), lambda b, h, i: (row_fn(b, i), blk + h))

    def step(b, h, i):
        return (b * DIFF_HEADS + h) * nq + i

    n_base = 7
    cast_in, cast_out, cast_shapes, cast_args, aliases = [], [], [], [], {}
    for k, (arr, half, dst) in enumerate(cast_jobs):
        _, rows, cols = arr.shape
        per_step = rows // n_steps
        assert per_step * n_steps == rows and per_step % 16 == 0
        cast_in.append(pl.BlockSpec((None, per_step, cols), lambda b, h, i, half=half: (half, step(b, h, i), 0)))
        cast_args.append(arr)
        cast_shapes.append(jax.ShapeDtypeStruct(arr.shape, BF16))
        if dst is None:
            cast_out.append(pl.BlockSpec((2, per_step, cols), lambda b, h, i: (0, step(b, h, i), 0)))
        else:
            assert half == 1
            cast_out.append(pl.BlockSpec((1, per_step, cols), lambda b, h, i: (1, step(b, h, i), 0)))
    for k, (arr, half, dst) in enumerate(cast_jobs):
        if dst is not None:
            aliases[n_base + len(cast_jobs) + len(aliases)] = 1 + k
            cast_in.append(pl.BlockSpec(memory_space=pl.ANY))
            cast_args.append(dst)
    n_alias = len(aliases)
    kern = functools.partial(_diff_kernel, lam_init=lam_init, has_lat=True, n_cast=len(cast_jobs),
                             n_alias=n_alias)
    res = pl.pallas_call(
        kern,
        grid=(bsz, DIFF_HEADS, nq),
        in_specs=[pl.BlockSpec((4, DIFF_HEAD_DIM), lambda b, h, i: (0, 0)),
                  pl.BlockSpec((1, hw), lambda b, h, i: (0, 0)),
                  spec(tq, lat["dq"], lambda b, i: b * nq + i),
                  spec(seq, lat["dk"], lambda b, i: b),
                  spec(seq, lat["dv"], lambda b, i: b),
                  spec(clen, ctxp["dk"], lambda b, i: b),
                  spec(clen, ctxp["dv"], lambda b, i: b)] + cast_in,
        out_specs=[pl.BlockSpec((tq, hw), lambda b, h, i: (b * nq + i, h))] + cast_out,
        out_shape=[jax.ShapeDtypeStruct((bsz * seq, BRANCH_WIDTH), BF16)] + cast_shapes,
        input_output_aliases=aliases,
        compiler_params=_params(("parallel", "parallel", "arbitrary")),
        name="diff_attn",
    )(lamv, subln_g.reshape(1, hw), lat["dq"][0], lat["dk"][0], lat["dv"][0],
      ctxp["dk"][0], ctxp["dv"][0], *cast_args)
    return res[0], res[1:]


def _diff_attn_ctx(ctxp, lamv, subln_g, lam_init, bsz, clen):
    hw = DIFF_V_DIM
    hpb = BRANCH_WIDTH // hw

    def spec(src):
        blk = src[1] * hpb
        return pl.BlockSpec((clen, hw), lambda b, h: (b, blk + h))

    kern = functools.partial(_diff_kernel, lam_init=lam_init, has_lat=False, n_cast=0)
    return pl.pallas_call(
        kern,
        grid=(bsz, DIFF_HEADS),
        in_specs=[pl.BlockSpec((4, DIFF_HEAD_DIM), lambda b, h: (0, 0)),
                  pl.BlockSpec((1, hw), lambda b, h: (0, 0)),
                  spec(ctxp["dq"]), spec(ctxp["dk"]), spec(ctxp["dv"])],
        out_specs=pl.BlockSpec((clen, hw), lambda b, h: (b, h)),
        out_shape=jax.ShapeDtypeStruct((bsz * clen, BRANCH_WIDTH), BF16),
        compiler_params=_params(("parallel", "parallel")),
        name="diff_attn_ctx",
    )(lamv, subln_g.reshape(1, hw), ctxp["dq"][0], ctxp["dk"][0], ctxp["dv"][0])


LRU_TW = 256
LRU_UNROLL = 4


def _gelu_tanh(x):
    return 0.5 * x * (1.0 + jnp.tanh(math.sqrt(2.0 / math.pi) * (x + 0.044715 * (x * x * x))))


def _tile_scan(a, b, reverse):
    rows = lax.broadcasted_iota(I32, a.shape, 0)
    for s in (1, 2, 4):
        if reverse:
            a_s = pltpu.roll(a, SUBLANES - s, 0)
            b_s = pltpu.roll(b, SUBLANES - s, 0)
            valid = rows < SUBLANES - s
        else:
            a_s = pltpu.roll(a, s, 0)
            b_s = pltpu.roll(b, s, 0)
            valid = rows >= s
        b = jnp.where(valid, a * b_s + b, b)
        a = jnp.where(valid, a * a_s, a)
    return a, b


def _lru_kernel(xl_ref, gl_ref, xc_ref, gc_ref, cw_ref, cb_ref, gw_ref, gb_ref, lam_ref,
                *out_and_scratch, seq, clen, need_ctx):
    if need_ctx:
        yl_ref, yc_ref, af_ref, bf_ref, ab_ref, bb_ref, hf_ref = out_and_scratch
    else:
        yl_ref, af_ref, bf_ref, ab_ref, bb_ref, hf_ref = out_and_scratch
        yc_ref = None
    tot = clen + seq
    x = jnp.concatenate([xc_ref[...], xl_ref[...]], axis=0).astype(F32)
    row = lax.broadcasted_iota(I32, x.shape, 0)
    pos = jnp.where(row < clen, row, row - clen)
    seg_len = jnp.where(row < clen, clen, seq)
    cw = cw_ref[...]
    conv = cb_ref[...] + x * cw[2:3]
    conv = conv + jnp.where(pos >= 2, pltpu.roll(x, 2, 0), 0.0) * cw[0:1]
    conv = conv + jnp.where(pos >= 1, pltpu.roll(x, 1, 0), 0.0) * cw[1:2]
    conv = conv + jnp.where(pos < seg_len - 1, pltpu.roll(x, tot - 1, 0), 0.0) * cw[3:4]
    xb = conv.astype(BF16)
    gb = gb_ref[...]
    lam = lam_ref[...]
    for d, (a_ref, b_ref) in enumerate(((af_ref, bf_ref), (ab_ref, bb_ref))):
        r = jax.nn.sigmoid(_dot(xb, gw_ref[d, 0]) + gb[2 * d:2 * d + 1])
        i = jax.nn.sigmoid(_dot(xb, gw_ref[d, 1]) + gb[2 * d + 1:2 * d + 2])
        z = -lam[d:d + 1]
        softplus = jnp.maximum(z, 0.0) + jnp.log1p(jnp.exp(-jnp.abs(z)))
        log_a = -LRU_C * r * softplus
        a = jnp.exp(log_a)
        beta = jnp.sqrt(-jnp.tanh(log_a) * (a * a + 1.0))
        a_ref[...] = a
        b_ref[...] = beta * i * conv

    n_ct = clen // SUBLANES
    n_t = tot // SUBLANES
    w = x.shape[1]
    assert n_ct % LRU_UNROLL == 0 and n_t % LRU_UNROLL == 0
    rows_per_iter = LRU_UNROLL * SUBLANES

    def fwd_step(t, carry):
        base = pl.multiple_of(t * rows_per_iter, rows_per_iter)
        scans = [_tile_scan(af_ref[pl.ds(base + k * SUBLANES, SUBLANES), :],
                            bf_ref[pl.ds(base + k * SUBLANES, SUBLANES), :], False)
                 for k in range(LRU_UNROLL)]
        for k, (a, b) in enumerate(scans):
            h = a * carry + b
            hf_ref[pl.ds(base + k * SUBLANES, SUBLANES), :] = h
            carry = h[SUBLANES - 1:SUBLANES, :]
        return carry

    lax.fori_loop(0, n_t // LRU_UNROLL, fwd_step, jnp.zeros((1, w), F32))

    def bwd_iter(first_tile, n_tiles, g_ref, y_ref, seg_off):
        def step(t, carry):
            base = pl.multiple_of((first_tile + n_tiles) * SUBLANES - (t + 1) * rows_per_iter,
                                  rows_per_iter)
            offs = [base + k * SUBLANES for k in reversed(range(LRU_UNROLL))]
            scans = [_tile_scan(ab_ref[pl.ds(off, SUBLANES), :], bb_ref[pl.ds(off, SUBLANES), :], True)
                     for off in offs]
            for off, (a, b) in zip(offs, scans):
                h = a * carry + b
                carry = h[0:1, :]
                if y_ref is not None:
                    g = g_ref[pl.ds(off - seg_off, SUBLANES), :].astype(F32)
                    y_ref[pl.ds(off - seg_off, SUBLANES), :] = (
                        (hf_ref[pl.ds(off, SUBLANES), :] + h) * _gelu_tanh(g)).astype(y_ref.dtype)
            return carry
        return step

    carry = lax.fori_loop(0, n_ct // LRU_UNROLL, bwd_iter(0, n_ct, gc_ref, yc_ref, 0),
                          jnp.zeros((1, w), F32))
    lax.fori_loop(0, (n_t - n_ct) // LRU_UNROLL, bwd_iter(n_ct, n_t - n_ct, gl_ref, yl_ref, clen), carry)


def _lru(lat, ctxp, conv_w, conv_b, gate_w, gate_b, lru_lam, bsz, seq, clen, need_ctx):
    n = bsz * seq
    tw = LRU_TW
    nwt = BRANCH_WIDTH // tw
    per = tw // LRU_BLOCK_DIM
    gw = gate_w.reshape(2, 2, nwt, per, LRU_BLOCK_DIM, LRU_BLOCK_DIM)
    eye = jnp.eye(per, dtype=gate_w.dtype)
    gw = jnp.einsum('dgnpcf,pq->dgnpcqf', gw, eye).reshape(2, 2, nwt, tw, tw).astype(BF16)
    gb = gate_b.reshape(4, BRANCH_WIDTH)
    tot = seq + clen
    kern = functools.partial(_lru_kernel, seq=seq, clen=clen, need_ctx=need_ctx)

    def part_spec(rows, src):
        blk = src[1] * nwt
        return pl.BlockSpec((rows, tw), lambda b, j: (b, blk + j))

    out_shape = [jax.ShapeDtypeStruct((n, BRANCH_WIDTH), BF16)]
    out_specs = [pl.BlockSpec((seq, tw), lambda b, j: (b, j))]
    if need_ctx:
        out_shape.append(jax.ShapeDtypeStruct((bsz * clen, BRANCH_WIDTH), BF16))
        out_specs.append(pl.BlockSpec((clen, tw), lambda b, j: (b, j)))
    res = pl.pallas_call(
        kern,
        grid=(bsz, nwt),
        in_specs=[part_spec(seq, lat["lx"]), part_spec(seq, lat["lg"]),
                  part_spec(clen, ctxp["lx"]), part_spec(clen, ctxp["lg"]),
                  pl.BlockSpec((LRU_CONV_W, tw), lambda b, j: (0, j)),
                  pl.BlockSpec((1, tw), lambda b, j: (0, j)),
                  pl.BlockSpec((2, 2, None, tw, tw), lambda b, j: (0, 0, j, 0, 0)),
                  pl.BlockSpec((4, tw), lambda b, j: (0, j)),
                  pl.BlockSpec((2, tw), lambda b, j: (0, j))],
        out_specs=out_specs,
        out_shape=out_shape,
        scratch_shapes=[pltpu.VMEM((tot, tw), F32) for _ in range(5)],
        compiler_params=_params(("parallel", "parallel")),
        name="rglru",
    )(lat["lx"][0], lat["lg"][0], ctxp["lx"][0], ctxp["lg"][0], conv_w,
      conv_b.reshape(1, BRANCH_WIDTH), gw, gb, lru_lam)
    return (res[0], res[1]) if need_ctx else (res[0], None)


NA_ROWS_PER_TILE = 4


def _na_tables(rows, rpt):
    kr = min(NA_WIN_R, rows)
    nkr = min(rows, rpt + kr - 1)
    variants, index, tile_variant, tile_kb = [], {}, [], []
    for t in range(rows // rpt):
        r0 = t * rpt
        kb = int(np.clip(r0 - kr // 2, 0, rows - nkr))
        geom = []
        for a in range(rpt):
            r = r0 + a
            rs = int(np.clip(r - kr // 2, 0, rows - kr))
            geom.append(tuple((kb + k - r + NA_WIN_R - 1) if rs <= kb + k < rs + kr else None
                              for k in range(nkr)))
        geom = tuple(geom)
        if geom not in index:
            index[geom] = len(variants)
            variants.append(geom)
        tile_variant.append(index[geom])
        tile_kb.append(kb)
    return nkr, np.array(tile_variant, np.int32), np.array(tile_kb, np.int32), variants


def _na_bias(rpb, variants):
    heads = rpb.shape[0]
    pad = GRID_W - NA_WIN_C
    padded = jnp.pad(rpb * LOG2E, ((0, 0), (0, 0), (pad, pad)))
    toep = jnp.stack([padded[..., GRID_W - 1 - qc:2 * GRID_W - 1 - qc] for qc in range(GRID_W)],
                     axis=2)
    col = np.arange(GRID_W)
    cs = np.clip(col - NA_WIN_C // 2, 0, GRID_W - NA_WIN_C)
    in_cols = (col[None, :] >= cs[:, None]) & (col[None, :] < cs[:, None] + NA_WIN_C)
    toep = jnp.where(in_cols, toep, NEG).astype(F32)
    outside = jnp.full((heads, GRID_W, GRID_W), NEG, F32)
    return jnp.stack([
        jnp.concatenate([
            jnp.concatenate([outside if dr is None else toep[:, dr] for dr in row], axis=-1)
            for row in geom], axis=-2)
        for geom in variants])


def _na_kernel(var_ref, kb_ref, q_ref, k_ref, v_ref, kc_ref, vc_ref, bias_ref, o_ref, *, nk):
    t = pl.program_id(1)
    start = pl.multiple_of(kb_ref[t] * GRID_W, GRID_W)
    for h in range(NA_HEADS):
        sl = slice(h * NA_HEAD_DIM, (h + 1) * NA_HEAD_DIM)
        qh = q_ref[:, sl]
        s_w = _dot_nt(qh, k_ref[pl.ds(start, nk), sl]) + bias_ref[h]
        s_c = _dot_nt(qh, kc_ref[:, sl])
        m = jnp.maximum(jnp.max(s_w, axis=-1, keepdims=True), jnp.max(s_c, axis=-1, keepdims=True))
        e_w = jnp.exp2(s_w - m)
        e_c = jnp.exp2(s_c - m)
        inv = 1.0 / (jnp.sum(e_w, axis=-1, keepdims=True) + jnp.sum(e_c, axis=-1, keepdims=True))
        o = (_dot(e_w.astype(BF16), v_ref[pl.ds(start, nk), sl])
             + _dot(e_c.astype(BF16), vc_ref[:, sl]))
        o_ref[:, sl] = (o * inv).astype(o_ref.dtype)


def _na_lat(lat, ctxp, rpb, bsz, seq, clen):
    n = bsz * seq
    rows = seq // GRID_W
    rpt = min(NA_ROWS_PER_TILE, rows)
    nkr, tile_variant, tile_kb, variants = _na_tables(rows, rpt)
    bias = _na_bias(rpb, variants)
    nt = rows // rpt
    rq, nk = rpt * GRID_W, nkr * GRID_W
    bw = BRANCH_WIDTH
    kern = functools.partial(_na_kernel, nk=nk)
    return pl.pallas_call(
        kern,
        grid_spec=pltpu.PrefetchScalarGridSpec(
            num_scalar_prefetch=2,
            grid=(bsz, nt),
            in_specs=[pl.BlockSpec((rq, bw), lambda b, t, var, kb: (b * nt + t, lat["nq"][1])),
                      pl.BlockSpec((seq, bw), lambda b, t, var, kb: (b, lat["nk"][1])),
                      pl.BlockSpec((seq, bw), lambda b, t, var, kb: (b, lat["nv"][1])),
                      pl.BlockSpec((clen, bw), lambda b, t, var, kb: (b, ctxp["nk"][1])),
                      pl.BlockSpec((clen, bw), lambda b, t, var, kb: (b, ctxp["nv"][1])),
                      pl.BlockSpec((None, NA_HEADS, rq, nk), lambda b, t, var, kb: (var[t], 0, 0, 0))],
            out_specs=pl.BlockSpec((rq, bw), lambda b, t, var, kb: (b * nt + t, 0))),
        out_shape=jax.ShapeDtypeStruct((n, bw), BF16),
        compiler_params=_params(("parallel", "arbitrary")),
        name="na_attn",
    )(jnp.asarray(tile_variant), jnp.asarray(tile_kb), lat["nq"][0], lat["nk"][0], lat["nv"][0],
      ctxp["nk"][0], ctxp["nv"][0], bias)


def _na_ctx_kernel(q_ref, k_ref, v_ref, o_ref):
    for h in range(NA_HEADS):
        sl = slice(h * NA_HEAD_DIM, (h + 1) * NA_HEAD_DIM)
        s = _dot_nt(q_ref[:, sl], k_ref[:, sl])
        e = jnp.exp2(s - jnp.max(s, axis=-1, keepdims=True))
        inv = 1.0 / jnp.sum(e, axis=-1, keepdims=True)
        o_ref[:, sl] = (_dot(e.astype(BF16), v_ref[:, sl]) * inv).astype(o_ref.dtype)


def _na_ctx(ctxp, bsz, clen):
    bw = BRANCH_WIDTH
    return pl.pallas_call(
        _na_ctx_kernel,
        grid=(bsz,),
        in_specs=[pl.BlockSpec((clen, bw), lambda b: (b, ctxp["nq"][1])),
                  pl.BlockSpec((clen, bw), lambda b: (b, ctxp["nk"][1])),
                  pl.BlockSpec((clen, bw), lambda b: (b, ctxp["nv"][1]))],
        out_specs=pl.BlockSpec((clen, bw), lambda b: (b, 0)),
        out_shape=jax.ShapeDtypeStruct((bsz * clen, bw), BF16),
        compiler_params=_params(("parallel",)),
        name="na_attn_ctx",
    )(ctxp["nq"][0], ctxp["nk"][0], ctxp["nv"][0])


def _merge_kernel(yd_ref, yl_ref, yn_ref, g0_ref, g1_ref, g2_ref, wb_ref, wo_ref, x_ref,
                  n1_ref, gate_ref, n2_ref, sc_ref, sh_ref, xm_ref, h2_ref):
    m = (g0_ref[...].astype(F32) * _dot(yd_ref[...], wb_ref[0])
         + g1_ref[...].astype(F32) * _dot(yl_ref[...], wb_ref[1])
         + g2_ref[...].astype(F32) * _dot(yn_ref[...], wb_ref[2]))
    mo = _dot(m.astype(BF16), wo_ref[...])
    xm = x_ref[...] + gate_ref[...] * _rms(mo, n1_ref[...])
    xm_ref[...] = xm
    h2 = _rms(xm, n2_ref[...]) * (1.0 + sc_ref[...]) + sh_ref[...]
    h2_ref[...] = h2.astype(h2_ref.dtype)


def _merge(yd, yl, yn, gates, wb, wo, x, n1, n2, mod, row_fn, tm, h2_dtype):
    n, d = x.shape
    bw = BRANCH_WIDTH
    p, gcol = gates
    resident = dict(pipeline_mode=pl.Buffered(1))
    return pl.pallas_call(
        _merge_kernel,
        grid=(n // tm,),
        in_specs=[pl.BlockSpec((tm, bw), lambda i: (i, 0)),
                  pl.BlockSpec((tm, bw), lambda i: (i, 0)),
                  pl.BlockSpec((tm, bw), lambda i: (i, 0)),
                  pl.BlockSpec((tm, d), lambda i: (i, gcol)),
                  pl.BlockSpec((tm, d), lambda i: (i, gcol + 1)),
                  pl.BlockSpec((tm, d), lambda i: (i, gcol + 2)),
                  pl.BlockSpec((3, bw, d), lambda i: (0, 0, 0), **resident),
                  pl.BlockSpec((d, d), lambda i: (0, 0), **resident),
                  pl.BlockSpec((tm, d), lambda i: (i, 0)),
                  _vec_spec(d),
                  _mod_spec(d, M_G1, row_fn),
                  _vec_spec(d),
                  _mod_spec(d, M_SC2, row_fn),
                  _mod_spec(d, M_SH2, row_fn)],
        out_specs=[pl.BlockSpec((tm, d), lambda i: (i, 0)),
                   pl.BlockSpec((tm, d), lambda i: (i, 0))],
        out_shape=[jax.ShapeDtypeStruct((n, d), F32),
                   jax.ShapeDtypeStruct((n, d), h2_dtype)],
        compiler_params=_params(("parallel",)),
        name="merge",
    )(yd, yl, yn, p, p, p, wb, wo, x, n1.reshape(1, d), mod, n2.reshape(1, d), mod, mod)


def _swiglu_step(x_ref, wg_ref, wu_ref, wd_ref, acc_ref, f):
    @pl.when(f == 0)
    def _():
        acc_ref[...] = jnp.zeros_like(acc_ref)
    xb = x_ref[...].astype(BF16)
    g = _dot(xb, wg_ref[...])
    u = _dot(xb, wu_ref[...])
    a = (g * jax.nn.sigmoid(g) * u).astype(BF16)
    acc_ref[...] += _dot(a, wd_ref[...])


def _ffn_dense_kernel(x_ref, wg_ref, wu_ref, wd_ref, xm_ref, gate_ref, n3_ref, *rest, nf, with_next):
    if with_next:
        n0_ref, sc_ref, sh_ref, xo_ref, hn_ref, acc_ref = rest
    else:
        xo_ref, acc_ref = rest
    f = pl.program_id(1)
    _swiglu_step(x_ref, wg_ref, wu_ref, wd_ref, acc_ref, f)

    @pl.when(f == nf - 1)
    def _():
        xo = xm_ref[...] + gate_ref[...] * _rms(acc_ref[...], n3_ref[...])
        xo_ref[...] = xo
        if with_next:
            hn = _rms(xo, n0_ref[...]) * (1.0 + sc_ref[...]) + sh_ref[...]
            hn_ref[...] = hn.astype(hn_ref.dtype)


def _ffn_dense(h2, wg, wu, wd, xm, n3, mod, row_fn, tm, tf, nxt):
    n, d = h2.shape
    fp = wg.shape[1]
    nf = fp // tf
    with_next = nxt is not None
    in_specs = [pl.BlockSpec((tm, d), lambda i, f: (i, 0)),
                pl.BlockSpec((d, tf), lambda i, f: (0, f)),
                pl.BlockSpec((d, tf), lambda i, f: (0, f)),
                pl.BlockSpec((tf, d), lambda i, f: (f, 0)),
                pl.BlockSpec((tm, d), lambda i, f: (i, 0)),
                _mod_spec(d, M_G2, row_fn),
                _vec_spec(d)]
    args = [h2, wg, wu, wd, xm, mod, n3.reshape(1, d)]
    out_specs = [pl.BlockSpec((tm, d), lambda i, f: (i, 0))]
    out_shape = [jax.ShapeDtypeStruct((n, d), F32)]
    if with_next:
        n0, mod_next = nxt
        in_specs += [_vec_spec(d), _mod_spec(d, M_SC1, row_fn), _mod_spec(d, M_SH1, row_fn)]
        args += [n0.reshape(1, d), mod_next, mod_next]
        out_specs.append(pl.BlockSpec((tm, d), lambda i, f: (i, 0)))
        out_shape.append(jax.ShapeDtypeStruct((n, d), BF16))
    kern = functools.partial(_ffn_dense_kernel, nf=nf, with_next=with_next)
    res = pl.pallas_call(
        kern,
        grid=(n // tm, nf),
        in_specs=in_specs,
        out_specs=out_specs,
        out_shape=out_shape,
        scratch_shapes=[pltpu.VMEM((tm, d), F32)],
        compiler_params=_params(("parallel", "arbitrary")),
        name="ffn_dense",
    )(*args)
    return (res[0], res[1]) if with_next else (res[0], None)


def _row_copy(src_hbm, src_row, dst_ref, dst_row, sem):
    return pltpu.make_async_copy(src_hbm.at[pl.ds(src_row, 1)], dst_ref.at[pl.ds(dst_row, 1)], sem)


def _ffn_grouped_kernel(te_ref, nt_ref, used_ref, src_ref, x_hbm, wg_ref, wu_ref, wd_ref, o_ref,
                        rows_ref, xb_ref, sem, *, nf, tm, tm_small, grid_tiles):
    i = pl.program_id(0)
    f = pl.program_id(1)
    nt = nt_ref[0]
    slot = i % 2
    issue_steps = max(1, nf - 2)
    rows_per_step = tm // issue_steps

    def start_row(tile, slot_, r):
        _row_copy(x_hbm, src_ref[tile * tm + r], rows_ref.at[slot_], r, sem.at[slot_]).start()

    def wait_rows(slot_):
        pltpu.make_async_copy(x_hbm.at[pl.ds(0, tm)], rows_ref.at[slot_], sem.at[slot_]).wait()

    @pl.when((i == 0) & (f == 0))
    def _():
        def body(r, c):
            start_row(0, 0, r)
            return c
        lax.fori_loop(0, tm, body, 0)

    @pl.when((i < nt) & (f == 0))
    def _():
        wait_rows(slot)
        xb_ref[...] = rows_ref[slot].astype(BF16)
        o_ref[...] = jnp.zeros_like(o_ref)

    def step(request_rows, m):
        if request_rows:
            nxt = jnp.minimum(i + 1, grid_tiles - 1)
            for j in range(rows_per_step):
                start_row(nxt, 1 - slot, f * rows_per_step + j)
        xb = xb_ref[:m, :]
        g = _dot(xb, wg_ref[...])
        u = _dot(xb, wu_ref[...])
        a = (g * jax.nn.sigmoid(g) * u).astype(BF16)
        o_ref[:m, :] += _dot(a, wd_ref[...])

    small = used_ref[i] <= tm_small
    for request, in_range in ((True, f < issue_steps), (False, f >= issue_steps)):
        if request or issue_steps < nf:
            pl.when((i < nt) & in_range & small)(functools.partial(step, request, tm_small))
            pl.when((i < nt) & in_range & jnp.logical_not(small))(functools.partial(step, request, tm))

    @pl.when((i == nt - 1) & (f == nf - 1))
    def _():
        wait_rows(1 - slot)

    @pl.when((i >= nt) & (f == nf - 1))
    def _():
        o_ref[...] = jnp.zeros_like(o_ref)


def _ffn_grouped(h2, src, wg, wu, wd, tile_expert, n_tiles, tile_used, p_max, tm, tf):
    d = h2.shape[1]
    fe = wg.shape[2]
    nf = fe // tf
    assert tm % max(1, nf - 2) == 0, (tm, nf)

    def fidx(i, f, nt):
        return jnp.where(i < nt[0], f, nf - 1)

    tm_small = min(tm, -(-(tm // 2) // 16) * 16)
    kern = functools.partial(_ffn_grouped_kernel, nf=nf, tm=tm, tm_small=tm_small,
                             grid_tiles=p_max // tm)
    return pl.pallas_call(
        kern,
        grid_spec=pltpu.PrefetchScalarGridSpec(
            num_scalar_prefetch=4,
            grid=(p_max // tm, nf),
            in_specs=[pl.BlockSpec(memory_space=pl.ANY),
                      pl.BlockSpec((None, d, tf), lambda i, f, te, nt, *_: (te[i], 0, fidx(i, f, nt))),
                      pl.BlockSpec((None, d, tf), lambda i, f, te, nt, *_: (te[i], 0, fidx(i, f, nt))),
                      pl.BlockSpec((None, tf, d), lambda i, f, te, nt, *_: (te[i], fidx(i, f, nt), 0))],
            out_specs=pl.BlockSpec((tm, d), lambda i, f, *_: (i, 0)),
            scratch_shapes=[pltpu.VMEM((2, tm, d), h2.dtype),
                            pltpu.VMEM((tm, d), BF16),
                            pltpu.SemaphoreType.DMA((2,))]),
        out_shape=jax.ShapeDtypeStruct((p_max, d), F32),
        compiler_params=pltpu.CompilerParams(dimension_semantics=("arbitrary", "arbitrary"),
                                             vmem_limit_bytes=VMEM_LIMIT,
                                             disable_bounds_checks=True),
        name="ffn_grouped",
    )(tile_expert, n_tiles, tile_used, src, h2, wg, wu, wd)


def _router_kernel(h_ref, wr_ref, o_ref, cnt_ref, carry_ref, *, n_experts):
    i = pl.program_id(0)

    @pl.when(i == 0)
    def _():
        carry_ref[...] = jnp.zeros_like(carry_ref)

    logits = _dot(h_ref[...].astype(BF16), wr_ref[...])
    tm = logits.shape[0]
    lane_i = lax.broadcasted_iota(I32, logits.shape, 1)
    lane = lane_i.astype(F32)
    logits = jnp.where(lane_i < n_experts, logits, -jnp.inf)
    m1 = jnp.max(logits, axis=-1, keepdims=True)
    i1 = jnp.min(jnp.where(logits == m1, lane, float(LANES)), axis=-1, keepdims=True)
    rest = jnp.where(lane == i1, -jnp.inf, logits)
    m2 = jnp.max(rest, axis=-1, keepdims=True)
    i2 = jnp.min(jnp.where(rest == m2, lane, float(LANES)), axis=-1, keepdims=True)
    e21 = jnp.exp(m2 - m1)
    w1 = 1.0 / (1.0 + e21)
    w2 = e21 * w1

    sel1 = lane == i1
    sel2 = lane == i2
    onehot = jnp.where(sel1 | sel2, 1.0, 0.0).astype(BF16)
    r = lax.broadcasted_iota(I32, (tm, tm), 0)
    c = lax.broadcasted_iota(I32, (tm, tm), 1)
    strict_lower = jnp.where(c < r, 1.0, 0.0).astype(BF16)
    before = _dot(strict_lower, onehot) + carry_ref[0:1, :]
    r1 = jnp.sum(jnp.where(sel1, before, 0.0), axis=-1, keepdims=True)
    r2 = jnp.sum(jnp.where(sel2, before, 0.0), axis=-1, keepdims=True)
    total = carry_ref[0:1, :] + jnp.sum(onehot.astype(F32), axis=0, keepdims=True)
    carry_ref[...] = jnp.broadcast_to(total, carry_ref.shape)
    cnt_ref[...] = jnp.broadcast_to(total, cnt_ref.shape)

    packed = jnp.where(lane_i == 0, i1, 0.0)
    packed = jnp.where(lane_i == 1, i2, packed)
    packed = jnp.where(lane_i == 2, r1, packed)
    packed = jnp.where(lane_i == 3, r2, packed)
    packed = jnp.where(lane_i == 4, w1, packed)
    packed = jnp.where(lane_i == 5, w2, packed)
    o_ref[...] = packed


def _router(h2, w_router, tm):
    n, d = h2.shape
    n_experts = w_router.shape[1]
    wr = jnp.pad(w_router, ((0, 0), (0, LANES - n_experts))).astype(BF16)
    kern = functools.partial(_router_kernel, n_experts=n_experts)
    return pl.pallas_call(
        kern,
        grid=(n // tm,),
        in_specs=[pl.BlockSpec((tm, d), lambda i: (i, 0)),
                  pl.BlockSpec((d, LANES), lambda i: (0, 0))],
        out_specs=[pl.BlockSpec((tm, LANES), lambda i: (i, 0)),
                   pl.BlockSpec((SUBLANES, LANES), lambda i: (0, 0))],
        out_shape=[jax.ShapeDtypeStruct((n, LANES), F32),
                   jax.ShapeDtypeStruct((SUBLANES, LANES), F32)],
        scratch_shapes=[pltpu.VMEM((SUBLANES, LANES), F32)],
        compiler_params=_params(("arbitrary",)),
        name="router",
    )(h2, wr)


def _combine_kernel(d1_ref, d2_ref, y_hbm, r_ref, xm_ref, gate_ref, n3_ref, xo_ref, buf_ref, sem, *,
                    tm, n_tiles):
    i = pl.program_id(0)
    slot = i % 2
    dest = (d1_ref, d2_ref)

    def start_rows(tile, slot_, r, priority=0):
        for k in range(TOP_K):
            _row_copy(y_hbm, dest[k][tile * tm + r], buf_ref.at[slot_, k], r,
                      sem.at[slot_]).start(priority=priority)

    def wait_rows(slot_):
        for k in range(TOP_K):
            pltpu.make_async_copy(y_hbm.at[pl.ds(0, tm)], buf_ref.at[slot_, k], sem.at[slot_]).wait()

    @pl.when(i == 0)
    def _():
        def body(r, c):
            start_rows(0, 0, r)
            return c
        lax.fori_loop(0, tm, body, 0)

    wait_rows(slot)
    nxt = jnp.minimum(i + 1, n_tiles - 1)
    for r in range(tm):
        start_rows(nxt, 1 - slot, r, priority=r % 2)
    route = r_ref[...]
    y = route[:, 4:5] * buf_ref[slot, 0] + route[:, 5:6] * buf_ref[slot, 1]
    xo_ref[...] = xm_ref[...] + gate_ref[...] * _rms(y, n3_ref[...])

    @pl.when(i == n_tiles - 1)
    def _():
        wait_rows(1 - slot)


def _combine(yo, dest1, dest2, route, xm, n3, mod, row_fn, tm):
    n, d = xm.shape
    kern = functools.partial(_combine_kernel, tm=tm, n_tiles=n // tm)
    return pl.pallas_call(
        kern,
        grid_spec=pltpu.PrefetchScalarGridSpec(
            num_scalar_prefetch=2,
            grid=(n // tm,),
            in_specs=[pl.BlockSpec(memory_space=pl.ANY),
                      pl.BlockSpec((tm, LANES), lambda i, *_: (i, 0)),
                      pl.BlockSpec((tm, d), lambda i, *_: (i, 0)),
                      _mod_spec(d, M_G2, row_fn),
                      _vec_spec(d)],
            out_specs=pl.BlockSpec((tm, d), lambda i, *_: (i, 0)),
            scratch_shapes=[pltpu.VMEM((2, TOP_K, tm, d), F32), pltpu.SemaphoreType.DMA((2,))]),
        out_shape=jax.ShapeDtypeStruct((n, d), F32),
        compiler_params=pltpu.CompilerParams(dimension_semantics=("arbitrary",),
                                             vmem_limit_bytes=VMEM_LIMIT,
                                             disable_bounds_checks=True),
        name="moe_combine",
    )(dest1, dest2, yo, route, xm, mod, n3.reshape(1, d))


def _moe(h2, xm, w_router, wg, wu, wd, n3, mod, row_fn, tm_g, tf):
    n, d = h2.shape
    n_experts = w_router.shape[1]
    route, counts = _router(h2, w_router, min(512, n))
    e1 = route[:, 0].astype(I32)
    e2 = route[:, 1].astype(I32)
    r1 = route[:, 2].astype(I32)
    r2 = route[:, 3].astype(I32)
    cnt = counts[0, :n_experts].astype(I32)
    padded = ((cnt + tm_g - 1) // tm_g) * tm_g
    ends = jnp.cumsum(padded)
    offs = ends - padded
    dest1 = offs[e1] + r1
    dest2 = offs[e2] + r2
    p_max = ((TOP_K * n + n_experts * (tm_g - 1)) // tm_g) * tm_g
    n_rows = ends[-1:]
    tok = jnp.arange(n, dtype=I32)
    src = jnp.zeros((p_max,), I32).at[dest1].set(tok).at[dest2].set(tok)
    tile_start = jnp.arange(p_max // tm_g, dtype=I32) * tm_g
    tile_expert = jnp.minimum(jnp.sum(tile_start[:, None] >= ends[None, :], axis=1),
                              n_experts - 1).astype(I32)
    last_expert = tile_expert[jnp.maximum(n_rows[0] // tm_g - 1, 0)]
    tile_expert = jnp.where(tile_start < n_rows[0], tile_expert, last_expert)
    tile_used = jnp.clip((offs + cnt)[tile_expert] - tile_start, 0, tm_g).astype(I32)
    yo = _ffn_grouped(h2, src, wg, wu, wd, tile_expert, n_rows // tm_g, tile_used, p_max, tm_g, tf)
    return _combine(yo, dest1, dest2, route, xm, n3, mod, row_fn, min(256, n))


def _rope_tables(seq):
    inv = ROPE_BASE ** (-jnp.arange(ROPE_FREQ, dtype=F32) / ROPE_FREQ)
    t = jnp.arange(seq, dtype=I32)
    pos = jnp.stack([t // GRID_W, t % GRID_W], axis=-1).astype(F32)
    ang = pos[:, :, None] * inv
    cos, sin = jnp.cos(ang), jnp.sin(ang)
    cos = jnp.concatenate([cos, cos], axis=-1).reshape(seq, DIFF_HEAD_DIM)
    sin = jnp.concatenate([-sin, sin], axis=-1).reshape(seq, DIFF_HEAD_DIM)
    reps = LANES // DIFF_HEAD_DIM
    return jnp.tile(cos, (1, reps)), jnp.tile(sin, (1, reps))


def _pad_cols(w, mult):
    pad = (-w.shape[-1]) % mult
    return jnp.pad(w, [(0, 0)] * (w.ndim - 1) + [(0, pad)]) if pad else w


def _pad_rows(w, mult):
    pad = (-w.shape[-2]) % mult
    return jnp.pad(w, [(0, 0)] * (w.ndim - 2) + [(0, pad), (0, 0)]) if pad else w


def kernel(x, c, ctx, c_ctx, norm_g, w_ada, b_ada, w_in, diff_lambda, diff_subln_g, lru_conv_w, lru_conv_b, lru_gate_w, lru_gate_b, lru_lambda, na_rpb, w_branch, w_merge, b_merge, w_out, ffn_w_gate, ffn_w_up, ffn_w_down, moe_w_router, moe_w_gate, moe_w_up, moe_w_down):
    bsz, seq, d = x.shape
    clen = ctx.shape[1]
    depth = w_in.shape[0]
    n, nc = bsz * seq, bsz * clen
    x_lat = x.reshape(n, d)
    x_ctx = ctx.reshape(nc, d)

    mr = -(-(bsz + 1) // SUBLANES) * SUBLANES
    cvec = jnp.concatenate([c, c_ctx[None], jnp.zeros((mr - bsz - 1, d), F32)], axis=0)
    mod_all = _ada(cvec, w_ada, b_ada).reshape(depth, mr, 1, 6 * d)
    cos_t, sin_t = _rope_tables(seq)

    tm_lat, tm_ctx = min(1024, seq), min(1024, nc, seq)
    tm_mix_lat, tm_mix_ctx = min(256, seq), min(256, nc)
    tm_ffn_lat, tm_ffn_ctx = min(512, seq), min(512, nc)
    tf = min(512, d)
    tf_moe = min(MOE_TF, d)
    tq = min(DIFF_TQ, seq)

    def lat_row(tile_rows):
        per_batch = seq // tile_rows
        return lambda i: i // per_batch

    def ctx_row(i):
        return bsz

    h_lat = _prenorm(x_lat, norm_g[0, 0], mod_all[0], lat_row(tm_lat), tm_lat)
    h_ctx = _prenorm(x_ctx, norm_g[0, 0], mod_all[0], ctx_row, tm_ctx)

    rest_names = ("dv", "lx", "lg", "nq", "nk", "nv")
    part_scale = [1.0] * N_IN_PARTS
    part_scale[P_DQ], part_scale[P_NQ] = DIFF_QSCALE, NA_QSCALE
    col_scale = jnp.repeat(jnp.asarray(part_scale, F32), BRANCH_WIDTH)[None]
    scale_qk, scale_rest = col_scale[:, :2 * BRANCH_WIDTH], col_scale[:, 2 * BRANCH_WIDTH:]
    diff_steps = bsz * DIFF_HEADS * (seq // tq)
    moe_src = (moe_w_gate, moe_w_up, moe_w_down)

    def halves(w):
        return w.reshape(2, -1, w.shape[-1])

    def side_cast_ok(mi):
        return moe_w_router.shape[-1] % 2 == 0 and all(
            halves(w[mi]).shape[1] % (16 * diff_steps) == 0 for w in moe_src)

    moe_bf16 = {}
    for l in range(depth):
        need_ctx = l < depth - 1
        mod = mod_all[l]
        lam_init = 0.8 - 0.6 * math.exp(-0.3 * l)
        w_qk = w_in[l][:, :2 * BRANCH_WIDTH].astype(BF16)
        w_rest = w_in[l][:, 2 * BRANCH_WIDTH:].astype(BF16)
        w_gate = w_merge[l].astype(BF16)
        b_gate = b_merge[l][None]

        def project(h, tm, rope):
            qk = _proj(h, w_qk, scale_qk, cos_t, sin_t, seq, tm, PROJ_TN_ROPE if rope else PROJ_TN,
                       "rope" if rope else "plain")
            rest = _proj(h, w_rest, scale_rest, cos_t, sin_t, seq, tm, PROJ_TN, "plain")
            parts = {"dq": (qk, 0), "dk": (qk, 1)}
            parts.update({name: (rest, k) for k, name in enumerate(rest_names)})
            return parts

        lat = project(h_lat, tm_lat, True)
        ctxp = project(h_ctx, tm_ctx, False)
        gate_lat = _proj(h_lat, w_gate, b_gate, cos_t, sin_t, seq, tm_lat, PROJ_TN, "gate")

        mi, half = l // 2, l % 2
        jobs = ()
        if mi < moe_w_gate.shape[0] and side_cast_ok(mi):
            jobs = tuple((halves(w[mi]), half, moe_bf16[mi][k] if half else None)
                         for k, w in enumerate(moe_src))
        y_diff, cast = _diff_attn_lat(lat, ctxp, diff_lambda[l], diff_subln_g[l], lam_init,
                                      bsz, seq, clen, tq, jobs)
        if jobs:
            moe_bf16[mi] = cast
        y_lru, y_lru_c = _lru(lat, ctxp, lru_conv_w[l], lru_conv_b[l], lru_gate_w[l],
                              lru_gate_b[l], lru_lambda[l], bsz, seq, clen, need_ctx)
        y_na = _na_lat(lat, ctxp, na_rpb[l], bsz, seq, clen)

        wb = w_branch[l].astype(BF16)
        wo = w_out[l].astype(BF16)
        is_moe = l % 2 == 1
        h2_dtype = F32 if is_moe else BF16
        xm_lat, h2_lat = _merge(y_diff, y_lru, y_na, (gate_lat, 0), wb, wo, x_lat, norm_g[l, 1],
                                norm_g[l, 2], mod, lat_row(tm_mix_lat), tm_mix_lat, h2_dtype)
        if need_ctx:
            y_diff_c = _diff_attn_ctx(ctxp, diff_lambda[l], diff_subln_g[l], lam_init, bsz, clen)
            y_na_c = _na_ctx(ctxp, bsz, clen)
            gate_ctx = _proj(h_ctx, w_gate, b_gate, cos_t, sin_t, seq, tm_ctx, PROJ_TN, "gate")
            xm_ctx, h2_ctx = _merge(y_diff_c, y_lru_c, y_na_c, (gate_ctx, 0), wb, wo, x_ctx,
                                    norm_g[l, 1], norm_g[l, 2], mod, ctx_row, tm_mix_ctx, h2_dtype)

        nxt = (norm_g[l + 1, 0], mod_all[l + 1]) if need_ctx else None
        i = l // 2
        if is_moe:
            if i in moe_bf16:
                wg, wu, wd = (c.reshape(w.shape[1:]) for c, w in zip(moe_bf16[i], moe_src))
            else:
                wg, wu, wd = (w[i].astype(BF16) for w in moe_src)
            tm_g = min(MOE_TILE, n)
            x_lat = _moe(h2_lat, xm_lat, moe_w_router[i], wg, wu, wd, norm_g[l, 3], mod,
                         lat_row(min(256, n)), tm_g, tf_moe)
            if need_ctx:
                x_ctx = _moe(h2_ctx, xm_ctx, moe_w_router[i], wg, wu, wd, norm_g[l, 3], mod,
                             ctx_row, min(MOE_TILE, nc), tf_moe)
                h_lat = _prenorm(x_lat, nxt[0], nxt[1], lat_row(tm_lat), tm_lat)
                h_ctx = _prenorm(x_ctx, nxt[0], nxt[1], ctx_row, tm_ctx)
        else:
            wg = _pad_cols(ffn_w_gate[i], tf).astype(BF16)
            wu = _pad_cols(ffn_w_up[i], tf).astype(BF16)
            wd = _pad_rows(ffn_w_down[i], tf).astype(BF16)
            x_lat, h_lat = _ffn_dense(h2_lat, wg, wu, wd, xm_lat, norm_g[l, 3], mod,
                                      lat_row(tm_ffn_lat), tm_ffn_lat, tf, nxt)
            if need_ctx:
                x_ctx, h_ctx = _ffn_dense(h2_ctx, wg, wu, wd, xm_ctx, norm_g[l, 3], mod,
                                          ctx_row, tm_ffn_ctx, tf, nxt)
    return x_lat.reshape(bsz, seq, d)
```
